```python
import math, functools
import jax, jax.numpy as jnp
from jax import lax
import numpy as np

D_MODEL = 2048
BATCH = 4
SEQ = 2048
DEPTH = 4
DEC_BATCH = 8
DEC_SEQ = 1
PAST_LEN = 16384
PAGE_SIZE = 128

N_HEADS_A = 8
HEAD_DIM = 128
WIDTH_A = N_HEADS_A * HEAD_DIM
N_IDX_HEADS = 8
IDX_DIM = 64
TOPK_MAX = 256
Q_BLOCK = 128
ATTN_SCALE = HEAD_DIM ** -0.5
IDX_WEIGHT_SCALE = (N_IDX_HEADS ** -0.5) * (IDX_DIM ** -0.5)
NEG_LARGE = -1e30
NUM_BUCKETS = 32
MAX_DISTANCE = 128
N_HEADS_B = 8
KEY_DIM_B = 128
VAL_DIM_B = 128
WIDTH_B = N_HEADS_B * VAL_DIM_B
CHUNK = 16
D_FF = ((8 * D_MODEL // 3 + 255) // 256) * 256
EPS = 1e-6
IN_WIDTHS = (WIDTH_A, WIDTH_A, WIDTH_A, N_IDX_HEADS * IDX_DIM, IDX_DIM, N_IDX_HEADS,
             N_HEADS_B * KEY_DIM_B, N_HEADS_B * KEY_DIM_B, WIDTH_B, WIDTH_B, D_MODEL, D_MODEL)
IN_COLS = sum(IN_WIDTHS)

kernel_name = 'hybrid_dsa_hgrn2_decoder_step'


def rms_norm(x, g):
    xf = x.astype(jnp.float32)
    y = xf * lax.rsqrt(jnp.mean(xf * xf, axis=-1, keepdims=True) + EPS)
    return (y * g.astype(jnp.float32)).astype(x.dtype)


def split_cols(z):
    out = []
    off = 0
    for w in IN_WIDTHS:
        out.append(z[..., off:off + w])
        off += w
    return out


def topk_count(n_keys):
    return min(TOPK_MAX, n_keys // 4)


def t5_bucket(n):
    max_exact = NUM_BUCKETS // 2
    nf = jnp.maximum(n, 1).astype(jnp.float32)
    large = max_exact + (jnp.log(nf / max_exact) / math.log(MAX_DISTANCE / max_exact)
                         * (NUM_BUCKETS - max_exact)).astype(jnp.int32)
    large = jnp.minimum(large, NUM_BUCKETS - 1)
    return jnp.where(n < max_exact, n, large)


def index_scores(q_idx, w_idx, k_idx):
    s = jax.nn.relu(jnp.einsum('...qhd,...sd->...qhs', q_idx, k_idx).astype(jnp.float32))
    return jnp.einsum('...qhs,...qh->...qs', s, w_idx.astype(jnp.float32))


def sparse_attend(q, k_sel, v_sel, t_pos, s_pos, valid, rel_bias):
    bias = rel_bias[t5_bucket(jnp.maximum(t_pos[..., None] - s_pos, 0))]
    logits = (jnp.einsum('...qhd,...qkhd->...qhk', q, k_sel).astype(jnp.float32) * ATTN_SCALE
              + jnp.moveaxis(bias, -1, -2).astype(jnp.float32))
    logits = jnp.where(valid[..., None, :], logits, NEG_LARGE)
    p = jax.nn.softmax(logits, axis=-1).astype(v_sel.dtype)
    return jnp.einsum('...qhk,...qkhd->...qhd', p, v_sel)


def prompt_sparse_attention(q, k, v, q_idx, w_idx, k_idx, rel_bias):
    B, S = q.shape[:2]
    n_blk = S // Q_BLOCK
    topk = topk_count(S)

    def to_blocks(a):
        return a.reshape((B * n_blk, Q_BLOCK) + a.shape[2:])

    b_ids = jnp.repeat(jnp.arange(B, dtype=jnp.int32), n_blk)
    blk_ids = jnp.tile(jnp.arange(n_blk, dtype=jnp.int32), B)
    s_all = jnp.arange(S, dtype=jnp.int32)

    def one_block(args):
        qb, qib, wb, b, j = args
        t_pos = j * Q_BLOCK + jnp.arange(Q_BLOCK, dtype=jnp.int32)
        sc = index_scores(qib, wb, k_idx[b])
        sc = jnp.where(s_all[None, :] <= t_pos[:, None], sc, NEG_LARGE)
        _, sel = lax.top_k(sc, topk)
        valid = sel <= t_pos[:, None]
        return sparse_attend(qb, k[b, sel], v[b, sel], t_pos, sel, valid, rel_bias)

    out = lax.map(one_block, (to_blocks(q), to_blocks(q_idx), to_blocks(w_idx), b_ids, blk_ids))
    return out.reshape(B, S, WIDTH_A)


def sample_sparse_attention(q, k_new, v_new, q_idx, w_idx, kidx_new,
                            cache_k, cache_v, cache_kidx, layer, page_table, rel_bias):
    DB, DS = q.shape[:2]
    n_pages = page_table.shape[1]
    past = n_pages * PAGE_SIZE
    n_keys = past + DS
    topk = topk_count(n_keys)
    kidx_past = cache_kidx[layer, page_table].reshape(DB, past, IDX_DIM)
    kidx_all = jnp.concatenate([kidx_past, kidx_new.astype(kidx_past.dtype)], axis=1)
    t_pos = past + jnp.arange(DS, dtype=jnp.int32)
    sc = index_scores(q_idx, w_idx, kidx_all)
    s_all = jnp.arange(n_keys, dtype=jnp.int32)
    sc = jnp.where(s_all[None, None, :] <= t_pos[None, :, None], sc, NEG_LARGE)
    _, sel = lax.top_k(sc, topk)
    valid = sel <= t_pos[None, :, None]
    from_past = (sel < past)[..., None, None]
    bidx = jnp.arange(DB)[:, None, None]
    sp = jnp.minimum(sel, past - 1)
    phys = page_table[bidx, sp // PAGE_SIZE]
    off = sp % PAGE_SIZE
    sn = jnp.clip(sel - past, 0, DS - 1)
    k_sel = jnp.where(from_past, cache_k[layer, phys, off], k_new[bidx, sn].astype(cache_k.dtype))
    v_sel = jnp.where(from_past, cache_v[layer, phys, off], v_new[bidx, sn].astype(cache_v.dtype))
    out = sparse_attend(q, k_sel.astype(q.dtype), v_sel.astype(q.dtype), t_pos, sel, valid, rel_bias)
    return out.reshape(DB, DS, WIDTH_A)


def hgrn2_recurrence(q, k, logf, i, s0):
    f32 = jnp.float32
    B, T, H, _ = q.shape
    DV = i.shape[-1]
    pad = (-T) % CHUNK
    q, k, logf, i = [jnp.pad(a.astype(f32), ((0, 0), (0, pad), (0, 0), (0, 0))) for a in (q, k, logf, i)]
    n = (T + pad) // CHUNK
    q, k, logf, i = [a.reshape(B, n, CHUNK, H, a.shape[-1]) for a in (q, k, logf, i)]
    b = jnp.cumsum(logf, axis=2)
    b_last = b[:, :, -1:]
    q_dec = q * jnp.exp(b)
    k_end = k * jnp.exp(b_last - b)
    causal = jnp.tril(jnp.ones((CHUNK, CHUNK), dtype=bool))[None, None, :, :, None, None]
    diff = b[:, :, :, None] - b[:, :, None, :]
    dec = jnp.where(causal, jnp.exp(jnp.where(causal, diff, 0.0)), 0.0)
    a = jnp.einsum('bnchd,bnshd,bncshd->bnhcs', q, k, dec)
    o_intra = jnp.einsum('bnhcs,bnshv->bnchv', a, i)
    decay = jnp.exp(b_last[:, :, 0])

    def step(S, xs):
        qc, kc, ic, dc = xs
        o = jnp.einsum('bchd,bhdv->bchv', qc, S)
        S = dc[..., None] * S + jnp.einsum('bchd,bchv->bhdv', kc, ic)
        return S, o

    s_fin, o_inter = lax.scan(step, s0.astype(f32),
                              (jnp.moveaxis(q_dec, 1, 0), jnp.moveaxis(k_end, 1, 0),
                               jnp.moveaxis(i, 1, 0), jnp.moveaxis(decay, 1, 0)))
    o = (o_intra + jnp.moveaxis(o_inter, 0, 1)).reshape(B, n * CHUNK, H, DV)[:, :T]
    return o, s_fin


def trunk_layer(x, attend_fn, s0, norm_g, w_in_l, q_g, k_g, kidx_g, lb_l, out_g,
                w_up_a_l, w_up_b_l, w_out_l, ffn_g, w_gate_l, w_upf_l, w_down_l):
    B, T, _ = x.shape
    xn = rms_norm(x, norm_g)
    (zq, zk, zv, zqi, zki, zw, zhq, zhf, zhi, zhg, zga, zgb) = split_cols(xn @ w_in_l)
    q = rms_norm(zq.reshape(B, T, N_HEADS_A, HEAD_DIM), q_g)
    k = rms_norm(zk.reshape(B, T, N_HEADS_A, HEAD_DIM), k_g)
    v = zv.reshape(B, T, N_HEADS_A, HEAD_DIM)
    q_idx = zqi.reshape(B, T, N_IDX_HEADS, IDX_DIM)
    k_idx = rms_norm(zki, kidx_g)
    w_idx = zw * IDX_WEIGHT_SCALE
    o_a = attend_fn(q, k, v, q_idx, w_idx, k_idx)
    hq = jax.nn.silu(zhq).reshape(B, T, N_HEADS_B, KEY_DIM_B)
    zf = zhf.astype(jnp.float32).reshape(B, T, N_HEADS_B, KEY_DIM_B)
    lb = lb_l.reshape(N_HEADS_B, KEY_DIM_B)
    f_gate = lb + (1.0 - lb) * jax.nn.sigmoid(zf)
    logf = jnp.log(f_gate)
    hk = 1.0 - f_gate
    hi = zhi.reshape(B, T, N_HEADS_B, VAL_DIM_B)
    o_b, s_new = hgrn2_recurrence(hq, hk, logf, hi, s0)
    o_b = (rms_norm(o_b.astype(x.dtype), out_g)
           * jax.nn.silu(zhg.reshape(B, T, N_HEADS_B, VAL_DIM_B))).reshape(B, T, WIDTH_B)
    mixed = jax.nn.sigmoid(zga) * (o_a @ w_up_a_l) + jax.nn.sigmoid(zgb) * (o_b @ w_up_b_l)
    x = x + mixed @ w_out_l
    hn = rms_norm(x, ffn_g)
    x = x + (jax.nn.silu(hn @ w_gate_l) * (hn @ w_upf_l)) @ w_down_l
    return x, k, v, k_idx, s_new.astype(x.dtype)


def setup_inputs(seed: int = 0) -> dict:
    key = jax.random.key(seed)
    ks = jax.random.split(key, 24)
    f32 = jnp.float32
    n_pages = PAST_LEN // PAGE_SIZE
    n_pool = (DEC_BATCH * n_pages * 5 + 3) // 4

    def nrm(k, shape, scale):
        return jax.random.normal(k, shape, f32) * scale

    page_table = jax.random.permutation(ks[6], n_pool)[:DEC_BATCH * n_pages].reshape(DEC_BATCH, n_pages).astype(jnp.int32)
    return {
        'x_prompt': nrm(ks[0], (BATCH, SEQ, D_MODEL), 1.0),
        'x_sample': nrm(ks[1], (DEC_BATCH, DEC_SEQ, D_MODEL), 1.0),
        'cache_k': nrm(ks[2], (DEPTH, n_pool, PAGE_SIZE, N_HEADS_A, HEAD_DIM), 1.0),
        'cache_v': nrm(ks[3], (DEPTH, n_pool, PAGE_SIZE, N_HEADS_A, HEAD_DIM), 1.0),
        'cache_kidx': nrm(ks[4], (DEPTH, n_pool, PAGE_SIZE, IDX_DIM), 1.0),
        'state_hgrn': nrm(ks[5], (DEPTH, DEC_BATCH, N_HEADS_B, KEY_DIM_B, VAL_DIM_B), 0.5),
        'page_table': page_table,
        'norm_mix_g': 1.0 + nrm(ks[7], (DEPTH, D_MODEL), 0.05),
        'w_in': nrm(ks[8], (DEPTH, D_MODEL, IN_COLS), D_MODEL ** -0.5),
        'q_norm_g': 1.0 + nrm(ks[9], (DEPTH, HEAD_DIM), 0.05),
        'k_norm_g': 1.0 + nrm(ks[10], (DEPTH, HEAD_DIM), 0.05),
        'kidx_norm_g': 1.0 + nrm(ks[11], (DEPTH, IDX_DIM), 0.05),
        'rel_bias': nrm(ks[12], (NUM_BUCKETS, N_HEADS_A), 0.5),
        'hgrn_lb': nrm(ks[13], (DEPTH, N_HEADS_B * KEY_DIM_B), 0.1),
        'hgrn_out_g': 1.0 + nrm(ks[14], (DEPTH, VAL_DIM_B), 0.05),
        'w_up_a': nrm(ks[15], (DEPTH, WIDTH_A, D_MODEL), WIDTH_A ** -0.5),
        'w_up_b': nrm(ks[16], (DEPTH, WIDTH_B, D_MODEL), WIDTH_B ** -0.5),
        'w_out': nrm(ks[17], (DEPTH, D_MODEL, D_MODEL), D_MODEL ** -0.5),
        'norm_ffn_g': 1.0 + nrm(ks[18], (DEPTH, D_MODEL), 0.05),
        'w_ffn_gate': nrm(ks[19], (DEPTH, D_MODEL, D_FF), D_MODEL ** -0.5),
        'w_ffn_up': nrm(ks[20], (DEPTH, D_MODEL, D_FF), D_MODEL ** -0.5),
        'w_ffn_down': nrm(ks[21], (DEPTH, D_FF, D_MODEL), D_FF ** -0.5),
    }


def reference(x_prompt, x_sample, cache_k, cache_v, cache_kidx, state_hgrn, page_table,
              norm_mix_g, w_in, q_norm_g, k_norm_g, kidx_norm_g, rel_bias, hgrn_lb, hgrn_out_g,
              w_up_a, w_up_b, w_out, norm_ffn_g, w_ffn_gate, w_ffn_up, w_ffn_down):
    f32 = jnp.float32
    p = jax.nn.softmax(hgrn_lb.astype(f32), axis=0)
    lower_bounds = jnp.cumsum(p, axis=0) - p[0:1]
    xp, xs = x_prompt, x_sample
    kp_l, vp_l, kip_l, sp_l = [], [], [], []
    ks_l, vs_l, kis_l, ss_l = [], [], [], []
    attend_p = functools.partial(prompt_sparse_attention, rel_bias=rel_bias)
    for l in range(DEPTH):
        wl = (norm_mix_g[l], w_in[l], q_norm_g[l], k_norm_g[l], kidx_norm_g[l], lower_bounds[l], hgrn_out_g[l],
              w_up_a[l], w_up_b[l], w_out[l], norm_ffn_g[l], w_ffn_gate[l], w_ffn_up[l], w_ffn_down[l])
        s0 = jnp.zeros((xp.shape[0], N_HEADS_B, KEY_DIM_B, VAL_DIM_B), f32)
        xp, kp, vp, kip, sp = trunk_layer(xp, attend_p, s0, *wl)
        attend_s = functools.partial(sample_sparse_attention, cache_k=cache_k, cache_v=cache_v,
                                     cache_kidx=cache_kidx, layer=l, page_table=page_table, rel_bias=rel_bias)
        xs, ksm, vsm, kism, ssm = trunk_layer(xs, attend_s, state_hgrn[l], *wl)
        kp_l.append(kp); vp_l.append(vp); kip_l.append(kip); sp_l.append(sp)
        ks_l.append(ksm); vs_l.append(vsm); kis_l.append(kism); ss_l.append(ssm)
    k_prompt = jnp.stack(kp_l)
    v_prompt = jnp.stack(vp_l)
    kidx_prompt = jnp.stack(kip_l)
    state_prompt = jnp.stack(sp_l)
    k_sample = jnp.stack(ks_l)
    v_sample = jnp.stack(vs_l)
    kidx_sample = jnp.stack(kis_l)
    state_sample = jnp.stack(ss_l)
    return (xp, xs, k_prompt, v_prompt, kidx_prompt, state_prompt, k_sample, v_sample, kidx_sample, state_sample)
```

```python
import functools
import math

import jax
import jax.numpy as jnp
from jax import lax
from jax.experimental import pallas as pl
from jax.experimental.pallas import tpu as pltpu

F32 = jnp.float32
BF16 = jnp.bfloat16
I32 = jnp.int32

D_MODEL = 2048
BATCH = 4
SEQ = 2048
DEPTH = 4
DEC_BATCH = 8
PAST_LEN = 16384
PAGE_SIZE = 128
N_PAGES = PAST_LEN // PAGE_SIZE
N_HEADS = 8
HEAD_DIM = 128
WIDTH = N_HEADS * HEAD_DIM
IDX_DIM = 64
TOPK = 256
ATTN_SCALE = HEAD_DIM ** -0.5
IDX_WEIGHT_SCALE = (N_HEADS ** -0.5) * (IDX_DIM ** -0.5)
NEG_LARGE = -1e30
NUM_BUCKETS = 32
MAX_DISTANCE = 128
D_FF = ((8 * D_MODEL // 3 + 255) // 256) * 256
EPS = 1e-6
IN_WIDTHS = (WIDTH, WIDTH, WIDTH, N_HEADS * IDX_DIM, IDX_DIM, N_HEADS,
             WIDTH, WIDTH, WIDTH, WIDTH, D_MODEL, D_MODEL)

SUBLANES = 8
LANES = 128
VMEM_LIMIT = 56 * 1024 * 1024

N_PROMPT = BATCH * SEQ
M_ALL = N_PROMPT + LANES
SAMPLE_ROW_BLOCK = N_PROMPT // SUBLANES

COL = 512
NP_IN = 24 * COL
CB_Q, CB_K, CB_V, CB_QI, CB_KIW = 0, 2, 4, 6, 7
CB_HQ, CB_HF, CB_HI, CB_HG, CB_GA, CB_GB = 8, 10, 12, 14, 16, 20

TM = M_ALL // 5
TM_DOWN = M_ALL // 10
TM_NORM = M_ALL // 20
TN = 512

TQ = 256
NQB = SEQ // TQ
INT_MIN = -2 ** 31
BIG_IDX = 2 ** 30

SUB = 16
CHUNK = 128
N_SUB = CHUNK // SUB


def _cparams(sem):
    return pltpu.CompilerParams(dimension_semantics=sem, vmem_limit_bytes=VMEM_LIMIT)


def _sigmoid(x):
    return 1.0 / (1.0 + jnp.exp(-x))


def _silu(x):
    return x * _sigmoid(x)


def _lb_kernel(lb_ref, o_ref):
    x = lb_ref[...]
    m = jnp.max(x, axis=0, keepdims=True)
    e = jnp.exp(x - m)
    p = e / jnp.sum(e, axis=0, keepdims=True)
    acc = jnp.zeros_like(p[0:1])
    rows = []
    for l in range(DEPTH):
        acc = acc + p[l:l + 1]
        rows.append(acc - p[0:1])
    o_ref[...] = jnp.concatenate(rows, axis=0)


def _lower_bounds(hgrn_lb):
    return pl.pallas_call(
        _lb_kernel,
        out_shape=jax.ShapeDtypeStruct((DEPTH, WIDTH), F32),
    )(hgrn_lb)


def _t5_bucket(n):
    max_exact = NUM_BUCKETS // 2
    nf = jnp.maximum(n, 1).astype(F32)
    large = max_exact + (jnp.log(nf / max_exact) / math.log(MAX_DISTANCE / max_exact)
                         * (NUM_BUCKETS - max_exact)).astype(I32)
    large = jnp.minimum(large, NUM_BUCKETS - 1)
    return jnp.where(n < max_exact, n, large)


def _bias_kernel(rb_ref, o_ref):
    dc = pl.program_id(0)
    s = lax.broadcasted_iota(I32, (TQ, TQ), 0)
    t = lax.broadcasted_iota(I32, (TQ, TQ), 1)
    bucket = _t5_bucket(jnp.maximum(dc * TQ + t - s, 0))
    for h in range(N_HEADS):
        acc = jnp.zeros((TQ, TQ), F32)
        for beta in range(NUM_BUCKETS):
            acc = jnp.where(bucket == beta, rb_ref[beta, h], acc)
        o_ref[0, h] = acc


def _bias_tiles(rel_bias):
    return pl.pallas_call(
        _bias_kernel,
        grid=(3,),
        in_specs=[pl.BlockSpec(memory_space=pltpu.SMEM)],
        out_specs=pl.BlockSpec((1, N_HEADS, TQ, TQ), lambda d: (d, 0, 0, 0)),
        out_shape=jax.ShapeDtypeStruct((3, N_HEADS, TQ, TQ), F32),
        compiler_params=_cparams(("arbitrary",)),
    )(rel_bias)


def _rmsnorm_kernel(x_ref, g_ref, o_ref):
    x = x_ref[...]
    ms = jnp.mean(x * x, axis=-1, keepdims=True)
    o_ref[...] = (x * lax.rsqrt(ms + EPS) * g_ref[...]).astype(o_ref.dtype)


def _rmsnorm(x, g):
    m, d = x.shape
    return pl.pallas_call(
        _rmsnorm_kernel,
        grid=(m // TM_NORM,),
        in_specs=[pl.BlockSpec((TM_NORM, d), lambda i: (i, 0)),
                  pl.BlockSpec((1, d), lambda i: (0, 0))],
        out_specs=pl.BlockSpec((TM_NORM, d), lambda i: (i, 0)),
        out_shape=jax.ShapeDtypeStruct((m, d), BF16),
        compiler_params=_cparams(("arbitrary",)),
    )(x, g.reshape(1, d))


def _wspec(k, layer):
    return pl.BlockSpec((None, k, TN), lambda i, j: (layer, 0, j))


def _in_proj_kernel(a_ref, w_ref, o_ref):
    o_ref[...] = jnp.dot(a_ref[...], w_ref[...], preferred_element_type=F32)


def _in_proj(xn, w_in_p, layer):
    return pl.pallas_call(
        _in_proj_kernel,
        grid=(M_ALL // TM, NP_IN // TN),
        in_specs=[pl.BlockSpec((TM, D_MODEL), lambda i, j: (i, 0)),
                  _wspec(D_MODEL, layer)],
        out_specs=pl.BlockSpec((TM, TN), lambda i, j: (i, j)),
        out_shape=jax.ShapeDtypeStruct((M_ALL, NP_IN), F32),
        compiler_params=_cparams(("arbitrary", "arbitrary")),
    )(xn, w_in_p)


def _merge_kernel(oa_ref, ob_ref, wa_ref, wb_ref, ga_ref, gb_ref, o_ref):
    a = jnp.dot(oa_ref[...], wa_ref[...], preferred_element_type=F32)
    b = jnp.dot(ob_ref[...], wb_ref[...], preferred_element_type=F32)
    o_ref[...] = (_sigmoid(ga_ref[...]) * a + _sigmoid(gb_ref[...]) * b).astype(o_ref.dtype)


def _merge(oa, ob, wa, wb, z, layer):
    return pl.pallas_call(
        _merge_kernel,
        grid=(M_ALL // TM, D_MODEL // TN),
        in_specs=[pl.BlockSpec((TM, WIDTH), lambda i, j: (i, 0)),
                  pl.BlockSpec((TM, WIDTH), lambda i, j: (i, 0)),
                  _wspec(WIDTH, layer), _wspec(WIDTH, layer),
                  pl.BlockSpec((TM, TN), lambda i, j: (i, CB_GA + j)),
                  pl.BlockSpec((TM, TN), lambda i, j: (i, CB_GB + j))],
        out_specs=pl.BlockSpec((TM, TN), lambda i, j: (i, j)),
        out_shape=jax.ShapeDtypeStruct((M_ALL, D_MODEL), BF16),
        compiler_params=_cparams(("arbitrary", "arbitrary")),
    )(oa, ob, wa, wb, z, z)


def _residual_proj_kernel(a_ref, w_ref, x_ref, o_ref):
    o_ref[...] = x_ref[...] + jnp.dot(a_ref[...], w_ref[...], preferred_element_type=F32)


def _residual_proj(a, w, x, layer, tm):
    k = a.shape[1]
    return pl.pallas_call(
        _residual_proj_kernel,
        grid=(M_ALL // tm, D_MODEL // TN),
        in_specs=[pl.BlockSpec((tm, k), lambda i, j: (i, 0)),
                  _wspec(k, layer),
                  pl.BlockSpec((tm, TN), lambda i, j: (i, j))],
        out_specs=pl.BlockSpec((tm, TN), lambda i, j: (i, j)),
        out_shape=jax.ShapeDtypeStruct((M_ALL, D_MODEL), F32),
        compiler_params=_cparams(("arbitrary", "arbitrary")),
    )(a, w, x)


def _ffn_up_kernel(a_ref, wg_ref, wu_ref, o_ref):
    a = a_ref[...]
    g = jnp.dot(a, wg_ref[...], preferred_element_type=F32)
    u = jnp.dot(a, wu_ref[...], preferred_element_type=F32)
    o_ref[...] = (_silu(g) * u).astype(o_ref.dtype)


def _ffn_up(hn, wg, wu, layer):
    return pl.pallas_call(
        _ffn_up_kernel,
        grid=(M_ALL // TM, D_FF // TN),
        in_specs=[pl.BlockSpec((TM, D_MODEL), lambda i, j: (i, 0)),
                  _wspec(D_MODEL, layer), _wspec(D_MODEL, layer)],
        out_specs=pl.BlockSpec((TM, TN), lambda i, j: (i, j)),
        out_shape=jax.ShapeDtypeStruct((M_ALL, D_FF), BF16),
        compiler_params=_cparams(("arbitrary", "arbitrary")),
    )(hn, wg, wu)


def _head_norm(x, g):
    ms = jnp.mean(x * x, axis=-1, keepdims=True)
    return x * lax.rsqrt(ms + EPS) * g


def _kidx_norm(kiw, g_pad):
    lane = lax.broadcasted_iota(I32, kiw.shape, 1)
    ki = jnp.where(lane < IDX_DIM, kiw, 0.0)
    ms = jnp.sum(ki * ki, axis=-1, keepdims=True) * (1.0 / IDX_DIM)
    return ki * lax.rsqrt(ms + EPS) * g_pad


def _post_kernel(zq_ref, zk_ref, zv_ref, zqi_ref, zkiw_ref, gq_ref, gk_ref, gki_ref,
                 qT_ref, qiT_ref, wT_ref, kn_ref, knb_ref, vT_ref, ki_ref):
    gq = gq_ref[...]
    gk = gk_ref[...]
    for h in range(N_HEADS):
        hs = slice(h * HEAD_DIM, (h + 1) * HEAD_DIM)
        qn = _head_norm(zq_ref[:, hs], gq)
        qT_ref[0, hs, :] = qn.T.astype(BF16)
        kn = _head_norm(zk_ref[:, hs], gk)
        kn_ref[:, hs] = kn
        knb_ref[:, hs] = kn.astype(BF16)
        vT_ref[0, 0, hs, :] = zv_ref[:, hs].T.astype(BF16)
    for p in range(N_HEADS * IDX_DIM // LANES):
        ps = slice(p * LANES, (p + 1) * LANES)
        qiT_ref[0, ps, :] = zqi_ref[:, ps].T
    kiw = zkiw_ref[:, 0:LANES]
    ki_ref[...] = _kidx_norm(kiw, gki_ref[...])
    wT_ref[0] = (kiw * IDX_WEIGHT_SCALE).T[IDX_DIM:IDX_DIM + N_HEADS, :]


def _post(z, gq, gk, gki_pad):
    nb = SEQ // LANES
    zspec = lambda cb, w: pl.BlockSpec((LANES, w), lambda r: (r, cb * COL // w))
    gspec = pl.BlockSpec((1, LANES), lambda r: (0, 0))
    return pl.pallas_call(
        _post_kernel,
        grid=(N_PROMPT // LANES,),
        in_specs=[zspec(CB_Q, WIDTH), zspec(CB_K, WIDTH), zspec(CB_V, WIDTH),
                  zspec(CB_QI, COL), zspec(CB_KIW, COL), gspec, gspec, gspec],
        out_specs=[
            pl.BlockSpec((1, WIDTH, LANES), lambda r: (r // nb, 0, r % nb)),
            pl.BlockSpec((1, N_HEADS * IDX_DIM, LANES), lambda r: (r // nb, 0, r % nb)),
            pl.BlockSpec((1, N_HEADS, LANES), lambda r: (r // nb, 0, r % nb)),
            pl.BlockSpec((LANES, WIDTH), lambda r: (r, 0)),
            pl.BlockSpec((LANES, WIDTH), lambda r: (r, 0)),
            pl.BlockSpec((1, 1, WIDTH, LANES),
                         lambda r: (r // nb, (r % nb) // (TQ // LANES), 0, r % (TQ // LANES))),
            pl.BlockSpec((LANES, LANES), lambda r: (r, 0)),
        ],
        out_shape=[
            jax.ShapeDtypeStruct((BATCH, WIDTH, SEQ), BF16),
            jax.ShapeDtypeStruct((BATCH, N_HEADS * IDX_DIM, SEQ), F32),
            jax.ShapeDtypeStruct((BATCH, N_HEADS, SEQ), F32),
            jax.ShapeDtypeStruct((N_PROMPT, WIDTH), F32),
            jax.ShapeDtypeStruct((N_PROMPT, WIDTH), BF16),
            jax.ShapeDtypeStruct((BATCH, NQB, WIDTH, TQ), BF16),
            jax.ShapeDtypeStruct((N_PROMPT, LANES), F32),
        ],
        compiler_params=_cparams(("arbitrary",)),
    )(z, z, z, z, z, gq, gk, gki_pad)


def _sortable_key(score):
    score = jnp.where(score == 0.0, 0.0, score)
    bits = pltpu.bitcast(score, I32)
    return jnp.where(bits < 0, bits ^ 0x7FFFFFFF, bits)


def _attn_kernel(qT_ref, qiT_ref, wT_ref, k_ref, vT_ref, ki_ref, bias_ref, o_ref,
                 keys_ref, m_ref, l_ref, acc_ref, cut_ref):
    qb = pl.program_id(1)
    nkb = qb + 1
    t_pos = qb * TQ + lax.broadcasted_iota(I32, (1, TQ), 1)
    s_loc = lax.broadcasted_iota(I32, (TQ, TQ), 0)

    qi = qiT_ref[0].astype(BF16)
    w = wT_ref[0]

    def score_blk(kb, carry):
        r0 = pl.multiple_of(kb * TQ, TQ)
        ki = ki_ref[pl.ds(r0, TQ), :][:, 0:IDX_DIM].astype(BF16)
        acc = jnp.zeros((TQ, TQ), F32)
        for h in range(N_HEADS):
            s = jnp.dot(ki, qi[h * IDX_DIM:(h + 1) * IDX_DIM, :], preferred_element_type=F32)
            acc = acc + jnp.maximum(s, 0.0) * w[h:h + 1, :]
        key = _sortable_key(acc)
        keys_ref[kb] = jnp.where(kb * TQ + s_loc <= t_pos, key, INT_MIN)
        return carry

    lax.fori_loop(0, nkb, score_blk, 0)

    def count(indicator):
        def body(kb, acc):
            x = indicator(keys_ref[kb], kb * TQ + s_loc)
            return acc + jnp.sum(x.reshape(TQ // SUBLANES, SUBLANES, TQ), axis=0)
        acc = lax.fori_loop(0, nkb, body, jnp.zeros((SUBLANES, TQ), F32))
        return jnp.sum(acc, axis=0, keepdims=True)

    k_eff = jnp.minimum(TOPK, t_pos + 1).astype(F32)
    zero = jnp.zeros((1, TQ), I32)
    cnt = count(lambda key, s: jnp.where(key >= zero, 1.0, 0.0))
    tau0 = jnp.where(cnt >= k_eff, 0, INT_MIN).astype(I32)

    def bit_step(i, tau):
        cand = tau + jnp.left_shift(jnp.int32(1), 30 - i)
        cnt = count(lambda key, s: jnp.where(key >= cand, 1.0, 0.0))
        return jnp.where(cnt >= k_eff, cand, tau)

    tau = lax.fori_loop(0, 31, bit_step, tau0)

    need = k_eff - count(lambda key, s: jnp.where(key > tau, 1.0, 0.0))
    n_eq = count(lambda key, s: jnp.where(key == tau, 1.0, 0.0))
    cut_ref[...] = jnp.full((1, TQ), BIG_IDX, I32)

    @pl.when(jnp.max(n_eq - need) > 0.0)
    def _():
        def idx_step(i, c):
            trial = c + jnp.left_shift(jnp.int32(1), 10 - i)
            cnt = count(lambda key, s: jnp.where(key == tau, jnp.where(s < trial, 1.0, 0.0), 0.0))
            return jnp.where(cnt < need, trial, c)
        cut_ref[...] = lax.fori_loop(0, 11, idx_step, jnp.zeros((1, TQ), I32))

    cut = cut_ref[...]

    m_ref[...] = jnp.full(m_ref.shape, NEG_LARGE, F32)
    l_ref[...] = jnp.zeros(l_ref.shape, F32)
    acc_ref[...] = jnp.zeros(acc_ref.shape, F32)

    def attn_blk(kb, carry):
        r0 = pl.multiple_of(kb * TQ, TQ)
        key = keys_ref[kb]
        s_pos = kb * TQ + s_loc
        sel = jnp.where(key > tau, 0.0,
                        jnp.where(key == tau, jnp.where(s_pos <= cut, 0.0, NEG_LARGE), NEG_LARGE))
        dc = jnp.minimum(qb - kb, 2)
        for h in range(N_HEADS):
            hs = slice(h * HEAD_DIM, (h + 1) * HEAD_DIM)
            s = jnp.dot(k_ref[pl.ds(r0, TQ), hs], qT_ref[0, hs, :], preferred_element_type=F32)
            s = s * ATTN_SCALE + bias_ref[dc, h] + sel
            m_old = m_ref[h]
            m_new = jnp.maximum(m_old, jnp.max(s, axis=0, keepdims=True))
            alpha = jnp.exp(m_old - m_new)
            p = jnp.exp(s - m_new)
            l_ref[h] = alpha * l_ref[h] + jnp.sum(p, axis=0, keepdims=True)
            pv = jnp.dot(vT_ref[0, kb, hs, :], p.astype(BF16), preferred_element_type=F32)
            acc_ref[h] = acc_ref[h] * alpha + pv
            m_ref[h] = m_new
        return carry

    lax.fori_loop(0, nkb, attn_blk, 0)

    for h in range(N_HEADS):
        o = acc_ref[h] / l_ref[h]
        o_ref[:, h * HEAD_DIM:(h + 1) * HEAD_DIM] = o.T.astype(o_ref.dtype)


def _attn_prompt(qT, qiT, wT, knb, vT, ki, bias):
    return pl.pallas_call(
        _attn_kernel,
        grid=(BATCH, NQB),
        in_specs=[
            pl.BlockSpec((1, WIDTH, TQ), lambda b, q: (b, 0, q)),
            pl.BlockSpec((1, N_HEADS * IDX_DIM, TQ), lambda b, q: (b, 0, q)),
            pl.BlockSpec((1, N_HEADS, TQ), lambda b, q: (b, 0, q)),
            pl.BlockSpec((SEQ, WIDTH), lambda b, q: (b, 0)),
            pl.BlockSpec((1, NQB, WIDTH, TQ), lambda b, q: (b, 0, 0, 0)),
            pl.BlockSpec((SEQ, LANES), lambda b, q: (b, 0)),
            pl.BlockSpec((3, N_HEADS, TQ, TQ), lambda b, q: (0, 0, 0, 0)),
        ],
        out_specs=pl.BlockSpec((TQ, WIDTH), lambda b, q: (b * NQB + q, 0)),
        out_shape=jax.ShapeDtypeStruct((N_PROMPT, WIDTH), BF16),
        scratch_shapes=[
            pltpu.VMEM((NQB, TQ, TQ), I32),
            pltpu.VMEM((N_HEADS, 1, TQ), F32),
            pltpu.VMEM((N_HEADS, 1, TQ), F32),
            pltpu.VMEM((N_HEADS, HEAD_DIM, TQ), F32),
            pltpu.VMEM((1, TQ), I32),
        ],
        compiler_params=_cparams(("arbitrary", "arbitrary")),
    )(qT, qiT, wT, knb, vT, ki, bias)


def _hgrn_kernel(zq_ref, zf_ref, zi_ref, zg_ref, lb_ref, og_ref, ob_ref, st_ref,
                 ST_ref, b_ref, k_ref, q_ref, oi_ref):
    ST_ref[...] = jnp.zeros(ST_ref.shape, F32)
    lb = lb_ref[...]
    og = og_ref[...]
    row = lax.broadcasted_iota(I32, (CHUNK, LANES), 0)
    lane = lax.broadcasted_iota(I32, (CHUNK, LANES), 1)
    sub_row = lax.broadcasted_iota(I32, (SUBLANES, LANES), 0)

    def chunk(c, carry):
        r0 = pl.multiple_of(c * CHUNK, CHUNK)
        zq = zq_ref[pl.ds(r0, CHUNK), :]
        f = lb + (1.0 - lb) * _sigmoid(zf_ref[pl.ds(r0, CHUNK), :])
        logf = jnp.log(f)
        k = 1.0 - f
        q = _silu(zq)
        b = logf
        for d in (1, 2, 4, 8):
            b = b + jnp.where(row % SUB >= d, pltpu.roll(b, d, axis=0), 0.0)
        b_ref[...] = b
        k_ref[...] = k
        q_ref[...] = q
        b_last = b_ref[pl.ds(SUB - 1, N_SUB, stride=SUB), :]
        bl_b = jnp.concatenate(
            [jnp.broadcast_to(b_last[j:j + 1, :], (SUB, LANES)) for j in range(N_SUB)], axis=0)

        def sub_block(j, carry2):
            j0 = pl.multiple_of(j * SUB, SUB)
            b_lo = b_ref[pl.ds(j0, SUBLANES), :]
            b_hi = b_ref[pl.ds(j0 + SUBLANES, SUBLANES), :]
            q_lo = q_ref[pl.ds(j0, SUBLANES), :]
            q_hi = q_ref[pl.ds(j0 + SUBLANES, SUBLANES), :]
            o_lo = jnp.zeros((SUBLANES, LANES), F32)
            o_hi = jnp.zeros((SUBLANES, LANES), F32)
            for s in range(SUB):
                bs = jnp.broadcast_to(b_ref[pl.ds(j0 + s, 1), :], (SUBLANES, LANES))
                ks = jnp.broadcast_to(k_ref[pl.ds(j0 + s, 1), :], (SUBLANES, LANES))
                iv = jnp.broadcast_to(zi_ref[pl.ds(r0 + j0 + s, 1), :], (SUBLANES, LANES))
                if s < SUBLANES:
                    ok = sub_row >= s
                    x = jnp.where(ok, q_lo * ks * jnp.exp(jnp.where(ok, b_lo - bs, 0.0)), 0.0)
                    o_lo = o_lo + jnp.sum(x, axis=-1, keepdims=True) * iv
                    x = q_hi * ks * jnp.exp(b_hi - bs)
                    o_hi = o_hi + jnp.sum(x, axis=-1, keepdims=True) * iv
                else:
                    ok = sub_row >= s - SUBLANES
                    x = jnp.where(ok, q_hi * ks * jnp.exp(jnp.where(ok, b_hi - bs, 0.0)), 0.0)
                    o_hi = o_hi + jnp.sum(x, axis=-1, keepdims=True) * iv
            oi_ref[pl.ds(j0, SUBLANES), :] = o_lo
            oi_ref[pl.ds(j0 + SUBLANES, SUBLANES), :] = o_hi
            return carry2

        lax.fori_loop(0, N_SUB, sub_block, 0)

        q_dec = (q * jnp.exp(b)).astype(BF16)
        k_end = (k * jnp.exp(bl_b - b)).astype(BF16)
        decay = jnp.exp(b_last)
        iT = zi_ref[pl.ds(r0, CHUNK), :].T
        it_stack = jnp.concatenate(
            [jnp.where(lane // SUB == j, iT, 0.0).astype(BF16) for j in range(N_SUB)], axis=0)
        pT = jnp.dot(it_stack, k_end, preferred_element_type=F32)
        r = ST_ref[...]
        r_list = []
        for j in range(N_SUB):
            r_list.append(r.astype(BF16))
            r = r * decay[j:j + 1, :] + pT[j * LANES:(j + 1) * LANES, :]
        ST_ref[...] = r
        r_stack = jnp.concatenate(r_list, axis=0)
        oT_all = lax.dot_general(r_stack, q_dec, (((1,), (1,)), ((), ())),
                                 preferred_element_type=F32)
        oT = jnp.zeros((LANES, CHUNK), F32)
        for j in range(N_SUB):
            oT = oT + jnp.where(lane // SUB == j, oT_all[j * LANES:(j + 1) * LANES, :], 0.0)
        o = oT.T + oi_ref[...]
        zg = zg_ref[pl.ds(r0, CHUNK), :]
        ob_ref[pl.ds(r0, CHUNK), :] = (_head_norm(o, og) * _silu(zg)).astype(ob_ref.dtype)
        return carry

    lax.fori_loop(0, SEQ // CHUNK, chunk, 0)
    st_ref[...] = ST_ref[...].T


def _hgrn_prompt(z, lb3, og):
    cpb = COL // LANES
    zspec = lambda cb: pl.BlockSpec((SEQ, LANES), lambda b, h: (b, cb * cpb + h))
    return pl.pallas_call(
        _hgrn_kernel,
        grid=(BATCH, N_HEADS),
        in_specs=[zspec(CB_HQ), zspec(CB_HF), zspec(CB_HI), zspec(CB_HG),
                  pl.BlockSpec((None, 1, LANES), lambda b, h: (h, 0, 0)),
                  pl.BlockSpec((1, LANES), lambda b, h: (0, 0))],
        out_specs=[pl.BlockSpec((SEQ, LANES), lambda b, h: (b, h)),
                   pl.BlockSpec((None, None, LANES, LANES), lambda b, h: (b, h, 0, 0))],
        out_shape=[jax.ShapeDtypeStruct((N_PROMPT, WIDTH), BF16),
                   jax.ShapeDtypeStruct((BATCH, N_HEADS, LANES, LANES), F32)],
        scratch_shapes=[pltpu.VMEM((LANES, LANES), F32),
                        pltpu.VMEM((CHUNK, LANES), F32),
                        pltpu.VMEM((CHUNK, LANES), F32),
                        pltpu.VMEM((CHUNK, LANES), F32),
                        pltpu.VMEM((CHUNK, LANES), F32)],
        compiler_params=_cparams(("arbitrary", "arbitrary")),
    )(z, z, z, z, lb3, og)


def _rows_to_cols(x):
    pad = jnp.zeros((LANES - x.shape[0], LANES), F32)
    return jnp.concatenate([x, pad], axis=0).T


def _hgrn_sample_kernel(zq_ref, zf_ref, zi_ref, zg_ref, lb_ref, og_ref, s0_ref, ob_ref, s1_ref):
    lb = lb_ref[...]
    f = lb + (1.0 - lb) * _sigmoid(zf_ref[...])
    fT = _rows_to_cols(f)
    kT = _rows_to_cols(1.0 - f)
    qT = _rows_to_cols(_silu(zq_ref[...]))
    iv = zi_ref[...]
    rows = []
    for b in range(DEC_BATCH):
        s1 = fT[:, b:b + 1] * s0_ref[b] + kT[:, b:b + 1] * iv[b:b + 1, :]
        s1_ref[b] = s1
        rows.append(jnp.sum(qT[:, b:b + 1] * s1, axis=0, keepdims=True))
    o = jnp.concatenate(rows, axis=0)
    ob_ref[...] = _head_norm(o, og_ref[...]) * _silu(zg_ref[...])


def _hgrn_sample(z, lb3, og, state_hgrn, layer):
    cpb = COL // LANES
    zspec = lambda cb: pl.BlockSpec((DEC_BATCH, LANES), lambda h: (SAMPLE_ROW_BLOCK, cb * cpb + h))
    return pl.pallas_call(
        _hgrn_sample_kernel,
        grid=(N_HEADS,),
        in_specs=[zspec(CB_HQ), zspec(CB_HF), zspec(CB_HI), zspec(CB_HG),
                  pl.BlockSpec((None, 1, LANES), lambda h: (h, 0, 0)),
                  pl.BlockSpec((1, LANES), lambda h: (0, 0)),
                  pl.BlockSpec((None, DEC_BATCH, None, LANES, LANES), lambda h: (layer, 0, h, 0, 0))],
        out_specs=[pl.BlockSpec((DEC_BATCH, LANES), lambda h: (0, h)),
                   pl.BlockSpec((DEC_BATCH, None, LANES, LANES), lambda h: (0, h, 0, 0))],
        out_shape=[jax.ShapeDtypeStruct((DEC_BATCH, WIDTH), F32),
                   jax.ShapeDtypeStruct((DEC_BATCH, N_HEADS, LANES, LANES), F32)],
        compiler_params=_cparams(("arbitrary",)),
    )(z, z, z, z, lb3, og, state_hgrn)


def _sample_norm_kernel(zq_ref, zk_ref, zkiw_ref, gq_ref, gk_ref, gki_ref, qn_ref, kn_ref, kiw_ref):
    gq = gq_ref[...]
    gk = gk_ref[...]
    for h in range(N_HEADS):
        hs = slice(h * HEAD_DIM, (h + 1) * HEAD_DIM)
        qn_ref[:, hs] = _head_norm(zq_ref[:, hs], gq)
        kn_ref[:, hs] = _head_norm(zk_ref[:, hs], gk)
    kiw = zkiw_ref[:, 0:LANES]
    lane = lax.broadcasted_iota(I32, kiw.shape, 1)
    kiw_ref[...] = _kidx_norm(kiw, gki_ref[...]) + jnp.where(
        (lane >= IDX_DIM) & (lane < IDX_DIM + N_HEADS), kiw * IDX_WEIGHT_SCALE, 0.0)


def _sample_norms(z, gq, gk, gki_pad):
    zspec = lambda cb, w: pl.BlockSpec((DEC_BATCH, w), lambda i: (SAMPLE_ROW_BLOCK, cb * COL // w))
    gspec = pl.BlockSpec((1, LANES), lambda i: (0, 0))
    return pl.pallas_call(
        _sample_norm_kernel,
        grid=(1,),
        in_specs=[zspec(CB_Q, WIDTH), zspec(CB_K, WIDTH), zspec(CB_KIW, COL), gspec, gspec, gspec],
        out_specs=[pl.BlockSpec((DEC_BATCH, WIDTH), lambda i: (0, 0)),
                   pl.BlockSpec((DEC_BATCH, WIDTH), lambda i: (0, 0)),
                   pl.BlockSpec((DEC_BATCH, LANES), lambda i: (0, 0))],
        out_shape=[jax.ShapeDtypeStruct((DEC_BATCH, WIDTH), F32),
                   jax.ShapeDtypeStruct((DEC_BATCH, WIDTH), F32),
                   jax.ShapeDtypeStruct((DEC_BATCH, LANES), F32)],
        compiler_params=_cparams(("arbitrary",)),
    )(z, z, z, gq, gk, gki_pad)


HALF_KEYS = PAST_LEN // 2
SCORE_W = PAST_LEN + LANES


def _select_row(x, b):
    sub = lax.broadcasted_iota(I32, x.shape, 0)
    return jnp.sum(jnp.where(sub == b, x, 0.0), axis=0, keepdims=True)


def _sample_topk_kernel(pt_ref, zqi_ref, kiw_ref, cache_ref, sel_ref, buf_ref, sem_ref, sc_ref,
                        *, layer, n_pool):
    b = pl.program_id(0)

    def page_copy(p):
        page = layer * n_pool + pt_ref[b, p]
        return pltpu.make_async_copy(cache_ref.at[page], buf_ref.at[p], sem_ref.at[0])

    def start(p, c):
        page_copy(p).start()
        return c

    lax.fori_loop(0, N_PAGES, start, 0)

    qrow = _select_row(zqi_ref[...], b)
    head = lax.broadcasted_iota(I32, (N_HEADS, N_HEADS * IDX_DIM), 0)
    col = lax.broadcasted_iota(I32, (N_HEADS, N_HEADS * IDX_DIM), 1)
    qm = jnp.where(col // IDX_DIM == head, jnp.broadcast_to(qrow, head.shape), 0.0)
    q128 = qm[:, 0:LANES]
    for p in range(1, N_HEADS * IDX_DIM // LANES):
        q128 = q128 + qm[:, p * LANES:(p + 1) * LANES]
    q_sw = pltpu.roll(q128, IDX_DIM, axis=1)
    h8 = lax.broadcasted_iota(I32, (N_HEADS, LANES), 0)
    q_even = jnp.where(h8 % 2 == 0, q128, q_sw)
    q_odd = jnp.where(h8 % 2 == 0, q_sw, q128)
    q2 = jnp.concatenate([q_even, q_odd], axis=0).astype(BF16)

    kiw_row = _select_row(kiw_ref[...], b)
    r16 = lax.broadcasted_iota(I32, (2 * N_HEADS, LANES), 0)
    l16 = lax.broadcasted_iota(I32, (2 * N_HEADS, LANES), 1)
    wcol = jnp.sum(jnp.where(l16 == IDX_DIM + r16 % N_HEADS,
                             jnp.broadcast_to(kiw_row, (2 * N_HEADS, LANES)), 0.0),
                   axis=-1, keepdims=True)

    def wait(p, c):
        page_copy(p).wait()
        return c

    lax.fori_loop(0, N_PAGES, wait, 0)

    nt = (((1,), (1,)), ((), ()))
    keys = buf_ref[...].reshape(HALF_KEYS, LANES).astype(BF16)
    s = lax.dot_general(q2, keys, nt, preferred_element_type=F32)
    s = jnp.maximum(s, 0.0) * wcol
    s_even = jnp.sum(s[0:N_HEADS], axis=0, keepdims=True)
    s_odd = jnp.sum(s[N_HEADS:], axis=0, keepdims=True)
    lane1 = lax.broadcasted_iota(I32, (1, LANES), 1)
    knew = jnp.where(lane1 < IDX_DIM, kiw_row, 0.0)
    knew8 = jnp.broadcast_to(knew, (SUBLANES, LANES)).astype(BF16)
    sn = lax.dot_general(q2, knew8, nt, preferred_element_type=F32)
    sn = jnp.maximum(sn, 0.0) * wcol
    s_new = jnp.sum(sn[0:N_HEADS, 0:1], axis=0, keepdims=True)
    tail = jnp.where(lane1 == 0, jnp.broadcast_to(s_new, (1, LANES)), -jnp.inf)
    srow = jnp.concatenate([s_even, s_odd, tail], axis=1)
    srow = jnp.where(srow == 0.0, 0.0, srow)
    @pl.when(b == 0)
    def _():
        sc_ref[...] = jnp.full(sc_ref.shape, -jnp.inf, F32)

    sub = lax.broadcasted_iota(I32, (DEC_BATCH, SCORE_W), 0)
    sc_ref[...] = jnp.where(sub == b, jnp.broadcast_to(srow, (DEC_BATCH, SCORE_W)), sc_ref[...])

    @pl.when(b == DEC_BATCH - 1)
    def _():
        slot = lax.broadcasted_iota(I32, (DEC_BATCH, SCORE_W), 1)
        pos = jnp.where(slot < HALF_KEYS, 2 * slot,
                        jnp.where(slot < PAST_LEN, 2 * (slot - HALF_KEYS) + 1,
                                  jnp.where(slot == PAST_LEN, PAST_LEN, BIG_IDX))).astype(F32)
        out_lane = lax.broadcasted_iota(I32, (DEC_BATCH, TOPK), 1)

        def pick(it, sel):
            sc = sc_ref[...]
            m = jnp.max(sc, axis=1, keepdims=True)
            first = jnp.min(jnp.where(sc == m, pos, float(BIG_IDX)), axis=1, keepdims=True)
            sc_ref[...] = jnp.where(pos == first, -jnp.inf, sc)
            return jnp.where(out_lane == it, first.astype(I32), sel)

        sel_ref[...] = lax.fori_loop(0, TOPK, pick, jnp.zeros((DEC_BATCH, TOPK), I32))


def _sample_topk(page_table, z, kiw_s, cache_kidx, layer):
    n_pool = cache_kidx.shape[1]
    cache = cache_kidx.reshape(DEPTH * n_pool, PAGE_SIZE // 2, LANES)
    grid_spec = pltpu.PrefetchScalarGridSpec(
        num_scalar_prefetch=1,
        grid=(DEC_BATCH,),
        in_specs=[pl.BlockSpec((DEC_BATCH, COL), lambda b, pt: (SAMPLE_ROW_BLOCK, CB_QI)),
                  pl.BlockSpec((DEC_BATCH, LANES), lambda b, pt: (0, 0)),
                  pl.BlockSpec(memory_space=pl.ANY)],
        out_specs=pl.BlockSpec((DEC_BATCH, TOPK), lambda b, pt: (0, 0)),
        scratch_shapes=[pltpu.VMEM((N_PAGES, PAGE_SIZE // 2, LANES), F32),
                        pltpu.SemaphoreType.DMA((1,)),
                        pltpu.VMEM((DEC_BATCH, SCORE_W), F32)],
    )
    return pl.pallas_call(
        functools.partial(_sample_topk_kernel, layer=layer, n_pool=n_pool),
        grid_spec=grid_spec,
        out_shape=jax.ShapeDtypeStruct((DEC_BATCH, TOPK), I32),
        compiler_params=_cparams(("arbitrary",)),
    )(page_table, z, kiw_s, cache)


def _sample_attn_kernel(sel_s_ref, pt_ref, sel_ref, qn_ref, kn_ref, zv_ref, rbT_ref,
                        ck_ref, cv_ref, o_ref, kbuf_ref, vbuf_ref, sem_ref, *, layer):
    b = pl.program_id(0)

    def copies(j):
        sp = jnp.minimum(sel_s_ref[b, j], PAST_LEN - 1)
        page = pt_ref[b, sp // PAGE_SIZE]
        off = sp % PAGE_SIZE
        dst = pl.ds(pl.multiple_of(j * N_HEADS, N_HEADS), N_HEADS)
        return (pltpu.make_async_copy(ck_ref.at[layer, page, off], kbuf_ref.at[dst], sem_ref.at[0]),
                pltpu.make_async_copy(cv_ref.at[layer, page, off], vbuf_ref.at[dst], sem_ref.at[1]))

    def start(j, c):
        ck, cv = copies(j)
        ck.start()
        cv.start()
        return c

    lax.fori_loop(0, TOPK, start, 0)

    def heads_on_rows(x):
        row = _select_row(x, b)
        return jnp.concatenate([row[:, h * HEAD_DIM:(h + 1) * HEAD_DIM] for h in range(N_HEADS)],
                               axis=0)

    q = heads_on_rows(qn_ref[...])
    k_new = heads_on_rows(kn_ref[...])
    v_new = heads_on_rows(zv_ref[...])
    sel_b = jnp.sum(jnp.where(lax.broadcasted_iota(I32, (DEC_BATCH, TOPK), 0) == b, sel_ref[...], 0),
                    axis=0, keepdims=True)
    is_new = sel_b >= PAST_LEN
    bucket = _t5_bucket(jnp.maximum(PAST_LEN - sel_b, 0))
    bias = jnp.zeros((N_HEADS, TOPK), F32)
    for beta in range(NUM_BUCKETS):
        bias = jnp.where(bucket == beta, rbT_ref[:, beta:beta + 1], bias)

    def wait(j, c):
        ck, cv = copies(j)
        ck.wait()
        cv.wait()
        return c

    lax.fori_loop(0, TOPK, wait, 0)

    nt = (((1,), (1,)), ((), ()))
    hrow = lax.broadcasted_iota(I32, (N_HEADS, TOPK), 0)
    qb16 = q.astype(BF16)
    logits = jnp.zeros((N_HEADS, TOPK), F32)
    for h in range(N_HEADS):
        kh = kbuf_ref[pl.ds(h, TOPK, stride=N_HEADS), :].astype(BF16)
        res = lax.dot_general(qb16, kh, nt, preferred_element_type=F32)
        logits = jnp.where(hrow == h, res, logits)
    logit_new = jnp.sum(qb16.astype(F32) * k_new.astype(BF16).astype(F32), axis=-1, keepdims=True)
    logits = jnp.where(is_new, logit_new, logits) * ATTN_SCALE + bias
    m = jnp.max(logits, axis=1, keepdims=True)
    e = jnp.exp(logits - m)
    p = e / jnp.sum(e, axis=1, keepdims=True)
    p_new = jnp.sum(jnp.where(is_new, p, 0.0), axis=1, keepdims=True)
    p_mm = jnp.where(is_new, 0.0, p).astype(BF16)
    hrow_o = lax.broadcasted_iota(I32, (N_HEADS, HEAD_DIM), 0)
    o = jnp.zeros((N_HEADS, HEAD_DIM), F32)
    for h in range(N_HEADS):
        vh = vbuf_ref[pl.ds(h, TOPK, stride=N_HEADS), :].astype(BF16)
        res = jnp.dot(p_mm, vh, preferred_element_type=F32)
        o = jnp.where(hrow_o == h, res, o)
    o = o + p_new.astype(BF16).astype(F32) * v_new.astype(BF16).astype(F32)
    orow = jnp.concatenate([o[h:h + 1, :] for h in range(N_HEADS)], axis=1)

    @pl.when(b == 0)
    def _():
        o_ref[...] = jnp.zeros(o_ref.shape, F32)

    sub = lax.broadcasted_iota(I32, (DEC_BATCH, WIDTH), 0)
    o_ref[...] = jnp.where(sub == b, jnp.broadcast_to(orow, (DEC_BATCH, WIDTH)), o_ref[...])


def _sample_attn(sel, page_table, qn_s, kn_s, z, rbT, cache_k, cache_v, layer):
    grid_spec = pltpu.PrefetchScalarGridSpec(
        num_scalar_prefetch=2,
        grid=(DEC_BATCH,),
        in_specs=[pl.BlockSpec((DEC_BATCH, TOPK), lambda b, s, pt: (0, 0)),
                  pl.BlockSpec((DEC_BATCH, WIDTH), lambda b, s, pt: (0, 0)),
                  pl.BlockSpec((DEC_BATCH, WIDTH), lambda b, s, pt: (0, 0)),
                  pl.BlockSpec((DEC_BATCH, WIDTH), lambda b, s, pt: (SAMPLE_ROW_BLOCK, CB_V * COL // WIDTH)),
                  pl.BlockSpec((N_HEADS, NUM_BUCKETS), lambda b, s, pt: (0, 0)),
                  pl.BlockSpec(memory_space=pl.ANY),
                  pl.BlockSpec(memory_space=pl.ANY)],
        out_specs=pl.BlockSpec((DEC_BATCH, WIDTH), lambda b, s, pt: (0, 0)),
        scratch_shapes=[pltpu.VMEM((TOPK * N_HEADS, HEAD_DIM), F32),
                        pltpu.VMEM((TOPK * N_HEADS, HEAD_DIM), F32),
                        pltpu.SemaphoreType.DMA((2,))],
    )
    return pl.pallas_call(
        functools.partial(_sample_attn_kernel, layer=layer),
        grid_spec=grid_spec,
        out_shape=jax.ShapeDtypeStruct((DEC_BATCH, WIDTH), F32),
        compiler_params=_cparams(("arbitrary",)),
    )(sel, page_table, sel, qn_s, kn_s, z, rbT, cache_k, cache_v)


def _permute_w_in(w_in):
    parts = []
    off = 0
    for w in IN_WIDTHS:
        parts.append(w_in[..., off:off + w])
        off += w
    q, k, v, qi, ki, wi, hq, hf, hi, hg, ga, gb = parts
    pad = jnp.zeros(w_in.shape[:-1] + (COL - IDX_DIM - N_HEADS,), w_in.dtype)
    return jnp.concatenate([q, k, v, qi, ki, wi, pad, hq, hf, hi, hg, ga, gb], axis=-1).astype(BF16)


def _stack_rows(prompt_rows, sample_rows):
    pad = jnp.zeros((M_ALL - N_PROMPT - DEC_BATCH, prompt_rows.shape[1]), prompt_rows.dtype)
    return jnp.concatenate([prompt_rows, sample_rows.astype(prompt_rows.dtype), pad], axis=0)


def kernel(x_prompt, x_sample, cache_k, cache_v, cache_kidx, state_hgrn, page_table, norm_mix_g, w_in,
           q_norm_g, k_norm_g, kidx_norm_g, rel_bias, hgrn_lb, hgrn_out_g, w_up_a, w_up_b, w_out,
           norm_ffn_g, w_ffn_gate, w_ffn_up, w_ffn_down):
    w_in_p = _permute_w_in(w_in)
    w_up_a, w_up_b, w_out = (w.astype(BF16) for w in (w_up_a, w_up_b, w_out))
    w_ffn_gate, w_ffn_up, w_ffn_down = (w.astype(BF16) for w in (w_ffn_gate, w_ffn_up, w_ffn_down))
    lower = _lower_bounds(hgrn_lb)
    bias = _bias_tiles(rel_bias)
    rbT = rel_bias.T
    gki_pad = jnp.pad(kidx_norm_g, ((0, 0), (0, LANES - IDX_DIM)))

    x = _stack_rows(x_prompt.reshape(N_PROMPT, D_MODEL), x_sample.reshape(DEC_BATCH, D_MODEL))
    outs = [[] for _ in range(8)]
    for l in range(DEPTH):
        gq = q_norm_g[l].reshape(1, HEAD_DIM)
        gk = k_norm_g[l].reshape(1, HEAD_DIM)
        gki = gki_pad[l].reshape(1, LANES)
        og = hgrn_out_g[l].reshape(1, LANES)
        lb3 = lower[l].reshape(N_HEADS, 1, LANES)

        xn = _rmsnorm(x, norm_mix_g[l])
        z = _in_proj(xn, w_in_p, l)

        qT, qiT, wT, kn, knb, vT, ki = _post(z, gq, gk, gki)
        oa_p = _attn_prompt(qT, qiT, wT, knb, vT, ki, bias)
        qn_s, kn_s, kiw_s = _sample_norms(z, gq, gk, gki)
        sel = _sample_topk(page_table, z, kiw_s, cache_kidx, l)
        oa_s = _sample_attn(sel, page_table, qn_s, kn_s, z, rbT, cache_k, cache_v, l)

        ob_p, st_p = _hgrn_prompt(z, lb3, og)
        ob_s, st_s = _hgrn_sample(z, lb3, og, state_hgrn, l)

        mixed = _merge(_stack_rows(oa_p, oa_s), _stack_rows(ob_p, ob_s), w_up_a, w_up_b, z, l)
        x = _residual_proj(mixed, w_out, x, l, TM)
        hn = _rmsnorm(x, norm_ffn_g[l])
        x = _residual_proj(_ffn_up(hn, w_ffn_gate, w_ffn_up, l), w_ffn_down, x, l, TM_DOWN)

        v_cols = slice(CB_V * COL, CB_V * COL + WIDTH)
        outs[0].append(kn.reshape(BATCH, SEQ, N_HEADS, HEAD_DIM))
        outs[1].append(z[:N_PROMPT, v_cols].reshape(BATCH, SEQ, N_HEADS, HEAD_DIM))
        outs[2].append(ki[:, :IDX_DIM].reshape(BATCH, SEQ, IDX_DIM))
        outs[3].append(st_p)
        outs[4].append(kn_s.reshape(DEC_BATCH, 1, N_HEADS, HEAD_DIM))
        outs[5].append(z[N_PROMPT:N_PROMPT + DEC_BATCH, v_cols].reshape(DEC_BATCH, 1, N_HEADS, HEAD_DIM))
        outs[6].append(kiw_s[:, :IDX_DIM].reshape(DEC_BATCH, 1, IDX_DIM))
        outs[7].append(st_s)

    y_prompt = x[:N_PROMPT].reshape(BATCH, SEQ, D_MODEL)
    y_sample = x[N_PROMPT:N_PROMPT + DEC_BATCH].reshape(DEC_BATCH, 1, D_MODEL)
    return (y_prompt, y_sample) + tuple(jnp.stack(o) for o in outs)
```

```python
import functools
import math

import jax
import jax.numpy as jnp
from jax import lax
from jax.experimental import pallas as pl
from jax.experimental.pallas import tpu as pltpu

F32 = jnp.float32
BF16 = jnp.bfloat16
I32 = jnp.int32

D_MODEL = 2048
BATCH = 4
SEQ = 2048
DEPTH = 4
DEC_BATCH = 8
PAST_LEN = 16384
PAGE_SIZE = 128
N_PAGES = PAST_LEN // PAGE_SIZE
N_HEADS = 8
HEAD_DIM = 128
WIDTH = N_HEADS * HEAD_DIM
IDX_DIM = 64
TOPK = 256
ATTN_SCALE = HEAD_DIM ** -0.5
IDX_WEIGHT_SCALE = (N_HEADS ** -0.5) * (IDX_DIM ** -0.5)
NEG_LARGE = -1e30
NUM_BUCKETS = 32
MAX_DISTANCE = 128
D_FF = ((8 * D_MODEL // 3 + 255) // 256) * 256
EPS = 1e-6
IN_WIDTHS = (WIDTH, WIDTH, WIDTH, N_HEADS * IDX_DIM, IDX_DIM, N_HEADS,
             WIDTH, WIDTH, WIDTH, WIDTH, D_MODEL, D_MODEL)

SUBLANES = 8
LANES = 128
VMEM_LIMIT = 56 * 1024 * 1024

N_PROMPT = BATCH * SEQ
M_ALL = N_PROMPT + LANES
SAMPLE_ROW_BLOCK = N_PROMPT // SUBLANES
SAMPLE_TILE = N_PROMPT // LANES

COL = 512
NP_IN = 24 * COL
CB_Q, CB_K, CB_V, CB_QI, CB_KIW = 0, 2, 4, 6, 7
CB_HQ, CB_HF, CB_HI, CB_HG, CB_GA, CB_GB = 8, 10, 12, 14, 16, 20

TM = M_ALL // 5
TM_DOWN = M_ALL // 10
TM_NORM = M_ALL // 20
TN = 512

TQ = 256
NQB = SEQ // TQ
TK_SUB = 128
INT_MIN = -2 ** 31
BIG_IDX = 2 ** 30

SUB = 16
CHUNK = 128
N_SUB = CHUNK // SUB
HPB = 4


def _cparams(sem):
    return pltpu.CompilerParams(dimension_semantics=sem, vmem_limit_bytes=VMEM_LIMIT)


def _sigmoid(x):
    return 1.0 / (1.0 + jnp.exp(-x))


def _silu(x):
    return x * _sigmoid(x)


def _lb_kernel(lb_ref, o_ref):
    x = lb_ref[...]
    m = jnp.max(x, axis=0, keepdims=True)
    e = jnp.exp(x - m)
    p = e / jnp.sum(e, axis=0, keepdims=True)
    acc = jnp.zeros_like(p[0:1])
    rows = []
    for l in range(DEPTH):
        acc = acc + p[l:l + 1]
        rows.append(acc - p[0:1])
    o_ref[...] = jnp.concatenate(rows, axis=0)


def _lower_bounds(hgrn_lb):
    return pl.pallas_call(
        _lb_kernel,
        out_shape=jax.ShapeDtypeStruct((DEPTH, WIDTH), F32),
        name="hgrn_lower_bounds",
    )(hgrn_lb)


def _t5_bucket(n):
    max_exact = NUM_BUCKETS // 2
    nf = jnp.maximum(n, 1).astype(F32)
    large = max_exact + (jnp.log(nf / max_exact) / math.log(MAX_DISTANCE / max_exact)
                         * (NUM_BUCKETS - max_exact)).astype(I32)
    large = jnp.minimum(large, NUM_BUCKETS - 1)
    return jnp.where(n < max_exact, n, large)


def _bias_kernel(rb_ref, o_ref):
    dc = pl.program_id(0)
    s = lax.broadcasted_iota(I32, (TQ, TQ), 0)
    t = lax.broadcasted_iota(I32, (TQ, TQ), 1)
    bucket = _t5_bucket(jnp.maximum(dc * TQ + t - s, 0))
    for h in range(N_HEADS):
        acc = jnp.zeros((TQ, TQ), F32)
        for beta in range(NUM_BUCKETS):
            acc = jnp.where(bucket == beta, rb_ref[beta, h], acc)
        o_ref[0, h] = acc


def _bias_tiles(rel_bias):
    return pl.pallas_call(
        _bias_kernel,
        grid=(3,),
        in_specs=[pl.BlockSpec(memory_space=pltpu.SMEM)],
        out_specs=pl.BlockSpec((1, N_HEADS, TQ, TQ), lambda d: (d, 0, 0, 0)),
        out_shape=jax.ShapeDtypeStruct((3, N_HEADS, TQ, TQ), F32),
        compiler_params=_cparams(("arbitrary",)),
        name="bias_tiles",
    )(rel_bias)


def _rmsnorm_kernel(x_ref, g_ref, o_ref):
    x = x_ref[...]
    ms = jnp.mean(x * x, axis=-1, keepdims=True)
    o_ref[...] = (x * lax.rsqrt(ms + EPS) * g_ref[...]).astype(o_ref.dtype)


def _rmsnorm(x, g):
    m, d = x.shape
    return pl.pallas_call(
        _rmsnorm_kernel,
        grid=(m // TM_NORM,),
        in_specs=[pl.BlockSpec((TM_NORM, d), lambda i: (i, 0)),
                  pl.BlockSpec((1, d), lambda i: (0, 0))],
        out_specs=pl.BlockSpec((TM_NORM, d), lambda i: (i, 0)),
        out_shape=jax.ShapeDtypeStruct((m, d), BF16),
        compiler_params=_cparams(("arbitrary",)),
        name="rmsnorm",
    )(x, g.reshape(1, d))


def _wspec(k, layer):
    return pl.BlockSpec((None, k, TN), lambda i, j: (layer, 0, j))


def _in_proj_kernel(a_ref, w_ref, o_ref):
    o_ref[...] = jnp.dot(a_ref[...], w_ref[...], preferred_element_type=F32)


def _in_proj(xn, w_in_p, layer):
    return pl.pallas_call(
        _in_proj_kernel,
        grid=(M_ALL // TM, NP_IN // TN),
        in_specs=[pl.BlockSpec((TM, D_MODEL), lambda i, j: (i, 0)),
                  _wspec(D_MODEL, layer)],
        out_specs=pl.BlockSpec((TM, TN), lambda i, j: (i, j)),
        out_shape=jax.ShapeDtypeStruct((M_ALL, NP_IN), F32),
        compiler_params=_cparams(("arbitrary", "arbitrary")),
        name="in_proj",
    )(xn, w_in_p)


def _merge_kernel(oa_ref, ob_ref, wa_ref, wb_ref, ga_ref, gb_ref, o_ref):
    a = jnp.dot(oa_ref[...], wa_ref[...], preferred_element_type=F32)
    b = jnp.dot(ob_ref[...], wb_ref[...], preferred_element_type=F32)
    o_ref[...] = (_sigmoid(ga_ref[...]) * a + _sigmoid(gb_ref[...]) * b).astype(o_ref.dtype)


def _merge(oa, ob, wa, wb, z, layer):
    return pl.pallas_call(
        _merge_kernel,
        grid=(M_ALL // TM, D_MODEL // TN),
        in_specs=[pl.BlockSpec((TM, WIDTH), lambda i, j: (i, 0)),
                  pl.BlockSpec((TM, WIDTH), lambda i, j: (i, 0)),
                  _wspec(WIDTH, layer), _wspec(WIDTH, layer),
                  pl.BlockSpec((TM, TN), lambda i, j: (i, CB_GA + j)),
                  pl.BlockSpec((TM, TN), lambda i, j: (i, CB_GB + j))],
        out_specs=pl.BlockSpec((TM, TN), lambda i, j: (i, j)),
        out_shape=jax.ShapeDtypeStruct((M_ALL, D_MODEL), BF16),
        compiler_params=_cparams(("arbitrary", "arbitrary")),
        name="merge",
    )(oa, ob, wa, wb, z, z)


def _residual_proj_kernel(a_ref, w_ref, x_ref, o_ref):
    o_ref[...] = x_ref[...] + jnp.dot(a_ref[...], w_ref[...], preferred_element_type=F32)


def _residual_proj(a, w, x, layer, tm, name):
    k = a.shape[1]
    return pl.pallas_call(
        _residual_proj_kernel,
        grid=(M_ALL // tm, D_MODEL // TN),
        in_specs=[pl.BlockSpec((tm, k), lambda i, j: (i, 0)),
                  _wspec(k, layer),
                  pl.BlockSpec((tm, TN), lambda i, j: (i, j))],
        out_specs=pl.BlockSpec((tm, TN), lambda i, j: (i, j)),
        out_shape=jax.ShapeDtypeStruct((M_ALL, D_MODEL), F32),
        compiler_params=_cparams(("arbitrary", "arbitrary")),
        name=name,
    )(a, w, x)


def _ffn_up_kernel(a_ref, wg_ref, wu_ref, o_ref):
    a = a_ref[...]
    g = jnp.dot(a, wg_ref[...], preferred_element_type=F32)
    u = jnp.dot(a, wu_ref[...], preferred_element_type=F32)
    o_ref[...] = (_silu(g) * u).astype(o_ref.dtype)


def _ffn_up(hn, wg, wu, layer):
    return pl.pallas_call(
        _ffn_up_kernel,
        grid=(M_ALL // TM, D_FF // TN),
        in_specs=[pl.BlockSpec((TM, D_MODEL), lambda i, j: (i, 0)),
                  _wspec(D_MODEL, layer), _wspec(D_MODEL, layer)],
        out_specs=pl.BlockSpec((TM, TN), lambda i, j: (i, j)),
        out_shape=jax.ShapeDtypeStruct((M_ALL, D_FF), BF16),
        compiler_params=_cparams(("arbitrary", "arbitrary")),
        name="ffn_up",
    )(hn, wg, wu)


def _head_norm(x, g):
    ms = jnp.mean(x * x, axis=-1, keepdims=True)
    return x * lax.rsqrt(ms + EPS) * g


def _kidx_norm(kiw, g_pad):
    lane = lax.broadcasted_iota(I32, kiw.shape, 1)
    ki = jnp.where(lane < IDX_DIM, kiw, 0.0)
    ms = jnp.sum(ki * ki, axis=-1, keepdims=True) * (1.0 / IDX_DIM)
    return ki * lax.rsqrt(ms + EPS) * g_pad


def _post_kernel(zq_ref, zk_ref, zv_ref, zqi_ref, zkiw_ref, gq_ref, gk_ref, gki_ref, *rest):
    qT_ref, qiT_ref, wT_ref, knb_ref, vT_ref, kn_ref, v_ref, ki_ref = rest[-8:]
    gq = gq_ref[...]
    gk = gk_ref[...]
    for h in range(N_HEADS):
        hs = slice(h * HEAD_DIM, (h + 1) * HEAD_DIM)
        qn = _head_norm(zq_ref[:, hs], gq)
        qT_ref[0, hs, :] = qn.T.astype(BF16)
        kn = _head_norm(zk_ref[:, hs], gk)
        kn_ref[:, hs] = kn
        knb_ref[:, hs] = kn.astype(BF16)
        v = zv_ref[:, hs]
        v_ref[:, hs] = v
        vT_ref[0, 0, hs, :] = v.T.astype(BF16)
    for p in range(N_HEADS * IDX_DIM // LANES):
        ps = slice(p * LANES, (p + 1) * LANES)
        qiT_ref[0, ps, :] = zqi_ref[:, ps].T
    kiw = zkiw_ref[:, 0:LANES]
    ki_ref[...] = _kidx_norm(kiw, gki_ref[...])
    wT_ref[0] = (kiw * IDX_WEIGHT_SCALE).T[IDX_DIM:IDX_DIM + N_HEADS, :]


def _post(z, gq, gk, gki_pad, layer, stacked):
    nb = SEQ // LANES
    zspec = lambda cb, w: pl.BlockSpec((LANES, w), lambda r: (r, cb * COL // w))
    gspec = pl.BlockSpec((1, LANES), lambda r: (0, 0))
    in_specs = [zspec(CB_Q, WIDTH), zspec(CB_K, WIDTH), zspec(CB_V, WIDTH),
                zspec(CB_QI, COL), zspec(CB_KIW, COL), gspec, gspec, gspec]
    args = [z, z, z, z, z, gq, gk, gki_pad]
    aliases = {}
    for n, buf in enumerate(stacked):
        aliases[len(args)] = 5 + n
        in_specs.append(pl.BlockSpec(memory_space=pl.ANY))
        args.append(buf)
    return pl.pallas_call(
        _post_kernel,
        grid=(N_PROMPT // LANES,),
        in_specs=in_specs,
        out_specs=[
            pl.BlockSpec((1, WIDTH, LANES), lambda r: (r // nb, 0, r % nb)),
            pl.BlockSpec((1, N_HEADS * IDX_DIM, LANES), lambda r: (r // nb, 0, r % nb)),
            pl.BlockSpec((1, N_HEADS, LANES), lambda r: (r // nb, 0, r % nb)),
            pl.BlockSpec((LANES, WIDTH), lambda r: (r, 0)),
            pl.BlockSpec((1, 1, WIDTH, LANES),
                         lambda r: (r // nb, (r % nb) // (TQ // LANES), 0, r % (TQ // LANES))),
            pl.BlockSpec((None, LANES, WIDTH), lambda r: (layer, r, 0)),
            pl.BlockSpec((None, LANES, WIDTH), lambda r: (layer, r, 0)),
            pl.BlockSpec((None, LANES, LANES), lambda r: (layer, r, 0)),
        ],
        out_shape=[
            jax.ShapeDtypeStruct((BATCH, WIDTH, SEQ), BF16),
            jax.ShapeDtypeStruct((BATCH, N_HEADS * IDX_DIM, SEQ), F32),
            jax.ShapeDtypeStruct((BATCH, N_HEADS, SEQ), F32),
            jax.ShapeDtypeStruct((N_PROMPT, WIDTH), BF16),
            jax.ShapeDtypeStruct((BATCH, NQB, WIDTH, TQ), BF16),
            jax.ShapeDtypeStruct((DEPTH, N_PROMPT, WIDTH), F32),
            jax.ShapeDtypeStruct((DEPTH, N_PROMPT, WIDTH), F32),
            jax.ShapeDtypeStruct((DEPTH, N_PROMPT, LANES), F32),
        ],
        input_output_aliases=aliases,
        compiler_params=_cparams(("arbitrary",)),
        name="post_a",
    )(*args)


def _sortable_key(score):
    score = jnp.where(score == 0.0, 0.0, score)
    bits = pltpu.bitcast(score, I32)
    return jnp.where(bits < 0, bits ^ 0x7FFFFFFF, bits)


def _attn_kernel(qT_ref, qiT_ref, wT_ref, k_ref, vT_ref, ki_ref, bias_ref, _rows_ref, o_ref,
                 keys_ref, sel_ref, acc_ref, p_ref, cut_ref):
    qb = pl.program_id(1)
    nkb = qb + 1
    t_pos = qb * TQ + lax.broadcasted_iota(I32, (1, TQ), 1)
    s_loc = lax.broadcasted_iota(I32, (TQ, TQ), 0)

    qi = qiT_ref[0].astype(BF16)
    w = wT_ref[0]

    def score_blk(kb, carry):
        r0 = pl.multiple_of(kb * TQ, TQ)
        ki = ki_ref[pl.ds(r0, TQ), :][:, 0:IDX_DIM].astype(BF16)
        acc = jnp.zeros((TQ, TQ), F32)
        for h in range(N_HEADS):
            s = jnp.dot(ki, qi[h * IDX_DIM:(h + 1) * IDX_DIM, :], preferred_element_type=F32)
            acc = acc + jnp.maximum(s, 0.0) * w[h:h + 1, :]
        key = _sortable_key(acc)
        keys_ref[kb] = jnp.where(kb * TQ + s_loc <= t_pos, key, INT_MIN)
        return carry

    lax.fori_loop(0, nkb, score_blk, 0)

    @pl.when(nkb % 2 == 1)
    def _():
        keys_ref[nkb] = jnp.full((TQ, TQ), INT_MIN, I32)

    def count(indicator):
        def body(i, acc):
            for d in range(2):
                kb = 2 * i + d
                x = indicator(keys_ref[kb], kb * TQ + s_loc)
                acc = acc + jnp.sum(x.reshape(TQ // SUBLANES, SUBLANES, TQ), axis=0)
            return acc
        acc = lax.fori_loop(0, (nkb + 1) // 2, body, jnp.zeros((SUBLANES, TQ), F32))
        return jnp.sum(acc, axis=0, keepdims=True)

    k_eff = jnp.minimum(TOPK, t_pos + 1).astype(F32)
    zero = jnp.zeros((1, TQ), I32)
    cnt = count(lambda key, s: jnp.where(key >= zero, 1.0, 0.0))
    tau0 = jnp.where(cnt >= k_eff, 0, INT_MIN).astype(I32)

    def bit_step(i, tau):
        cand = tau + jnp.left_shift(jnp.int32(1), 30 - i)
        cnt = count(lambda key, s: jnp.where(key >= cand, 1.0, 0.0))
        return jnp.where(cnt >= k_eff, cand, tau)

    tau = lax.fori_loop(0, 31, bit_step, tau0)

    need = k_eff - count(lambda key, s: jnp.where(key > tau, 1.0, 0.0))
    n_eq = count(lambda key, s: jnp.where(key == tau, 1.0, 0.0))
    cut_ref[...] = jnp.full((1, TQ), BIG_IDX, I32)

    @pl.when(jnp.max(n_eq - need) > 0.0)
    def _():
        def idx_step(i, c):
            trial = c + jnp.left_shift(jnp.int32(1), 10 - i)
            cnt = count(lambda key, s: jnp.where(key == tau, jnp.where(s < trial, 1.0, 0.0), 0.0))
            return jnp.where(cnt < need, trial, c)
        cut_ref[...] = lax.fori_loop(0, 11, idx_step, jnp.zeros((1, TQ), I32))

    cut = cut_ref[...]

    def sel_blk(kb, carry):
        key = keys_ref[kb]
        s_pos = kb * TQ + s_loc
        sel_ref[kb] = jnp.where(
            key > tau, 0.0,
            jnp.where(key == tau, jnp.where(s_pos <= cut, 0.0, NEG_LARGE), NEG_LARGE))
        return carry

    lax.fori_loop(0, nkb, sel_blk, 0)

    n_sub = TQ // TK_SUB

    def logits(h, kb, u):
        hs = slice(h * HEAD_DIM, (h + 1) * HEAD_DIM)
        r0 = pl.multiple_of(kb * TQ + u * TK_SUB, TK_SUB)
        us = slice(u * TK_SUB, (u + 1) * TK_SUB)
        dc = jnp.minimum(qb - kb, 2)
        s = jnp.dot(k_ref[pl.ds(r0, TK_SUB), hs], qT_ref[0, hs, :], preferred_element_type=F32)
        return s * ATTN_SCALE + bias_ref[dc, h, us, :] + sel_ref[kb, us, :]

    def fold8(x):
        return x.reshape(TK_SUB // SUBLANES, SUBLANES, TQ)

    def max_blk(kb, ms):
        out = []
        for h in range(N_HEADS):
            m8 = ms[h]
            for u in range(n_sub):
                m8 = jnp.maximum(m8, jnp.max(fold8(logits(h, kb, u)), axis=0))
            out.append(m8)
        return tuple(out)

    ms = lax.fori_loop(0, nkb, max_blk,
                       tuple(jnp.full((SUBLANES, TQ), NEG_LARGE, F32) for _ in range(N_HEADS)))
    m = [jnp.max(m8, axis=0, keepdims=True) for m8 in ms]
    acc_ref[...] = jnp.zeros(acc_ref.shape, F32)

    def pv_blk(kb, ls):
        out = []
        for h in range(N_HEADS):
            l8 = ls[h]
            for u in range(n_sub):
                p = jnp.exp(logits(h, kb, u) - m[h])
                l8 = l8 + jnp.sum(fold8(p), axis=0)
                p_ref[h, u * TK_SUB:(u + 1) * TK_SUB, :] = p.astype(BF16)
            out.append(l8)
        for h in range(N_HEADS):
            hs = slice(h * HEAD_DIM, (h + 1) * HEAD_DIM)
            acc_ref[h] += jnp.dot(vT_ref[0, kb, hs, :], p_ref[h], preferred_element_type=F32)
        return tuple(out)

    ls = lax.fori_loop(0, nkb, pv_blk,
                       tuple(jnp.zeros((SUBLANES, TQ), F32) for _ in range(N_HEADS)))
    for h in range(N_HEADS):
        o = acc_ref[h] / jnp.sum(ls[h], axis=0, keepdims=True)
        o_ref[:, h * HEAD_DIM:(h + 1) * HEAD_DIM] = o.T.astype(o_ref.dtype)


def _attn_prompt(qT, qiT, wT, knb, vT, ki_all, bias, layer):
    return pl.pallas_call(
        _attn_kernel,
        grid=(BATCH, NQB),
        in_specs=[
            pl.BlockSpec((1, WIDTH, TQ), lambda b, q: (b, 0, q)),
            pl.BlockSpec((1, N_HEADS * IDX_DIM, TQ), lambda b, q: (b, 0, q)),
            pl.BlockSpec((1, N_HEADS, TQ), lambda b, q: (b, 0, q)),
            pl.BlockSpec((SEQ, WIDTH), lambda b, q: (b, 0)),
            pl.BlockSpec((1, NQB, WIDTH, TQ), lambda b, q: (b, 0, 0, 0)),
            pl.BlockSpec((None, SEQ, LANES), lambda b, q: (layer, b, 0)),
            pl.BlockSpec((3, N_HEADS, TQ, TQ), lambda b, q: (0, 0, 0, 0)),
            pl.BlockSpec(memory_space=pl.ANY),
        ],
        out_specs=pl.BlockSpec((TQ, WIDTH), lambda b, q: (b * NQB + q, 0)),
        out_shape=jax.ShapeDtypeStruct((M_ALL, WIDTH), BF16),
        input_output_aliases={7: 0},
        scratch_shapes=[
            pltpu.VMEM((NQB, TQ, TQ), I32),
            pltpu.VMEM((NQB, TQ, TQ), F32),
            pltpu.VMEM((N_HEADS, HEAD_DIM, TQ), F32),
            pltpu.VMEM((N_HEADS, TQ, TQ), BF16),
            pltpu.VMEM((1, TQ), I32),
        ],
        compiler_params=_cparams(("arbitrary", "arbitrary")),
        name="attn_prompt",
    )(qT, qiT, wT, knb, vT, ki_all, bias, jnp.zeros((M_ALL, WIDTH), BF16))


def _hgrn_kernel(zq_ref, zf_ref, zi_ref, zg_ref, lb_ref, og_ref, _rows_ref, ob_ref, st_ref,
                 ST_ref, b_ref, k_ref, q_ref, i_ref, oi_ref):
    ST_ref[...] = jnp.zeros(ST_ref.shape, F32)
    og = og_ref[...]
    row = lax.broadcasted_iota(I32, (CHUNK, LANES), 0)
    lane = lax.broadcasted_iota(I32, (CHUNK, LANES), 1)
    sub_row = lax.broadcasted_iota(I32, (SUBLANES, LANES), 0)

    def chunk(c, carry):
        r0 = pl.multiple_of(c * CHUNK, CHUNK)
        b_last = []
        for hh in range(HPB):
            hs = slice(hh * LANES, (hh + 1) * LANES)
            lb = lb_ref[hh]
            f = lb + (1.0 - lb) * _sigmoid(zf_ref[pl.ds(r0, CHUNK), hs])
            b = jnp.log(f)
            for d in (1, 2, 4, 8):
                b = b + jnp.where(row % SUB >= d, pltpu.roll(b, d, axis=0), 0.0)
            b_ref[hh] = b
            k_ref[hh] = 1.0 - f
            q_ref[hh] = _silu(zq_ref[pl.ds(r0, CHUNK), hs])
            i_ref[hh] = zi_ref[pl.ds(r0, CHUNK), hs]
            b_last.append(b_ref[hh, pl.ds(SUB - 1, N_SUB, stride=SUB), :])

        def sub_block(j, carry2):
            j0 = pl.multiple_of(j * SUB, SUB)
            for hh in range(HPB):
                hs = slice(hh * LANES, (hh + 1) * LANES)
                b_lo = b_ref[hh, pl.ds(j0, SUBLANES), :]
                b_hi = b_ref[hh, pl.ds(j0 + SUBLANES, SUBLANES), :]
                q_lo = q_ref[hh, pl.ds(j0, SUBLANES), :]
                q_hi = q_ref[hh, pl.ds(j0 + SUBLANES, SUBLANES), :]
                o_lo = jnp.zeros((SUBLANES, LANES), F32)
                o_hi = jnp.zeros((SUBLANES, LANES), F32)
                for s in range(SUB):
                    bs = jnp.broadcast_to(b_ref[hh, pl.ds(j0 + s, 1), :], (SUBLANES, LANES))
                    ks = jnp.broadcast_to(k_ref[hh, pl.ds(j0 + s, 1), :], (SUBLANES, LANES))
                    iv = jnp.broadcast_to(i_ref[hh, pl.ds(j0 + s, 1), :], (SUBLANES, LANES))
                    if s < SUBLANES:
                        ok = sub_row >= s
                        x = jnp.where(ok, q_lo * ks * jnp.exp(jnp.where(ok, b_lo - bs, 0.0)), 0.0)
                        o_lo = o_lo + jnp.sum(x, axis=-1, keepdims=True) * iv
                        x = q_hi * ks * jnp.exp(b_hi - bs)
                        o_hi = o_hi + jnp.sum(x, axis=-1, keepdims=True) * iv
                    else:
                        ok = sub_row >= s - SUBLANES
                        x = jnp.where(ok, q_hi * ks * jnp.exp(jnp.where(ok, b_hi - bs, 0.0)), 0.0)
                        o_hi = o_hi + jnp.sum(x, axis=-1, keepdims=True) * iv
                oi_ref[hh, pl.ds(j0, SUBLANES), :] = o_lo
                oi_ref[hh, pl.ds(j0 + SUBLANES, SUBLANES), :] = o_hi
            return carry2

        lax.fori_loop(0, N_SUB, sub_block, 0)

        for hh in range(HPB):
            hs = slice(hh * LANES, (hh + 1) * LANES)
            b = b_ref[hh]
            bl_b = jnp.concatenate(
                [jnp.broadcast_to(b_last[hh][j:j + 1, :], (SUB, LANES)) for j in range(N_SUB)], axis=0)
            q_dec = (q_ref[hh] * jnp.exp(b)).astype(BF16)
            k_end = (k_ref[hh] * jnp.exp(bl_b - b)).astype(BF16)
            decay = jnp.exp(b_last[hh])
            iT = i_ref[hh].T
            it_stack = jnp.concatenate(
                [jnp.where(lane // SUB == j, iT, 0.0).astype(BF16) for j in range(N_SUB)], axis=0)
            pT = jnp.dot(it_stack, k_end, preferred_element_type=F32)
            r = ST_ref[hh]
            r_list = []
            for j in range(N_SUB):
                r_list.append(r.astype(BF16))
                r = r * decay[j:j + 1, :] + pT[j * LANES:(j + 1) * LANES, :]
            ST_ref[hh] = r
            r_stack = jnp.concatenate(r_list, axis=0)
            oT_all = lax.dot_general(r_stack, q_dec, (((1,), (1,)), ((), ())),
                                     preferred_element_type=F32)
            oT = jnp.zeros((LANES, CHUNK), F32)
            for j in range(N_SUB):
                oT = oT + jnp.where(lane // SUB == j, oT_all[j * LANES:(j + 1) * LANES, :], 0.0)
            o = oT.T + oi_ref[hh]
            zg = zg_ref[pl.ds(r0, CHUNK), hs]
            ob_ref[pl.ds(r0, CHUNK), hs] = (_head_norm(o, og) * _silu(zg)).astype(ob_ref.dtype)
        return carry

    lax.fori_loop(0, SEQ // CHUNK, chunk, 0)
    for hh in range(HPB):
        st_ref[hh] = ST_ref[hh].T


def _hgrn_prompt(z, lb3, og):
    w = HPB * LANES
    zspec = lambda cb: pl.BlockSpec((SEQ, w), lambda b, h: (b, cb * COL // w + h))
    return pl.pallas_call(
        _hgrn_kernel,
        grid=(BATCH, N_HEADS // HPB),
        in_specs=[zspec(CB_HQ), zspec(CB_HF), zspec(CB_HI), zspec(CB_HG),
                  pl.BlockSpec((HPB, 1, LANES), lambda b, h: (h, 0, 0)),
                  pl.BlockSpec((1, LANES), lambda b, h: (0, 0)),
                  pl.BlockSpec(memory_space=pl.ANY)],
        out_specs=[pl.BlockSpec((SEQ, w), lambda b, h: (b, h)),
                   pl.BlockSpec((None, HPB, LANES, LANES), lambda b, h: (b, h, 0, 0))],
        out_shape=[jax.ShapeDtypeStruct((M_ALL, WIDTH), BF16),
                   jax.ShapeDtypeStruct((BATCH, N_HEADS, LANES, LANES), F32)],
        input_output_aliases={6: 0},
        scratch_shapes=[pltpu.VMEM((HPB, LANES, LANES), F32),
                        pltpu.VMEM((HPB, CHUNK, LANES), F32),
                        pltpu.VMEM((HPB, CHUNK, LANES), F32),
                        pltpu.VMEM((HPB, CHUNK, LANES), F32),
                        pltpu.VMEM((HPB, CHUNK, LANES), F32),
                        pltpu.VMEM((HPB, CHUNK, LANES), F32)],
        compiler_params=_cparams(("arbitrary", "arbitrary")),
        name="hgrn_prompt",
    )(z, z, z, z, lb3, og, jnp.zeros((M_ALL, WIDTH), BF16))


def _rows_to_cols(x):
    pad = jnp.zeros((LANES - x.shape[0], LANES), F32)
    return jnp.concatenate([x, pad], axis=0).T


def _hgrn_sample_kernel(zq_ref, zf_ref, zi_ref, zg_ref, lb_ref, og_ref, s0_ref, ob_any_ref,
                        ob_ref, s1_ref):
    del ob_any_ref
    lb = lb_ref[...]
    f = lb + (1.0 - lb) * _sigmoid(zf_ref[...])
    fT = _rows_to_cols(f)
    kT = _rows_to_cols(1.0 - f)
    qT = _rows_to_cols(_silu(zq_ref[...]))
    iv = zi_ref[...]
    rows = []
    for b in range(DEC_BATCH):
        s1 = fT[:, b:b + 1] * s0_ref[b] + kT[:, b:b + 1] * iv[b:b + 1, :]
        s1_ref[b] = s1
        rows.append(jnp.sum(qT[:, b:b + 1] * s1, axis=0, keepdims=True))
    o = jnp.concatenate(rows, axis=0)
    o = _head_norm(o, og_ref[...]) * _silu(zg_ref[...])
    pad = jnp.zeros((LANES - DEC_BATCH, LANES), F32)
    ob_ref[...] = jnp.concatenate([o, pad], axis=0).astype(ob_ref.dtype)


def _hgrn_sample(z, lb3, og, state_hgrn, ob_all, layer):
    cpb = COL // LANES
    zspec = lambda cb: pl.BlockSpec((DEC_BATCH, LANES), lambda h: (SAMPLE_ROW_BLOCK, cb * cpb + h))
    return pl.pallas_call(
        _hgrn_sample_kernel,
        grid=(N_HEADS,),
        in_specs=[zspec(CB_HQ), zspec(CB_HF), zspec(CB_HI), zspec(CB_HG),
                  pl.BlockSpec((None, 1, LANES), lambda h: (h, 0, 0)),
                  pl.BlockSpec((1, LANES), lambda h: (0, 0)),
                  pl.BlockSpec((None, DEC_BATCH, None, LANES, LANES), lambda h: (layer, 0, h, 0, 0)),
                  pl.BlockSpec(memory_space=pl.ANY)],
        out_specs=[pl.BlockSpec((LANES, LANES), lambda h: (SAMPLE_TILE, h)),
                   pl.BlockSpec((DEC_BATCH, None, LANES, LANES), lambda h: (0, h, 0, 0))],
        out_shape=[jax.ShapeDtypeStruct((M_ALL, WIDTH), BF16),
                   jax.ShapeDtypeStruct((DEC_BATCH, N_HEADS, LANES, LANES), F32)],
        input_output_aliases={7: 0},
        compiler_params=_cparams(("arbitrary",)),
        name="hgrn_sample",
    )(z, z, z, z, lb3, og, state_hgrn, ob_all)


def _sample_norm_kernel(zq_ref, zk_ref, zkiw_ref, gq_ref, gk_ref, gki_ref, qn_ref, kn_ref, kiw_ref):
    gq = gq_ref[...]
    gk = gk_ref[...]
    for h in range(N_HEADS):
        hs = slice(h * HEAD_DIM, (h + 1) * HEAD_DIM)
        qn_ref[:, hs] = _head_norm(zq_ref[:, hs], gq)
        kn_ref[:, hs] = _head_norm(zk_ref[:, hs], gk)
    kiw = zkiw_ref[:, 0:LANES]
    lane = lax.broadcasted_iota(I32, kiw.shape, 1)
    kiw_ref[...] = _kidx_norm(kiw, gki_ref[...]) + jnp.where(
        (lane >= IDX_DIM) & (lane < IDX_DIM + N_HEADS), kiw * IDX_WEIGHT_SCALE, 0.0)


def _sample_norms(z, gq, gk, gki_pad):
    zspec = lambda cb, w: pl.BlockSpec((DEC_BATCH, w), lambda i: (SAMPLE_ROW_BLOCK, cb * COL // w))
    gspec = pl.BlockSpec((1, LANES), lambda i: (0, 0))
    return pl.pallas_call(
        _sample_norm_kernel,
        grid=(1,),
        in_specs=[zspec(CB_Q, WIDTH), zspec(CB_K, WIDTH), zspec(CB_KIW, COL), gspec, gspec, gspec],
        out_specs=[pl.BlockSpec((DEC_BATCH, WIDTH), lambda i: (0, 0)),
                   pl.BlockSpec((DEC_BATCH, WIDTH), lambda i: (0, 0)),
                   pl.BlockSpec((DEC_BATCH, LANES), lambda i: (0, 0))],
        out_shape=[jax.ShapeDtypeStruct((DEC_BATCH, WIDTH), F32),
                   jax.ShapeDtypeStruct((DEC_BATCH, WIDTH), F32),
                   jax.ShapeDtypeStruct((DEC_BATCH, LANES), F32)],
        compiler_params=_cparams(("arbitrary",)),
        name="sample_norms",
    )(z, z, z, gq, gk, gki_pad)


SCORE_W = PAST_LEN + LANES


def _select_row(x, b):
    sub = lax.broadcasted_iota(I32, x.shape, 0)
    return jnp.sum(jnp.where(sub == b, x, 0.0), axis=0, keepdims=True)


def _sample_topk_kernel(pt_ref, zqi_ref, kiw_ref, cache_ref, sel_ref, buf_ref, sem_ref, sc_ref,
                        *, layer):
    b = pl.program_id(0)

    def page_copy(p):
        return pltpu.make_async_copy(cache_ref.at[layer, pt_ref[b, p]], buf_ref.at[p], sem_ref.at[0])

    def start(p, c):
        page_copy(p).start()
        return c

    lax.fori_loop(0, N_PAGES, start, 0)

    qrow = _select_row(zqi_ref[...], b)
    head = lax.broadcasted_iota(I32, (N_HEADS, N_HEADS * IDX_DIM), 0)
    col = lax.broadcasted_iota(I32, (N_HEADS, N_HEADS * IDX_DIM), 1)
    qm = jnp.where(col // IDX_DIM == head, jnp.broadcast_to(qrow, head.shape), 0.0)
    q128 = qm[:, 0:LANES]
    for p in range(1, N_HEADS * IDX_DIM // LANES):
        q128 = q128 + qm[:, p * LANES:(p + 1) * LANES]
    q8 = (q128[:, 0:IDX_DIM] + q128[:, IDX_DIM:]).astype(BF16)

    kiw_row = _select_row(kiw_ref[...], b)
    r8 = lax.broadcasted_iota(I32, (N_HEADS, LANES), 0)
    l8 = lax.broadcasted_iota(I32, (N_HEADS, LANES), 1)
    wcol = jnp.sum(jnp.where(l8 == IDX_DIM + r8, jnp.broadcast_to(kiw_row, (N_HEADS, LANES)), 0.0),
                   axis=-1, keepdims=True)

    def wait(p, c):
        page_copy(p).wait()
        return c

    lax.fori_loop(0, N_PAGES, wait, 0)

    nt = (((1,), (1,)), ((), ()))
    keys = buf_ref[...].reshape(PAST_LEN, IDX_DIM).astype(BF16)
    s = lax.dot_general(q8, keys, nt, preferred_element_type=F32)
    s_past = jnp.sum(jnp.maximum(s, 0.0) * wcol, axis=0, keepdims=True)
    knew8 = jnp.broadcast_to(kiw_row[:, 0:IDX_DIM], (SUBLANES, IDX_DIM)).astype(BF16)
    sn = lax.dot_general(q8, knew8, nt, preferred_element_type=F32)
    s_new = jnp.sum(jnp.maximum(sn[:, 0:1], 0.0) * wcol, axis=0, keepdims=True)
    lane1 = lax.broadcasted_iota(I32, (1, LANES), 1)
    tail = jnp.where(lane1 == 0, jnp.broadcast_to(s_new, (1, LANES)), -jnp.inf)
    srow = jnp.concatenate([s_past, tail], axis=1)
    srow = jnp.where(srow == 0.0, 0.0, srow)

    @pl.when(b == 0)
    def _():
        sc_ref[...] = jnp.full(sc_ref.shape, -jnp.inf, F32)

    sub = lax.broadcasted_iota(I32, (DEC_BATCH, SCORE_W), 0)
    sc_ref[...] = jnp.where(sub == b, jnp.broadcast_to(srow, (DEC_BATCH, SCORE_W)), sc_ref[...])

    @pl.when(b == DEC_BATCH - 1)
    def _():
        pos = lax.broadcasted_iota(I32, (DEC_BATCH, SCORE_W), 1).astype(F32)
        out_lane = lax.broadcasted_iota(I32, (DEC_BATCH, TOPK), 1)

        def pick(it, sel):
            sc = sc_ref[...]
            m = jnp.max(sc, axis=1, keepdims=True)
            first = jnp.min(jnp.where(sc == m, pos, float(BIG_IDX)), axis=1, keepdims=True)
            sc_ref[...] = jnp.where(pos == first, -jnp.inf, sc)
            return jnp.where(out_lane == it, first.astype(I32), sel)

        sel_ref[...] = lax.fori_loop(0, TOPK, pick, jnp.zeros((DEC_BATCH, TOPK), I32))


def _sample_topk(page_table, z, kiw_s, cache_kidx, layer):
    grid_spec = pltpu.PrefetchScalarGridSpec(
        num_scalar_prefetch=1,
        grid=(DEC_BATCH,),
        in_specs=[pl.BlockSpec((DEC_BATCH, COL), lambda b, pt: (SAMPLE_ROW_BLOCK, CB_QI)),
                  pl.BlockSpec((DEC_BATCH, LANES), lambda b, pt: (0, 0)),
                  pl.BlockSpec(memory_space=pl.ANY)],
        out_specs=pl.BlockSpec((DEC_BATCH, TOPK), lambda b, pt: (0, 0)),
        scratch_shapes=[pltpu.VMEM((N_PAGES, PAGE_SIZE, IDX_DIM), F32),
                        pltpu.SemaphoreType.DMA((1,)),
                        pltpu.VMEM((DEC_BATCH, SCORE_W), F32)],
    )
    return pl.pallas_call(
        functools.partial(_sample_topk_kernel, layer=layer),
        grid_spec=grid_spec,
        out_shape=jax.ShapeDtypeStruct((DEC_BATCH, TOPK), I32),
        compiler_params=_cparams(("arbitrary",)),
        name="sample_topk",
    )(page_table, z, kiw_s, cache_kidx)


def _sample_attn_kernel(sel_s_ref, pt_ref, sel_ref, qn_ref, kn_ref, zv_ref, rbT_ref,
                        ck_ref, cv_ref, oa_any_ref, o_ref, kbuf_ref, vbuf_ref, sem_ref, rows_ref,
                        *, layer):
    del oa_any_ref
    b = pl.program_id(0)

    def copies(j):
        sp = jnp.minimum(sel_s_ref[b, j], PAST_LEN - 1)
        page = pt_ref[b, sp // PAGE_SIZE]
        off = sp % PAGE_SIZE
        dst = pl.ds(pl.multiple_of(j * N_HEADS, N_HEADS), N_HEADS)
        return (pltpu.make_async_copy(ck_ref.at[layer, page, off], kbuf_ref.at[dst], sem_ref.at[0]),
                pltpu.make_async_copy(cv_ref.at[layer, page, off], vbuf_ref.at[dst], sem_ref.at[1]))

    def start(j, c):
        ck, cv = copies(j)
        ck.start()
        cv.start()
        return c

    lax.fori_loop(0, TOPK, start, 0)

    def heads_on_rows(x):
        row = _select_row(x, b)
        return jnp.concatenate([row[:, h * HEAD_DIM:(h + 1) * HEAD_DIM] for h in range(N_HEADS)],
                               axis=0)

    q = heads_on_rows(qn_ref[...])
    k_new = heads_on_rows(kn_ref[...])
    v_new = heads_on_rows(zv_ref[...])
    sel_b = jnp.sum(jnp.where(lax.broadcasted_iota(I32, (DEC_BATCH, TOPK), 0) == b, sel_ref[...], 0),
                    axis=0, keepdims=True)
    is_new = sel_b >= PAST_LEN
    bucket = _t5_bucket(jnp.maximum(PAST_LEN - sel_b, 0))
    bias = jnp.zeros((N_HEADS, TOPK), F32)
    for beta in range(NUM_BUCKETS):
        bias = jnp.where(bucket == beta, rbT_ref[:, beta:beta + 1], bias)

    def wait(j, c):
        ck, cv = copies(j)
        ck.wait()
        cv.wait()
        return c

    lax.fori_loop(0, TOPK, wait, 0)

    nt = (((1,), (1,)), ((), ()))
    hrow = lax.broadcasted_iota(I32, (N_HEADS, TOPK), 0)
    qb16 = q.astype(BF16)
    logits = jnp.zeros((N_HEADS, TOPK), F32)
    for h in range(N_HEADS):
        kh = kbuf_ref[pl.ds(h, TOPK, stride=N_HEADS), :].astype(BF16)
        res = lax.dot_general(qb16, kh, nt, preferred_element_type=F32)
        logits = jnp.where(hrow == h, res, logits)
    logit_new = jnp.sum(qb16.astype(F32) * k_new.astype(BF16).astype(F32), axis=-1, keepdims=True)
    logits = jnp.where(is_new, logit_new, logits) * ATTN_SCALE + bias
    m = jnp.max(logits, axis=1, keepdims=True)
    e = jnp.exp(logits - m)
    p = e / jnp.sum(e, axis=1, keepdims=True)
    p_new = jnp.sum(jnp.where(is_new, p, 0.0), axis=1, keepdims=True)
    p_mm = jnp.where(is_new, 0.0, p).astype(BF16)
    hrow_o = lax.broadcasted_iota(I32, (N_HEADS, HEAD_DIM), 0)
    o = jnp.zeros((N_HEADS, HEAD_DIM), F32)
    for h in range(N_HEADS):
        vh = vbuf_ref[pl.ds(h, TOPK, stride=N_HEADS), :].astype(BF16)
        res = jnp.dot(p_mm, vh, preferred_element_type=F32)
        o = jnp.where(hrow_o == h, res, o)
    o = o + p_new.astype(BF16).astype(F32) * v_new.astype(BF16).astype(F32)
    orow = jnp.concatenate([o[h:h + 1, :] for h in range(N_HEADS)], axis=1)

    @pl.when(b == 0)
    def _():
        rows_ref[...] = jnp.zeros(rows_ref.shape, F32)

    sub = lax.broadcasted_iota(I32, (DEC_BATCH, WIDTH), 0)
    rows_ref[...] = jnp.where(sub == b, jnp.broadcast_to(orow, (DEC_BATCH, WIDTH)), rows_ref[...])

    @pl.when(b == DEC_BATCH - 1)
    def _():
        pad = jnp.zeros((LANES - DEC_BATCH, WIDTH), F32)
        o_ref[...] = jnp.concatenate([rows_ref[...], pad], axis=0).astype(o_ref.dtype)


def _sample_attn(sel, page_table, qn_s, kn_s, z, rbT, cache_k, cache_v, oa_all, layer):
    grid_spec = pltpu.PrefetchScalarGridSpec(
        num_scalar_prefetch=2,
        grid=(DEC_BATCH,),
        in_specs=[pl.BlockSpec((DEC_BATCH, TOPK), lambda b, s, pt: (0, 0)),
                  pl.BlockSpec((DEC_BATCH, WIDTH), lambda b, s, pt: (0, 0)),
                  pl.BlockSpec((DEC_BATCH, WIDTH), lambda b, s, pt: (0, 0)),
                  pl.BlockSpec((DEC_BATCH, WIDTH), lambda b, s, pt: (SAMPLE_ROW_BLOCK, CB_V * COL // WIDTH)),
                  pl.BlockSpec((N_HEADS, NUM_BUCKETS), lambda b, s, pt: (0, 0)),
                  pl.BlockSpec(memory_space=pl.ANY),
                  pl.BlockSpec(memory_space=pl.ANY),
                  pl.BlockSpec(memory_space=pl.ANY)],
        out_specs=pl.BlockSpec((LANES, WIDTH), lambda b, s, pt: (SAMPLE_TILE, 0)),
        scratch_shapes=[pltpu.VMEM((TOPK * N_HEADS, HEAD_DIM), F32),
                        pltpu.VMEM((TOPK * N_HEADS, HEAD_DIM), F32),
                        pltpu.SemaphoreType.DMA((2,)),
                        pltpu.VMEM((DEC_BATCH, WIDTH), F32)],
    )
    return pl.pallas_call(
        functools.partial(_sample_attn_kernel, layer=layer),
        grid_spec=grid_spec,
        out_shape=jax.ShapeDtypeStruct((M_ALL, WIDTH), BF16),
        input_output_aliases={9: 0},
        compiler_params=_cparams(("arbitrary",)),
        name="sample_attn",
    )(sel, page_table, sel, qn_s, kn_s, z, rbT, cache_k, cache_v, oa_all)


def _permute_w_in(w_in):
    parts = []
    off = 0
    for w in IN_WIDTHS:
        parts.append(w_in[..., off:off + w])
        off += w
    q, k, v, qi, ki, wi, hq, hf, hi, hg, ga, gb = parts
    pad = jnp.zeros(w_in.shape[:-1] + (COL - IDX_DIM - N_HEADS,), w_in.dtype)
    return jnp.concatenate([q, k, v, qi, ki, wi, pad, hq, hf, hi, hg, ga, gb], axis=-1).astype(BF16)


def _stack_rows(prompt_rows, sample_rows):
    pad = jnp.zeros((M_ALL - N_PROMPT - DEC_BATCH, prompt_rows.shape[1]), prompt_rows.dtype)
    return jnp.concatenate([prompt_rows, sample_rows.astype(prompt_rows.dtype), pad], axis=0)


def kernel(x_prompt, x_sample, cache_k, cache_v, cache_kidx, state_hgrn, page_table, norm_mix_g, w_in,
           q_norm_g, k_norm_g, kidx_norm_g, rel_bias, hgrn_lb, hgrn_out_g, w_up_a, w_up_b, w_out,
           norm_ffn_g, w_ffn_gate, w_ffn_up, w_ffn_down):
    w_in_p = _permute_w_in(w_in)
    w_up_a, w_up_b, w_out = (w.astype(BF16) for w in (w_up_a, w_up_b, w_out))
    w_ffn_gate, w_ffn_up, w_ffn_down = (w.astype(BF16) for w in (w_ffn_gate, w_ffn_up, w_ffn_down))
    lower = _lower_bounds(hgrn_lb)
    bias = _bias_tiles(rel_bias)
    rbT = rel_bias.T
    gki_pad = jnp.pad(kidx_norm_g, ((0, 0), (0, LANES - IDX_DIM)))

    x = _stack_rows(x_prompt.reshape(N_PROMPT, D_MODEL), x_sample.reshape(DEC_BATCH, D_MODEL))
    stacked = (jnp.zeros((DEPTH, N_PROMPT, WIDTH), F32), jnp.zeros((DEPTH, N_PROMPT, WIDTH), F32),
               jnp.zeros((DEPTH, N_PROMPT, LANES), F32))
    outs = [[] for _ in range(5)]
    for l in range(DEPTH):
        gq = q_norm_g[l].reshape(1, HEAD_DIM)
        gk = k_norm_g[l].reshape(1, HEAD_DIM)
        gki = gki_pad[l].reshape(1, LANES)
        og = hgrn_out_g[l].reshape(1, LANES)
        lb3 = lower[l].reshape(N_HEADS, 1, LANES)

        xn = _rmsnorm(x, norm_mix_g[l])
        z = _in_proj(xn, w_in_p, l)

        qT, qiT, wT, knb, vT, *stacked = _post(z, gq, gk, gki, l, stacked)
        oa = _attn_prompt(qT, qiT, wT, knb, vT, stacked[2], bias, l)
        qn_s, kn_s, kiw_s = _sample_norms(z, gq, gk, gki)
        sel = _sample_topk(page_table, z, kiw_s, cache_kidx, l)
        oa = _sample_attn(sel, page_table, qn_s, kn_s, z, rbT, cache_k, cache_v, oa, l)

        ob, st_p = _hgrn_prompt(z, lb3, og)
        ob, st_s = _hgrn_sample(z, lb3, og, state_hgrn, ob, l)

        mixed = _merge(oa, ob, w_up_a, w_up_b, z, l)
        x = _residual_proj(mixed, w_out, x, l, TM, "out_proj")
        hn = _rmsnorm(x, norm_ffn_g[l])
        x = _residual_proj(_ffn_up(hn, w_ffn_gate, w_ffn_up, l), w_ffn_down, x, l, TM_DOWN, "ffn_down")

        v_cols = slice(CB_V * COL, CB_V * COL + WIDTH)
        outs[0].append(st_p)
        outs[1].append(kn_s.reshape(DEC_BATCH, 1, N_HEADS, HEAD_DIM))
        outs[2].append(z[N_PROMPT:N_PROMPT + DEC_BATCH, v_cols].reshape(DEC_BATCH, 1, N_HEADS, HEAD_DIM))
        outs[3].append(kiw_s[:, :IDX_DIM].reshape(DEC_BATCH, 1, IDX_DIM))
        outs[4].append(st_s)

    k_all, v_all, ki_all = stacked
    y_prompt = x[:N_PROMPT].reshape(BATCH, SEQ, D_MODEL)
    y_sample = x[N_PROMPT:N_PROMPT + DEC_BATCH].reshape(DEC_BATCH, 1, D_MODEL)
    st_p, k_s, v_s, ki_s, st_s = (jnp.stack(o) for o in outs)
    return (y_prompt, y_sample,
            k_all.reshape(DEPTH, BATCH, SEQ, N_HEADS, HEAD_DIM),
            v_all.reshape(DEPTH, BATCH, SEQ, N_HEADS, HEAD_DIM),
            ki_all[:, :, :IDX_DIM].reshape(DEPTH, BATCH, SEQ, IDX_DIM),
            st_p, k_s, v_s, ki_s, st_s)
```

```python
import functools
import math

import jax
import jax.numpy as jnp
from jax import lax
from jax.experimental import pallas as pl
from jax.experimental.pallas import tpu as pltpu

F32 = jnp.float32
BF16 = jnp.bfloat16
I32 = jnp.int32
I16 = jnp.int16

D_MODEL = 2048
BATCH = 4
SEQ = 2048
DEPTH = 4
DEC_BATCH = 8
PAST_LEN = 16384
PAGE_SIZE = 128
N_PAGES = PAST_LEN // PAGE_SIZE
N_HEADS = 8
HEAD_DIM = 128
WIDTH = N_HEADS * HEAD_DIM
IDX_DIM = 64
TOPK = 256
ATTN_SCALE = HEAD_DIM ** -0.5
IDX_WEIGHT_SCALE = (N_HEADS ** -0.5) * (IDX_DIM ** -0.5)
NEG_LARGE = -1e30
NUM_BUCKETS = 32
MAX_DISTANCE = 128
D_FF = ((8 * D_MODEL // 3 + 255) // 256) * 256
EPS = 1e-6
IN_WIDTHS = (WIDTH, WIDTH, WIDTH, N_HEADS * IDX_DIM, IDX_DIM, N_HEADS,
             WIDTH, WIDTH, WIDTH, WIDTH, D_MODEL, D_MODEL)

SUBLANES = 8
LANES = 128
VMEM_LIMIT = 56 * 1024 * 1024

N_PROMPT = BATCH * SEQ
M_ALL = N_PROMPT + LANES
SAMPLE_ROW_BLOCK = N_PROMPT // SUBLANES
SAMPLE_TILE = N_PROMPT // LANES

COL = 512
NP_IN = 24 * COL
CB_Q, CB_K, CB_V, CB_QI, CB_KIW = 0, 2, 4, 6, 7
CB_HQ, CB_HF, CB_HI, CB_HG, CB_GA, CB_GB = 8, 10, 12, 14, 16, 20

TM = M_ALL // 5
TM_DOWN = M_ALL // 10
TM_NORM = M_ALL // 20
TN = 512

TQ = 256
NQB = SEQ // TQ
TK_SUB = 128
INT_MIN = -2 ** 31
HALF16 = 2 ** 15
BIG_IDX = 2 ** 30

SUB = 16
CHUNK = 128
N_SUB = CHUNK // SUB
HPB = 4


def _cparams(sem):
    return pltpu.CompilerParams(dimension_semantics=sem, vmem_limit_bytes=VMEM_LIMIT)


def _sigmoid(x):
    return 1.0 / (1.0 + jnp.exp(-x))


def _silu(x):
    return x * _sigmoid(x)


def _lb_kernel(lb_ref, o_ref):
    x = lb_ref[...]
    m = jnp.max(x, axis=0, keepdims=True)
    e = jnp.exp(x - m)
    p = e / jnp.sum(e, axis=0, keepdims=True)
    acc = jnp.zeros_like(p[0:1])
    rows = []
    for l in range(DEPTH):
        acc = acc + p[l:l + 1]
        rows.append(acc - p[0:1])
    o_ref[...] = jnp.concatenate(rows, axis=0)


def _lower_bounds(hgrn_lb):
    return pl.pallas_call(
        _lb_kernel,
        out_shape=jax.ShapeDtypeStruct((DEPTH, WIDTH), F32),
        name="hgrn_lower_bounds",
    )(hgrn_lb)


def _t5_bucket(n):
    max_exact = NUM_BUCKETS // 2
    nf = jnp.maximum(n, 1).astype(F32)
    large = max_exact + (jnp.log(nf / max_exact) / math.log(MAX_DISTANCE / max_exact)
                         * (NUM_BUCKETS - max_exact)).astype(I32)
    large = jnp.minimum(large, NUM_BUCKETS - 1)
    return jnp.where(n < max_exact, n, large)


def _bias_kernel(rb_ref, o_ref):
    dc = pl.program_id(0)
    s = lax.broadcasted_iota(I32, (TQ, TQ), 0)
    t = lax.broadcasted_iota(I32, (TQ, TQ), 1)
    bucket = _t5_bucket(jnp.maximum(dc * TQ + t - s, 0))
    for h in range(N_HEADS):
        acc = jnp.zeros((TQ, TQ), F32)
        for beta in range(NUM_BUCKETS):
            acc = jnp.where(bucket == beta, rb_ref[beta, h], acc)
        o_ref[0, h] = acc


def _bias_tiles(rel_bias):
    return pl.pallas_call(
        _bias_kernel,
        grid=(3,),
        in_specs=[pl.BlockSpec(memory_space=pltpu.SMEM)],
        out_specs=pl.BlockSpec((1, N_HEADS, TQ, TQ), lambda d: (d, 0, 0, 0)),
        out_shape=jax.ShapeDtypeStruct((3, N_HEADS, TQ, TQ), F32),
        compiler_params=_cparams(("arbitrary",)),
        name="bias_tiles",
    )(rel_bias)


def _rmsnorm_kernel(x_ref, g_ref, o_ref):
    x = x_ref[...]
    ms = jnp.mean(x * x, axis=-1, keepdims=True)
    o_ref[...] = (x * lax.rsqrt(ms + EPS) * g_ref[...]).astype(o_ref.dtype)


def _rmsnorm(x, g):
    m, d = x.shape
    return pl.pallas_call(
        _rmsnorm_kernel,
        grid=(m // TM_NORM,),
        in_specs=[pl.BlockSpec((TM_NORM, d), lambda i: (i, 0)),
                  pl.BlockSpec((1, d), lambda i: (0, 0))],
        out_specs=pl.BlockSpec((TM_NORM, d), lambda i: (i, 0)),
        out_shape=jax.ShapeDtypeStruct((m, d), BF16),
        compiler_params=_cparams(("arbitrary",)),
        name="rmsnorm",
    )(x, g.reshape(1, d))


def _wspec(k, layer):
    return pl.BlockSpec((None, k, TN), lambda i, j: (layer, 0, j))


def _in_proj_kernel(a_ref, w_ref, o_ref):
    o_ref[...] = jnp.dot(a_ref[...], w_ref[...], preferred_element_type=F32)


def _in_proj(xn, w_in_p, layer):
    return pl.pallas_call(
        _in_proj_kernel,
        grid=(M_ALL // TM, NP_IN // TN),
        in_specs=[pl.BlockSpec((TM, D_MODEL), lambda i, j: (i, 0)),
                  _wspec(D_MODEL, layer)],
        out_specs=pl.BlockSpec((TM, TN), lambda i, j: (i, j)),
        out_shape=jax.ShapeDtypeStruct((M_ALL, NP_IN), F32),
        compiler_params=_cparams(("arbitrary", "arbitrary")),
        name="in_proj",
    )(xn, w_in_p)


def _merge_kernel(oa_ref, ob_ref, wa_ref, wb_ref, ga_ref, gb_ref, o_ref):
    a = jnp.dot(oa_ref[...], wa_ref[...], preferred_element_type=F32)
    b = jnp.dot(ob_ref[...], wb_ref[...], preferred_element_type=F32)
    o_ref[...] = (_sigmoid(ga_ref[...]) * a + _sigmoid(gb_ref[...]) * b).astype(o_ref.dtype)


def _merge(oa, ob, wa, wb, z, layer):
    return pl.pallas_call(
        _merge_kernel,
        grid=(M_ALL // TM, D_MODEL // TN),
        in_specs=[pl.BlockSpec((TM, WIDTH), lambda i, j: (i, 0)),
                  pl.BlockSpec((TM, WIDTH), lambda i, j: (i, 0)),
                  _wspec(WIDTH, layer), _wspec(WIDTH, layer),
                  pl.BlockSpec((TM, TN), lambda i, j: (i, CB_GA + j)),
                  pl.BlockSpec((TM, TN), lambda i, j: (i, CB_GB + j))],
        out_specs=pl.BlockSpec((TM, TN), lambda i, j: (i, j)),
        out_shape=jax.ShapeDtypeStruct((M_ALL, D_MODEL), BF16),
        compiler_params=_cparams(("arbitrary", "arbitrary")),
        name="merge",
    )(oa, ob, wa, wb, z, z)


def _residual_proj_kernel(a_ref, w_ref, x_ref, o_ref):
    o_ref[...] = x_ref[...] + jnp.dot(a_ref[...], w_ref[...], preferred_element_type=F32)


def _residual_proj(a, w, x, layer, tm, name):
    k = a.shape[1]
    return pl.pallas_call(
        _residual_proj_kernel,
        grid=(M_ALL // tm, D_MODEL // TN),
        in_specs=[pl.BlockSpec((tm, k), lambda i, j: (i, 0)),
                  _wspec(k, layer),
                  pl.BlockSpec((tm, TN), lambda i, j: (i, j))],
        out_specs=pl.BlockSpec((tm, TN), lambda i, j: (i, j)),
        out_shape=jax.ShapeDtypeStruct((M_ALL, D_MODEL), F32),
        compiler_params=_cparams(("arbitrary", "arbitrary")),
        name=name,
    )(a, w, x)


def _ffn_up_kernel(a_ref, wg_ref, wu_ref, o_ref):
    a = a_ref[...]
    g = jnp.dot(a, wg_ref[...], preferred_element_type=F32)
    u = jnp.dot(a, wu_ref[...], preferred_element_type=F32)
    o_ref[...] = (_silu(g) * u).astype(o_ref.dtype)


def _ffn_up(hn, wg, wu, layer):
    return pl.pallas_call(
        _ffn_up_kernel,
        grid=(M_ALL // TM, D_FF // TN),
        in_specs=[pl.BlockSpec((TM, D_MODEL), lambda i, j: (i, 0)),
                  _wspec(D_MODEL, layer), _wspec(D_MODEL, layer)],
        out_specs=pl.BlockSpec((TM, TN), lambda i, j: (i, j)),
        out_shape=jax.ShapeDtypeStruct((M_ALL, D_FF), BF16),
        compiler_params=_cparams(("arbitrary", "arbitrary")),
        name="ffn_up",
    )(hn, wg, wu)


def _head_norm(x, g):
    ms = jnp.mean(x * x, axis=-1, keepdims=True)
    return x * lax.rsqrt(ms + EPS) * g


def _kidx_norm(kiw, g_pad):
    lane = lax.broadcasted_iota(I32, kiw.shape, 1)
    ki = jnp.where(lane < IDX_DIM, kiw, 0.0)
    ms = jnp.sum(ki * ki, axis=-1, keepdims=True) * (1.0 / IDX_DIM)
    return ki * lax.rsqrt(ms + EPS) * g_pad


def _post_kernel(zq_ref, zk_ref, zv_ref, zqi_ref, zkiw_ref, gq_ref, gk_ref, gki_ref, *rest):
    qT_ref, qiT_ref, wT_ref, knb_ref, vT_ref, kn_ref, v_ref, ki_ref = rest[-8:]
    gq = gq_ref[...]
    gk = gk_ref[...]
    for h in range(N_HEADS):
        hs = slice(h * HEAD_DIM, (h + 1) * HEAD_DIM)
        qn = _head_norm(zq_ref[:, hs], gq)
        qT_ref[0, hs, :] = qn.T.astype(BF16)
        kn = _head_norm(zk_ref[:, hs], gk)
        kn_ref[:, hs] = kn
        knb_ref[:, hs] = kn.astype(BF16)
        v = zv_ref[:, hs]
        v_ref[:, hs] = v
        vT_ref[0, 0, hs, :] = v.T.astype(BF16)
    for p in range(N_HEADS * IDX_DIM // LANES):
        ps = slice(p * LANES, (p + 1) * LANES)
        qiT_ref[0, ps, :] = zqi_ref[:, ps].T
    kiw = zkiw_ref[:, 0:LANES]
    ki_ref[...] = _kidx_norm(kiw, gki_ref[...])
    wT_ref[0] = (kiw * IDX_WEIGHT_SCALE).T[IDX_DIM:IDX_DIM + N_HEADS, :]


def _post(z, gq, gk, gki_pad, layer, stacked):
    nb = SEQ // LANES
    zspec = lambda cb, w: pl.BlockSpec((LANES, w), lambda r: (r, cb * COL // w))
    gspec = pl.BlockSpec((1, LANES), lambda r: (0, 0))
    in_specs = [zspec(CB_Q, WIDTH), zspec(CB_K, WIDTH), zspec(CB_V, WIDTH),
                zspec(CB_QI, COL), zspec(CB_KIW, COL), gspec, gspec, gspec]
    args = [z, z, z, z, z, gq, gk, gki_pad]
    aliases = {}
    for n, buf in enumerate(stacked):
        aliases[len(args)] = 5 + n
        in_specs.append(pl.BlockSpec(memory_space=pl.ANY))
        args.append(buf)
    return pl.pallas_call(
        _post_kernel,
        grid=(N_PROMPT // LANES,),
        in_specs=in_specs,
        out_specs=[
            pl.BlockSpec((1, WIDTH, LANES), lambda r: (r // nb, 0, r % nb)),
            pl.BlockSpec((1, N_HEADS * IDX_DIM, LANES), lambda r: (r // nb, 0, r % nb)),
            pl.BlockSpec((1, N_HEADS, LANES), lambda r: (r // nb, 0, r % nb)),
            pl.BlockSpec((LANES, WIDTH), lambda r: (r, 0)),
            pl.BlockSpec((1, 1, WIDTH, LANES),
                         lambda r: (r // nb, (r % nb) // (TQ // LANES), 0, r % (TQ // LANES))),
            pl.BlockSpec((None, LANES, WIDTH), lambda r: (layer, r, 0)),
            pl.BlockSpec((None, LANES, WIDTH), lambda r: (layer, r, 0)),
            pl.BlockSpec((None, LANES, LANES), lambda r: (layer, r, 0)),
        ],
        out_shape=[
            jax.ShapeDtypeStruct((BATCH, WIDTH, SEQ), BF16),
            jax.ShapeDtypeStruct((BATCH, N_HEADS * IDX_DIM, SEQ), F32),
            jax.ShapeDtypeStruct((BATCH, N_HEADS, SEQ), F32),
            jax.ShapeDtypeStruct((N_PROMPT, WIDTH), BF16),
            jax.ShapeDtypeStruct((BATCH, NQB, WIDTH, TQ), BF16),
            jax.ShapeDtypeStruct((DEPTH, N_PROMPT, WIDTH), F32),
            jax.ShapeDtypeStruct((DEPTH, N_PROMPT, WIDTH), F32),
            jax.ShapeDtypeStruct((DEPTH, N_PROMPT, LANES), F32),
        ],
        input_output_aliases=aliases,
        compiler_params=_cparams(("arbitrary",)),
        name="post_a",
    )(*args)


def _sortable_key(score):
    score = jnp.where(score == 0.0, 0.0, score)
    bits = pltpu.bitcast(score, I32)
    return jnp.where(bits < 0, bits ^ 0x7FFFFFFF, bits)


def _attn_kernel(qT_ref, qiT_ref, wT_ref, k_ref, vT_ref, ki_ref, bias_ref, _rows_ref, o_ref,
                 keys_ref, hi_ref, lo_ref, eq_ref, sel_ref, acc_ref, p_ref, cut_ref):
    qb = pl.program_id(1)
    nkb = qb + 1
    t_pos = qb * TQ + lax.broadcasted_iota(I32, (1, TQ), 1)
    s_loc = lax.broadcasted_iota(I32, (TQ, TQ), 0)

    qi = qiT_ref[0].astype(BF16)
    w = wT_ref[0]

    def score_blk(kb, carry):
        r0 = pl.multiple_of(kb * TQ, TQ)
        ki = ki_ref[pl.ds(r0, TQ), :][:, 0:IDX_DIM].astype(BF16)
        acc = jnp.zeros((TQ, TQ), F32)
        for h in range(N_HEADS):
            s = jnp.dot(ki, qi[h * IDX_DIM:(h + 1) * IDX_DIM, :], preferred_element_type=F32)
            acc = acc + jnp.maximum(s, 0.0) * w[h:h + 1, :]
        key = jnp.where(kb * TQ + s_loc <= t_pos, _sortable_key(acc), INT_MIN)
        keys_ref[kb] = key
        hi_ref[kb] = (key >> 16).astype(I16)
        lo_ref[kb] = ((key & 0xFFFF) - HALF16).astype(I16)
        return carry

    lax.fori_loop(0, nkb, score_blk, 0)

    @pl.when(nkb % 2 == 1)
    def _():
        keys_ref[nkb] = jnp.full((TQ, TQ), INT_MIN, I32)
        hi_ref[nkb] = jnp.full((TQ, TQ), -HALF16, I16)
        lo_ref[nkb] = jnp.full((TQ, TQ), -HALF16, I16)

    n_pairs = (nkb + 1) // 2

    def count(indicator):
        def body(i, acc):
            for d in range(2):
                kb = 2 * i + d
                x = indicator(keys_ref[kb], kb * TQ + s_loc)
                acc = acc + jnp.sum(x.reshape(TQ // SUBLANES, SUBLANES, TQ), axis=0)
            return acc
        acc = lax.fori_loop(0, n_pairs, body, jnp.zeros((SUBLANES, TQ), F32))
        return jnp.sum(acc, axis=0, keepdims=True)

    def count16(indicator):
        rows = 2 * SUBLANES
        def body(i, acc):
            parts = []
            for d in range(2):
                x = indicator(2 * i + d).reshape(TQ // rows, rows, TQ)
                parts += [x[r] for r in range(TQ // rows)]
            while len(parts) > 1:
                parts = [parts[n] + parts[n + 1] for n in range(0, len(parts), 2)]
            return acc + parts[0]
        acc = lax.fori_loop(0, n_pairs, body, jnp.zeros((rows, TQ), I16))
        return jnp.sum(acc.astype(I32), axis=0, keepdims=True)

    one16 = jnp.int16(1)
    zero16 = jnp.int16(0)

    def bisect16(count_ge, target):
        cnt = count_ge(jnp.zeros((1, TQ), I16))
        ok = cnt >= target
        v0 = jnp.where(ok, 0, -HALF16).astype(I32)

        def step(i, carry):
            v, c_at = carry
            cand = v + jnp.left_shift(jnp.int32(1), 14 - i)
            cnt = count_ge(cand.astype(I16))
            ok = cnt >= target
            return jnp.where(ok, cand, v), jnp.where(ok, cnt, c_at)

        return lax.fori_loop(0, 15, step, (v0, jnp.where(ok, cnt, -1)))

    k_eff = jnp.minimum(TOPK, t_pos + 1)
    tau_hi, _ = bisect16(
        lambda c: count16(lambda kb: jnp.where(hi_ref[kb] >= c, one16, zero16)), k_eff)
    tau_hi16 = tau_hi.astype(I16)
    n_gt = count16(lambda kb: jnp.where(hi_ref[kb] > tau_hi16, one16, zero16))

    def eq_blk(i, carry):
        for d in range(2):
            kb = 2 * i + d
            eq_ref[kb] = jnp.where(hi_ref[kb] == tau_hi16, one16, zero16)
        return carry

    lax.fori_loop(0, n_pairs, eq_blk, 0)
    n_eq_hi = count16(lambda kb: eq_ref[kb])
    need_lo = k_eff - n_gt
    tau_lo, n_ge_lo = bisect16(
        lambda c: count16(lambda kb: jnp.where(lo_ref[kb] >= c, eq_ref[kb], zero16)), need_lo)
    n_ge_lo = jnp.where(n_ge_lo < 0, n_eq_hi, n_ge_lo)
    tau = tau_hi * (2 * HALF16) + (tau_lo + HALF16)

    cut_ref[...] = jnp.full((1, TQ), BIG_IDX, I32)

    @pl.when(jnp.max((n_gt + n_ge_lo - k_eff).astype(F32)) > 0.0)
    def _():
        need = k_eff.astype(F32) - count(lambda key, s: jnp.where(key > tau, 1.0, 0.0))

        def idx_step(i, c):
            trial = c + jnp.left_shift(jnp.int32(1), 10 - i)
            cnt = count(lambda key, s: jnp.where(key == tau, jnp.where(s < trial, 1.0, 0.0), 0.0))
            return jnp.where(cnt < need, trial, c)

        cut_ref[...] = lax.fori_loop(0, 11, idx_step, jnp.zeros((1, TQ), I32))

    cut = cut_ref[...]

    def sel_blk(kb, carry):
        key = keys_ref[kb]
        s_pos = kb * TQ + s_loc
        sel_ref[kb] = jnp.where(
            key > tau, 0.0,
            jnp.where(key == tau, jnp.where(s_pos <= cut, 0.0, NEG_LARGE), NEG_LARGE))
        return carry

    lax.fori_loop(0, nkb, sel_blk, 0)

    n_sub = TQ // TK_SUB

    def logits(h, kb, u):
        hs = slice(h * HEAD_DIM, (h + 1) * HEAD_DIM)
        r0 = pl.multiple_of(kb * TQ + u * TK_SUB, TK_SUB)
        us = slice(u * TK_SUB, (u + 1) * TK_SUB)
        dc = jnp.minimum(qb - kb, 2)
        s = jnp.dot(k_ref[pl.ds(r0, TK_SUB), hs], qT_ref[0, hs, :], preferred_element_type=F32)
        return s * ATTN_SCALE + bias_ref[dc, h, us, :] + sel_ref[kb, us, :]

    def fold8(x):
        return x.reshape(TK_SUB // SUBLANES, SUBLANES, TQ)

    def max_blk(kb, ms):
        out = []
        for h in range(N_HEADS):
            m8 = ms[h]
            for u in range(n_sub):
                m8 = jnp.maximum(m8, jnp.max(fold8(logits(h, kb, u)), axis=0))
            out.append(m8)
        return tuple(out)

    ms = lax.fori_loop(0, nkb, max_blk,
                       tuple(jnp.full((SUBLANES, TQ), NEG_LARGE, F32) for _ in range(N_HEADS)))
    m = [jnp.max(m8, axis=0, keepdims=True) for m8 in ms]
    acc_ref[...] = jnp.zeros(acc_ref.shape, F32)

    def pv_blk(kb, ls):
        out = []
        for h in range(N_HEADS):
            l8 = ls[h]
            for u in range(n_sub):
                p = jnp.exp(logits(h, kb, u) - m[h])
                l8 = l8 + jnp.sum(fold8(p), axis=0)
                p_ref[h, u * TK_SUB:(u + 1) * TK_SUB, :] = p.astype(BF16)
            out.append(l8)
        for h in range(N_HEADS):
            hs = slice(h * HEAD_DIM, (h + 1) * HEAD_DIM)
            acc_ref[h] += jnp.dot(vT_ref[0, kb, hs, :], p_ref[h], preferred_element_type=F32)
        return tuple(out)

    ls = lax.fori_loop(0, nkb, pv_blk,
                       tuple(jnp.zeros((SUBLANES, TQ), F32) for _ in range(N_HEADS)))
    for h in range(N_HEADS):
        o = acc_ref[h] / jnp.sum(ls[h], axis=0, keepdims=True)
        o_ref[:, h * HEAD_DIM:(h + 1) * HEAD_DIM] = o.T.astype(o_ref.dtype)


def _attn_prompt(qT, qiT, wT, knb, vT, ki_all, bias, layer):
    return pl.pallas_call(
        _attn_kernel,
        grid=(BATCH, NQB),
        in_specs=[
            pl.BlockSpec((1, WIDTH, TQ), lambda b, q: (b, 0, q)),
            pl.BlockSpec((1, N_HEADS * IDX_DIM, TQ), lambda b, q: (b, 0, q)),
            pl.BlockSpec((1, N_HEADS, TQ), lambda b, q: (b, 0, q)),
            pl.BlockSpec((SEQ, WIDTH), lambda b, q: (b, 0)),
            pl.BlockSpec((1, NQB, WIDTH, TQ), lambda b, q: (b, 0, 0, 0)),
            pl.BlockSpec((None, SEQ, LANES), lambda b, q: (layer, b, 0)),
            pl.BlockSpec((3, N_HEADS, TQ, TQ), lambda b, q: (0, 0, 0, 0)),
            pl.BlockSpec(memory_space=pl.ANY),
        ],
        out_specs=pl.BlockSpec((TQ, WIDTH), lambda b, q: (b * NQB + q, 0)),
        out_shape=jax.ShapeDtypeStruct((M_ALL, WIDTH), BF16),
        input_output_aliases={7: 0},
        scratch_shapes=[
            pltpu.VMEM((NQB, TQ, TQ), I32),
            pltpu.VMEM((NQB, TQ, TQ), I16),
            pltpu.VMEM((NQB, TQ, TQ), I16),
            pltpu.VMEM((NQB, TQ, TQ), I16),
            pltpu.VMEM((NQB, TQ, TQ), F32),
            pltpu.VMEM((N_HEADS, HEAD_DIM, TQ), F32),
            pltpu.VMEM((N_HEADS, TQ, TQ), BF16),
            pltpu.VMEM((1, TQ), I32),
        ],
        compiler_params=_cparams(("arbitrary", "arbitrary")),
        name="attn_prompt",
    )(qT, qiT, wT, knb, vT, ki_all, bias, jnp.zeros((M_ALL, WIDTH), BF16))


def _hgrn_kernel(zq_ref, zf_ref, zi_ref, zg_ref, lb_ref, og_ref, _rows_ref, ob_ref, st_ref,
                 ST_ref, b_ref, k_ref, q_ref, i_ref, oi_ref):
    ST_ref[...] = jnp.zeros(ST_ref.shape, F32)
    og = og_ref[...]
    row = lax.broadcasted_iota(I32, (CHUNK, LANES), 0)
    lane = lax.broadcasted_iota(I32, (CHUNK, LANES), 1)
    sub_row = lax.broadcasted_iota(I32, (SUBLANES, LANES), 0)

    def chunk(c, carry):
        r0 = pl.multiple_of(c * CHUNK, CHUNK)
        b_last = []
        for hh in range(HPB):
            hs = slice(hh * LANES, (hh + 1) * LANES)
            lb = lb_ref[hh]
            f = lb + (1.0 - lb) * _sigmoid(zf_ref[pl.ds(r0, CHUNK), hs])
            b = jnp.log(f)
            for d in (1, 2, 4, 8):
                b = b + jnp.where(row % SUB >= d, pltpu.roll(b, d, axis=0), 0.0)
            b_ref[hh] = b
            k_ref[hh] = 1.0 - f
            q_ref[hh] = _silu(zq_ref[pl.ds(r0, CHUNK), hs])
            i_ref[hh] = zi_ref[pl.ds(r0, CHUNK), hs]
            b_last.append(b_ref[hh, pl.ds(SUB - 1, N_SUB, stride=SUB), :])

        def sub_block(j, carry2):
            j0 = pl.multiple_of(j * SUB, SUB)
            for hh in range(HPB):
                hs = slice(hh * LANES, (hh + 1) * LANES)
                b_lo = b_ref[hh, pl.ds(j0, SUBLANES), :]
                b_hi = b_ref[hh, pl.ds(j0 + SUBLANES, SUBLANES), :]
                q_lo = q_ref[hh, pl.ds(j0, SUBLANES), :]
                q_hi = q_ref[hh, pl.ds(j0 + SUBLANES, SUBLANES), :]
                o_lo = jnp.zeros((SUBLANES, LANES), F32)
                o_hi = jnp.zeros((SUBLANES, LANES), F32)
                for s in range(SUB):
                    bs = jnp.broadcast_to(b_ref[hh, pl.ds(j0 + s, 1), :], (SUBLANES, LANES))
                    ks = jnp.broadcast_to(k_ref[hh, pl.ds(j0 + s, 1), :], (SUBLANES, LANES))
                    iv = jnp.broadcast_to(i_ref[hh, pl.ds(j0 + s, 1), :], (SUBLANES, LANES))
                    if s < SUBLANES:
                        ok = sub_row >= s
                        x = jnp.where(ok, q_lo * ks * jnp.exp(jnp.where(ok, b_lo - bs, 0.0)), 0.0)
                        o_lo = o_lo + jnp.sum(x, axis=-1, keepdims=True) * iv
                        x = q_hi * ks * jnp.exp(b_hi - bs)
                        o_hi = o_hi + jnp.sum(x, axis=-1, keepdims=True) * iv
                    else:
                        ok = sub_row >= s - SUBLANES
                        x = jnp.where(ok, q_hi * ks * jnp.exp(jnp.where(ok, b_hi - bs, 0.0)), 0.0)
                        o_hi = o_hi + jnp.sum(x, axis=-1, keepdims=True) * iv
                oi_ref[hh, pl.ds(j0, SUBLANES), :] = o_lo
                oi_ref[hh, pl.ds(j0 + SUBLANES, SUBLANES), :] = o_hi
            return carry2

        lax.fori_loop(0, N_SUB, sub_block, 0)

        for hh in range(HPB):
            hs = slice(hh * LANES, (hh + 1) * LANES)
            b = b_ref[hh]
            bl_b = jnp.concatenate(
                [jnp.broadcast_to(b_last[hh][j:j + 1, :], (SUB, LANES)) for j in range(N_SUB)], axis=0)
            q_dec = (q_ref[hh] * jnp.exp(b)).astype(BF16)
            k_end = (k_ref[hh] * jnp.exp(bl_b - b)).astype(BF16)
            decay = jnp.exp(b_last[hh])
            iT = i_ref[hh].T
            it_stack = jnp.concatenate(
                [jnp.where(lane // SUB == j, iT, 0.0).astype(BF16) for j in range(N_SUB)], axis=0)
            pT = jnp.dot(it_stack, k_end, preferred_element_type=F32)
            r = ST_ref[hh]
            r_list = []
            for j in range(N_SUB):
                r_list.append(r.astype(BF16))
                r = r * decay[j:j + 1, :] + pT[j * LANES:(j + 1) * LANES, :]
            ST_ref[hh] = r
            r_stack = jnp.concatenate(r_list, axis=0)
            oT_all = lax.dot_general(r_stack, q_dec, (((1,), (1,)), ((), ())),
                                     preferred_element_type=F32)
            oT = jnp.zeros((LANES, CHUNK), F32)
            for j in range(N_SUB):
                oT = oT + jnp.where(lane // SUB == j, oT_all[j * LANES:(j + 1) * LANES, :], 0.0)
            o = oT.T + oi_ref[hh]
            zg = zg_ref[pl.ds(r0, CHUNK), hs]
            ob_ref[pl.ds(r0, CHUNK), hs] = (_head_norm(o, og) * _silu(zg)).astype(ob_ref.dtype)
        return carry

    lax.fori_loop(0, SEQ // CHUNK, chunk, 0)
    for hh in range(HPB):
        st_ref[hh] = ST_ref[hh].T


def _hgrn_prompt(z, lb3, og):
    w = HPB * LANES
    zspec = lambda cb: pl.BlockSpec((SEQ, w), lambda b, h: (b, cb * COL // w + h))
    return pl.pallas_call(
        _hgrn_kernel,
        grid=(BATCH, N_HEADS // HPB),
        in_specs=[zspec(CB_HQ), zspec(CB_HF), zspec(CB_HI), zspec(CB_HG),
                  pl.BlockSpec((HPB, 1, LANES), lambda b, h: (h, 0, 0)),
                  pl.BlockSpec((1, LANES), lambda b, h: (0, 0)),
                  pl.BlockSpec(memory_space=pl.ANY)],
        out_specs=[pl.BlockSpec((SEQ, w), lambda b, h: (b, h)),
                   pl.BlockSpec((None, HPB, LANES, LANES), lambda b, h: (b, h, 0, 0))],
        out_shape=[jax.ShapeDtypeStruct((M_ALL, WIDTH), BF16),
                   jax.ShapeDtypeStruct((BATCH, N_HEADS, LANES, LANES), F32)],
        input_output_aliases={6: 0},
        scratch_shapes=[pltpu.VMEM((HPB, LANES, LANES), F32),
                        pltpu.VMEM((HPB, CHUNK, LANES), F32),
                        pltpu.VMEM((HPB, CHUNK, LANES), F32),
                        pltpu.VMEM((HPB, CHUNK, LANES), F32),
                        pltpu.VMEM((HPB, CHUNK, LANES), F32),
                        pltpu.VMEM((HPB, CHUNK, LANES), F32)],
        compiler_params=_cparams(("arbitrary", "arbitrary")),
        name="hgrn_prompt",
    )(z, z, z, z, lb3, og, jnp.zeros((M_ALL, WIDTH), BF16))


def _rows_to_cols(x):
    pad = jnp.zeros((LANES - x.shape[0], LANES), F32)
    return jnp.concatenate([x, pad], axis=0).T


def _hgrn_sample_kernel(zq_ref, zf_ref, zi_ref, zg_ref, lb_ref, og_ref, s0_ref, ob_any_ref,
                        ob_ref, s1_ref):
    del ob_any_ref
    lb = lb_ref[...]
    f = lb + (1.0 - lb) * _sigmoid(zf_ref[...])
    fT = _rows_to_cols(f)
    kT = _rows_to_cols(1.0 - f)
    qT = _rows_to_cols(_silu(zq_ref[...]))
    iv = zi_ref[...]
    rows = []
    for b in range(DEC_BATCH):
        s1 = fT[:, b:b + 1] * s0_ref[b] + kT[:, b:b + 1] * iv[b:b + 1, :]
        s1_ref[b] = s1
        rows.append(jnp.sum(qT[:, b:b + 1] * s1, axis=0, keepdims=True))
    o = jnp.concatenate(rows, axis=0)
    o = _head_norm(o, og_ref[...]) * _silu(zg_ref[...])
    pad = jnp.zeros((LANES - DEC_BATCH, LANES), F32)
    ob_ref[...] = jnp.concatenate([o, pad], axis=0).astype(ob_ref.dtype)


def _hgrn_sample(z, lb3, og, state_hgrn, ob_all, layer):
    cpb = COL // LANES
    zspec = lambda cb: pl.BlockSpec((DEC_BATCH, LANES), lambda h: (SAMPLE_ROW_BLOCK, cb * cpb + h))
    return pl.pallas_call(
        _hgrn_sample_kernel,
        grid=(N_HEADS,),
        in_specs=[zspec(CB_HQ), zspec(CB_HF), zspec(CB_HI), zspec(CB_HG),
                  pl.BlockSpec((None, 1, LANES), lambda h: (h, 0, 0)),
                  pl.BlockSpec((1, LANES), lambda h: (0, 0)),
                  pl.BlockSpec((None, DEC_BATCH, None, LANES, LANES), lambda h: (layer, 0, h, 0, 0)),
                  pl.BlockSpec(memory_space=pl.ANY)],
        out_specs=[pl.BlockSpec((LANES, LANES), lambda h: (SAMPLE_TILE, h)),
                   pl.BlockSpec((DEC_BATCH, None, LANES, LANES), lambda h: (0, h, 0, 0))],
        out_shape=[jax.ShapeDtypeStruct((M_ALL, WIDTH), BF16),
                   jax.ShapeDtypeStruct((DEC_BATCH, N_HEADS, LANES, LANES), F32)],
        input_output_aliases={7: 0},
        compiler_params=_cparams(("arbitrary",)),
        name="hgrn_sample",
    )(z, z, z, z, lb3, og, state_hgrn, ob_all)


def _sample_norm_kernel(zq_ref, zk_ref, zkiw_ref, gq_ref, gk_ref, gki_ref, qn_ref, kn_ref, kiw_ref):
    gq = gq_ref[...]
    gk = gk_ref[...]
    for h in range(N_HEADS):
        hs = slice(h * HEAD_DIM, (h + 1) * HEAD_DIM)
        qn_ref[:, hs] = _head_norm(zq_ref[:, hs], gq)
        kn_ref[:, hs] = _head_norm(zk_ref[:, hs], gk)
    kiw = zkiw_ref[:, 0:LANES]
    lane = lax.broadcasted_iota(I32, kiw.shape, 1)
    kiw_ref[...] = _kidx_norm(kiw, gki_ref[...]) + jnp.where(
        (lane >= IDX_DIM) & (lane < IDX_DIM + N_HEADS), kiw * IDX_WEIGHT_SCALE, 0.0)


def _sample_norms(z, gq, gk, gki_pad):
    zspec = lambda cb, w: pl.BlockSpec((DEC_BATCH, w), lambda i: (SAMPLE_ROW_BLOCK, cb * COL // w))
    gspec = pl.BlockSpec((1, LANES), lambda i: (0, 0))
    return pl.pallas_call(
        _sample_norm_kernel,
        grid=(1,),
        in_specs=[zspec(CB_Q, WIDTH), zspec(CB_K, WIDTH), zspec(CB_KIW, COL), gspec, gspec, gspec],
        out_specs=[pl.BlockSpec((DEC_BATCH, WIDTH), lambda i: (0, 0)),
                   pl.BlockSpec((DEC_BATCH, WIDTH), lambda i: (0, 0)),
                   pl.BlockSpec((DEC_BATCH, LANES), lambda i: (0, 0))],
        out_shape=[jax.ShapeDtypeStruct((DEC_BATCH, WIDTH), F32),
                   jax.ShapeDtypeStruct((DEC_BATCH, WIDTH), F32),
                   jax.ShapeDtypeStruct((DEC_BATCH, LANES), F32)],
        compiler_params=_cparams(("arbitrary",)),
        name="sample_norms",
    )(z, z, z, gq, gk, gki_pad)


SCORE_W = PAST_LEN + LANES
N_SCORE_TILES = SCORE_W // LANES


def _select_row(x, b):
    sub = lax.broadcasted_iota(I32, x.shape, 0)
    return jnp.sum(jnp.where(sub == b, x, 0.0), axis=0, keepdims=True)


def _sample_topk_kernel(pt_ref, zqi_ref, kiw_ref, cache_ref, sel_ref, buf_ref, sem_ref, key_ref,
                        pos_ref, cut_ref, *, layer):
    b = pl.program_id(0)

    def page_copy(p):
        return pltpu.make_async_copy(cache_ref.at[layer, pt_ref[b, p]], buf_ref.at[p], sem_ref.at[0])

    def start(p, c):
        page_copy(p).start()
        return c

    lax.fori_loop(0, N_PAGES, start, 0)

    qrow = _select_row(zqi_ref[...], b)
    head = lax.broadcasted_iota(I32, (N_HEADS, N_HEADS * IDX_DIM), 0)
    col = lax.broadcasted_iota(I32, (N_HEADS, N_HEADS * IDX_DIM), 1)
    qm = jnp.where(col // IDX_DIM == head, jnp.broadcast_to(qrow, head.shape), 0.0)
    q128 = qm[:, 0:LANES]
    for p in range(1, N_HEADS * IDX_DIM // LANES):
        q128 = q128 + qm[:, p * LANES:(p + 1) * LANES]
    q8 = (q128[:, 0:IDX_DIM] + q128[:, IDX_DIM:]).astype(BF16)

    kiw_row = _select_row(kiw_ref[...], b)
    r8 = lax.broadcasted_iota(I32, (N_HEADS, LANES), 0)
    l8 = lax.broadcasted_iota(I32, (N_HEADS, LANES), 1)
    wcol = jnp.sum(jnp.where(l8 == IDX_DIM + r8, jnp.broadcast_to(kiw_row, (N_HEADS, LANES)), 0.0),
                   axis=-1, keepdims=True)

    def wait(p, c):
        page_copy(p).wait()
        return c

    lax.fori_loop(0, N_PAGES, wait, 0)

    nt = (((1,), (1,)), ((), ()))
    keys = buf_ref[...].reshape(PAST_LEN, IDX_DIM).astype(BF16)
    s = lax.dot_general(q8, keys, nt, preferred_element_type=F32)
    s_past = jnp.sum(jnp.maximum(s, 0.0) * wcol, axis=0, keepdims=True)
    knew8 = jnp.broadcast_to(kiw_row[:, 0:IDX_DIM], (SUBLANES, IDX_DIM)).astype(BF16)
    sn = lax.dot_general(q8, knew8, nt, preferred_element_type=F32)
    s_new = jnp.sum(jnp.maximum(sn[:, 0:1], 0.0) * wcol, axis=0, keepdims=True)
    lane1 = lax.broadcasted_iota(I32, (1, LANES), 1)
    tail = jnp.where(lane1 == 0, jnp.broadcast_to(s_new, (1, LANES)), 0.0)
    srow = jnp.concatenate([s_past, tail], axis=1)
    slot = lax.broadcasted_iota(I32, (1, SCORE_W), 1)
    krow = jnp.where(slot <= PAST_LEN, _sortable_key(srow), INT_MIN)

    @pl.when(b == 0)
    def _():
        key_ref[...] = jnp.full(key_ref.shape, INT_MIN, I32)

    sub = lax.broadcasted_iota(I32, (DEC_BATCH, LANES), 0)
    for c in range(N_SCORE_TILES):
        tile = jnp.broadcast_to(krow[:, c * LANES:(c + 1) * LANES], (DEC_BATCH, LANES))
        key_ref[c] = jnp.where(sub == b, tile, key_ref[c])

    @pl.when(b == DEC_BATCH - 1)
    def _():
        shape = (N_SCORE_TILES, DEC_BATCH, LANES)
        pos = lax.broadcasted_iota(I32, shape, 0) * LANES + lax.broadcasted_iota(I32, shape, 2)

        def count(indicator):
            x = indicator(key_ref[...])
            return jnp.sum(jnp.sum(x, axis=0), axis=1, keepdims=True)

        cnt = count(lambda key: jnp.where(key >= 0, 1.0, 0.0))
        tau0 = jnp.where(cnt >= TOPK, 0, INT_MIN).astype(I32)

        def bit_step(i, tau):
            cand = tau + jnp.left_shift(jnp.int32(1), 30 - i)
            cnt = count(lambda key: jnp.where(key >= cand[None], 1.0, 0.0))
            return jnp.where(cnt >= TOPK, cand, tau)

        tau = lax.fori_loop(0, 31, bit_step, tau0)[None]

        need = TOPK - count(lambda key: jnp.where(key > tau, 1.0, 0.0))
        n_eq = count(lambda key: jnp.where(key == tau, 1.0, 0.0))
        cut_ref[...] = jnp.full(cut_ref.shape, BIG_IDX, I32)

        @pl.when(jnp.max(n_eq - need) > 0.0)
        def _():
            def idx_step(i, c):
                trial = c + jnp.left_shift(jnp.int32(1), 14 - i)
                cnt = count(lambda key: jnp.where(key == tau, jnp.where(pos < trial[None], 1.0, 0.0), 0.0))
                return jnp.where(cnt < need, trial, c)
            c = lax.fori_loop(0, 15, idx_step, jnp.zeros((DEC_BATCH, 1), I32))
            cut_ref[...] = jnp.broadcast_to(c, cut_ref.shape)

        cut = cut_ref[:, 0:1][None]
        key = key_ref[...]
        chosen = jnp.where(key > tau, pos,
                           jnp.where(key == tau, jnp.where(pos <= cut, pos, BIG_IDX), BIG_IDX))
        pos_ref[...] = chosen.astype(F32)

        out_lane = lax.broadcasted_iota(I32, (DEC_BATCH, TOPK), 1)

        def pick(it, carry):
            sel, last = carry
            p = pos_ref[...]
            first = jnp.min(jnp.min(jnp.where(p > last[None], p, float(BIG_IDX)), axis=0),
                            axis=1, keepdims=True)
            return jnp.where(out_lane == it, first.astype(I32), sel), first

        sel, _ = lax.fori_loop(0, TOPK, pick, (jnp.zeros((DEC_BATCH, TOPK), I32),
                                               jnp.full((DEC_BATCH, 1), -1.0, F32)))
        sel_ref[...] = sel


def _sample_topk(page_table, z, kiw_s, cache_kidx, layer):
    grid_spec = pltpu.PrefetchScalarGridSpec(
        num_scalar_prefetch=1,
        grid=(DEC_BATCH,),
        in_specs=[pl.BlockSpec((DEC_BATCH, COL), lambda b, pt: (SAMPLE_ROW_BLOCK, CB_QI)),
                  pl.BlockSpec((DEC_BATCH, LANES), lambda b, pt: (0, 0)),
                  pl.BlockSpec(memory_space=pl.ANY)],
        out_specs=pl.BlockSpec((DEC_BATCH, TOPK), lambda b, pt: (0, 0)),
        scratch_shapes=[pltpu.VMEM((N_PAGES, PAGE_SIZE, IDX_DIM), F32),
                        pltpu.SemaphoreType.DMA((1,)),
                        pltpu.VMEM((N_SCORE_TILES, DEC_BATCH, LANES), I32),
                        pltpu.VMEM((N_SCORE_TILES, DEC_BATCH, LANES), F32),
                        pltpu.VMEM((DEC_BATCH, LANES), I32)],
    )
    return pl.pallas_call(
        functools.partial(_sample_topk_kernel, layer=layer),
        grid_spec=grid_spec,
        out_shape=jax.ShapeDtypeStruct((DEC_BATCH, TOPK), I32),
        compiler_params=_cparams(("arbitrary",)),
        name="sample_topk",
    )(page_table, z, kiw_s, cache_kidx)


def _sample_attn_kernel(sel_s_ref, pt_ref, sel_ref, qn_ref, kn_ref, zv_ref, rbT_ref,
                        ck_ref, cv_ref, oa_any_ref, o_ref, kbuf_ref, vbuf_ref, sem_ref, rows_ref,
                        *, layer):
    del oa_any_ref
    b = pl.program_id(0)

    def copies(j):
        sp = jnp.minimum(sel_s_ref[b, j], PAST_LEN - 1)
        page = pt_ref[b, sp // PAGE_SIZE]
        off = sp % PAGE_SIZE
        dst = pl.ds(pl.multiple_of(j * N_HEADS, N_HEADS), N_HEADS)
        return (pltpu.make_async_copy(ck_ref.at[layer, page, off], kbuf_ref.at[dst], sem_ref.at[0]),
                pltpu.make_async_copy(cv_ref.at[layer, page, off], vbuf_ref.at[dst], sem_ref.at[1]))

    def start(j, c):
        ck, cv = copies(j)
        ck.start()
        cv.start()
        return c

    lax.fori_loop(0, TOPK, start, 0)

    def heads_on_rows(x):
        row = _select_row(x, b)
        return jnp.concatenate([row[:, h * HEAD_DIM:(h + 1) * HEAD_DIM] for h in range(N_HEADS)],
                               axis=0)

    q = heads_on_rows(qn_ref[...])
    k_new = heads_on_rows(kn_ref[...])
    v_new = heads_on_rows(zv_ref[...])
    sel_b = jnp.sum(jnp.where(lax.broadcasted_iota(I32, (DEC_BATCH, TOPK), 0) == b, sel_ref[...], 0),
                    axis=0, keepdims=True)
    is_new = sel_b >= PAST_LEN
    bucket = _t5_bucket(jnp.maximum(PAST_LEN - sel_b, 0))
    bias = jnp.zeros((N_HEADS, TOPK), F32)
    for beta in range(NUM_BUCKETS):
        bias = jnp.where(bucket == beta, rbT_ref[:, beta:beta + 1], bias)

    def wait(j, c):
        ck, cv = copies(j)
        ck.wait()
        cv.wait()
        return c

    lax.fori_loop(0, TOPK, wait, 0)

    nt = (((1,), (1,)), ((), ()))
    hrow = lax.broadcasted_iota(I32, (N_HEADS, TOPK), 0)
    qb16 = q.astype(BF16)
    logits = jnp.zeros((N_HEADS, TOPK), F32)
    for h in range(N_HEADS):
        kh = kbuf_ref[pl.ds(h, TOPK, stride=N_HEADS), :].astype(BF16)
        res = lax.dot_general(qb16, kh, nt, preferred_element_type=F32)
        logits = jnp.where(hrow == h, res, logits)
    logit_new = jnp.sum(qb16.astype(F32) * k_new.astype(BF16).astype(F32), axis=-1, keepdims=True)
    logits = jnp.where(is_new, logit_new, logits) * ATTN_SCALE + bias
    m = jnp.max(logits, axis=1, keepdims=True)
    e = jnp.exp(logits - m)
    p = e / jnp.sum(e, axis=1, keepdims=True)
    p_new = jnp.sum(jnp.where(is_new, p, 0.0), axis=1, keepdims=True)
    p_mm = jnp.where(is_new, 0.0, p).astype(BF16)
    hrow_o = lax.broadcasted_iota(I32, (N_HEADS, HEAD_DIM), 0)
    o = jnp.zeros((N_HEADS, HEAD_DIM), F32)
    for h in range(N_HEADS):
        vh = vbuf_ref[pl.ds(h, TOPK, stride=N_HEADS), :].astype(BF16)
        res = jnp.dot(p_mm, vh, preferred_element_type=F32)
        o = jnp.where(hrow_o == h, res, o)
    o = o + p_new.astype(BF16).astype(F32) * v_new.astype(BF16).astype(F32)
    orow = jnp.concatenate([o[h:h + 1, :] for h in range(N_HEADS)], axis=1)

    @pl.when(b == 0)
    def _():
        rows_ref[...] = jnp.zeros(rows_ref.shape, F32)

    sub = lax.broadcasted_iota(I32, (DEC_BATCH, WIDTH), 0)
    rows_ref[...] = jnp.where(sub == b, jnp.broadcast_to(orow, (DEC_BATCH, WIDTH)), rows_ref[...])

    @pl.when(b == DEC_BATCH - 1)
    def _():
        pad = jnp.zeros((LANES - DEC_BATCH, WIDTH), F32)
        o_ref[...] = jnp.concatenate([rows_ref[...], pad], axis=0).astype(o_ref.dtype)


def _sample_attn(sel, page_table, qn_s, kn_s, z, rbT, cache_k, cache_v, oa_all, layer):
    grid_spec = pltpu.PrefetchScalarGridSpec(
        num_scalar_prefetch=2,
        grid=(DEC_BATCH,),
        in_specs=[pl.BlockSpec((DEC_BATCH, TOPK), lambda b, s, pt: (0, 0)),
                  pl.BlockSpec((DEC_BATCH, WIDTH), lambda b, s, pt: (0, 0)),
                  pl.BlockSpec((DEC_BATCH, WIDTH), lambda b, s, pt: (0, 0)),
                  pl.BlockSpec((DEC_BATCH, WIDTH), lambda b, s, pt: (SAMPLE_ROW_BLOCK, CB_V * COL // WIDTH)),
                  pl.BlockSpec((N_HEADS, NUM_BUCKETS), lambda b, s, pt: (0, 0)),
                  pl.BlockSpec(memory_space=pl.ANY),
                  pl.BlockSpec(memory_space=pl.ANY),
                  pl.BlockSpec(memory_space=pl.ANY)],
        out_specs=pl.BlockSpec((LANES, WIDTH), lambda b, s, pt: (SAMPLE_TILE, 0)),
        scratch_shapes=[pltpu.VMEM((TOPK * N_HEADS, HEAD_DIM), F32),
                        pltpu.VMEM((TOPK * N_HEADS, HEAD_DIM), F32),
                        pltpu.SemaphoreType.DMA((2,)),
                        pltpu.VMEM((DEC_BATCH, WIDTH), F32)],
    )
    return pl.pallas_call(
        functools.partial(_sample_attn_kernel, layer=layer),
        grid_spec=grid_spec,
        out_shape=jax.ShapeDtypeStruct((M_ALL, WIDTH), BF16),
        input_output_aliases={9: 0},
        compiler_params=_cparams(("arbitrary",)),
        name="sample_attn",
    )(sel, page_table, sel, qn_s, kn_s, z, rbT, cache_k, cache_v, oa_all)


def _permute_w_in(w_in):
    parts = []
    off = 0
    for w in IN_WIDTHS:
        parts.append(w_in[..., off:off + w])
        off += w
    q, k, v, qi, ki, wi, hq, hf, hi, hg, ga, gb = parts
    pad = jnp.zeros(w_in.shape[:-1] + (COL - IDX_DIM - N_HEADS,), w_in.dtype)
    return jnp.concatenate([q, k, v, qi, ki, wi, pad, hq, hf, hi, hg, ga, gb], axis=-1).astype(BF16)


def _stack_rows(prompt_rows, sample_rows):
    pad = jnp.zeros((M_ALL - N_PROMPT - DEC_BATCH, prompt_rows.shape[1]), prompt_rows.dtype)
    return jnp.concatenate([prompt_rows, sample_rows.astype(prompt_rows.dtype), pad], axis=0)


def kernel(x_prompt, x_sample, cache_k, cache_v, cache_kidx, state_hgrn, page_table, norm_mix_g, w_in,
           q_norm_g, k_norm_g, kidx_norm_g, rel_bias, hgrn_lb, hgrn_out_g, w_up_a, w_up_b, w_out,
           norm_ffn_g, w_ffn_gate, w_ffn_up, w_ffn_down):
    w_in_p = _permute_w_in(w_in)
    w_up_a, w_up_b, w_out = (w.astype(BF16) for w in (w_up_a, w_up_b, w_out))
    w_ffn_gate, w_ffn_up, w_ffn_down = (w.astype(BF16) for w in (w_ffn_gate, w_ffn_up, w_ffn_down))
    lower = _lower_bounds(hgrn_lb)
    bias = _bias_tiles(rel_bias)
    rbT = rel_bias.T
    gki_pad = jnp.pad(kidx_norm_g, ((0, 0), (0, LANES - IDX_DIM)))

    x = _stack_rows(x_prompt.reshape(N_PROMPT, D_MODEL), x_sample.reshape(DEC_BATCH, D_MODEL))
    stacked = (jnp.zeros((DEPTH, N_PROMPT, WIDTH), F32), jnp.zeros((DEPTH, N_PROMPT, WIDTH), F32),
               jnp.zeros((DEPTH, N_PROMPT, LANES), F32))
    outs = [[] for _ in range(5)]
    for l in range(DEPTH):
        gq = q_norm_g[l].reshape(1, HEAD_DIM)
        gk = k_norm_g[l].reshape(1, HEAD_DIM)
        gki = gki_pad[l].reshape(1, LANES)
        og = hgrn_out_g[l].reshape(1, LANES)
        lb3 = lower[l].reshape(N_HEADS, 1, LANES)

        xn = _rmsnorm(x, norm_mix_g[l])
        z = _in_proj(xn, w_in_p, l)

        qT, qiT, wT, knb, vT, *stacked = _post(z, gq, gk, gki, l, stacked)
        oa = _attn_prompt(qT, qiT, wT, knb, vT, stacked[2], bias, l)
        qn_s, kn_s, kiw_s = _sample_norms(z, gq, gk, gki)
        sel = _sample_topk(page_table, z, kiw_s, cache_kidx, l)
        oa = _sample_attn(sel, page_table, qn_s, kn_s, z, rbT, cache_k, cache_v, oa, l)

        ob, st_p = _hgrn_prompt(z, lb3, og)
        ob, st_s = _hgrn_sample(z, lb3, og, state_hgrn, ob, l)

        mixed = _merge(oa, ob, w_up_a, w_up_b, z, l)
        x = _residual_proj(mixed, w_out, x, l, TM, "out_proj")
        hn = _rmsnorm(x, norm_ffn_g[l])
        x = _residual_proj(_ffn_up(hn, w_ffn_gate, w_ffn_up, l), w_ffn_down, x, l, TM_DOWN, "ffn_down")

        v_cols = slice(CB_V * COL, CB_V * COL + WIDTH)
        outs[0].append(st_p)
        outs[1].append(kn_s.reshape(DEC_BATCH, 1, N_HEADS, HEAD_DIM))
        outs[2].append(z[N_PROMPT:N_PROMPT + DEC_BATCH, v_cols].reshape(DEC_BATCH, 1, N_HEADS, HEAD_DIM))
        outs[3].append(kiw_s[:, :IDX_DIM].reshape(DEC_BATCH, 1, IDX_DIM))
        outs[4].append(st_s)

    k_all, v_all, ki_all = stacked
    y_prompt = x[:N_PROMPT].reshape(BATCH, SEQ, D_MODEL)
    y_sample = x[N_PROMPT:N_PROMPT + DEC_BATCH].reshape(DEC_BATCH, 1, D_MODEL)
    st_p, k_s, v_s, ki_s, st_s = (jnp.stack(o) for o in outs)
    return (y_prompt, y_sample,
            k_all.reshape(DEPTH, BATCH, SEQ, N_HEADS, HEAD_DIM),
            v_all.reshape(DEPTH, BATCH, SEQ, N_HEADS, HEAD_DIM),
            ki_all[:, :, :IDX_DIM].reshape(DEPTH, BATCH, SEQ, IDX_DIM),
            st_p, k_s, v_s, ki_s, st_s)
```

```python
import functools
import math

import jax
import jax.numpy as jnp
from jax import lax
from jax.experimental import pallas as pl
from jax.experimental.pallas import tpu as pltpu

F32 = jnp.float32
BF16 = jnp.bfloat16
I32 = jnp.int32
I16 = jnp.int16

D_MODEL = 2048
BATCH = 4
SEQ = 2048
DEPTH = 4
DEC_BATCH = 8
PAST_LEN = 16384
PAGE_SIZE = 128
N_PAGES = PAST_LEN // PAGE_SIZE
N_HEADS = 8
HEAD_DIM = 128
WIDTH = N_HEADS * HEAD_DIM
IDX_DIM = 64
TOPK = 256
ATTN_SCALE = HEAD_DIM ** -0.5
IDX_WEIGHT_SCALE = (N_HEADS ** -0.5) * (IDX_DIM ** -0.5)
NEG_LARGE = -1e30
NUM_BUCKETS = 32
MAX_DISTANCE = 128
D_FF = ((8 * D_MODEL // 3 + 255) // 256) * 256
EPS = 1e-6
IN_WIDTHS = (WIDTH, WIDTH, WIDTH, N_HEADS * IDX_DIM, IDX_DIM, N_HEADS,
             WIDTH, WIDTH, WIDTH, WIDTH, D_MODEL, D_MODEL)

SUBLANES = 8
LANES = 128
VMEM_LIMIT = 56 * 1024 * 1024

N_PROMPT = BATCH * SEQ
M_ALL = N_PROMPT + LANES
SAMPLE_ROW_BLOCK = N_PROMPT // SUBLANES
SAMPLE_TILE = N_PROMPT // LANES

COL = 512
NP_IN = 24 * COL
CB_Q, CB_K, CB_V, CB_QI, CB_KIW = 0, 2, 4, 6, 7
CB_HQ, CB_HF, CB_HI, CB_HG, CB_GA, CB_GB = 8, 10, 12, 14, 16, 20

TM = M_ALL // 5
TM_DOWN = M_ALL // 10
TM_NORM = M_ALL // 20
TN = 512

TQ = 256
NQB = SEQ // TQ
TK_SUB = 128
INT_MIN = -2 ** 31
HALF16 = 2 ** 15
BIG_IDX = 2 ** 30

SUB = 16
CHUNK = 128
N_SUB = CHUNK // SUB
HPB = 4


def _cparams(sem):
    return pltpu.CompilerParams(dimension_semantics=sem, vmem_limit_bytes=VMEM_LIMIT)


def _sigmoid(x):
    return 1.0 / (1.0 + jnp.exp(-x))


def _silu(x):
    return x * _sigmoid(x)


def _lb_kernel(lb_ref, o_ref):
    x = lb_ref[...]
    m = jnp.max(x, axis=0, keepdims=True)
    e = jnp.exp(x - m)
    p = e / jnp.sum(e, axis=0, keepdims=True)
    acc = jnp.zeros_like(p[0:1])
    rows = []
    for l in range(DEPTH):
        acc = acc + p[l:l + 1]
        rows.append(acc - p[0:1])
    o_ref[...] = jnp.concatenate(rows, axis=0)


def _lower_bounds(hgrn_lb):
    return pl.pallas_call(
        _lb_kernel,
        out_shape=jax.ShapeDtypeStruct((DEPTH, WIDTH), F32),
        name="hgrn_lower_bounds",
    )(hgrn_lb)


def _t5_bucket(n):
    max_exact = NUM_BUCKETS // 2
    nf = jnp.maximum(n, 1).astype(F32)
    large = max_exact + (jnp.log(nf / max_exact) / math.log(MAX_DISTANCE / max_exact)
                         * (NUM_BUCKETS - max_exact)).astype(I32)
    large = jnp.minimum(large, NUM_BUCKETS - 1)
    return jnp.where(n < max_exact, n, large)


def _bias_kernel(rb_ref, o_ref):
    dc = pl.program_id(0)
    s = lax.broadcasted_iota(I32, (TQ, TQ), 0)
    t = lax.broadcasted_iota(I32, (TQ, TQ), 1)
    bucket = _t5_bucket(jnp.maximum(dc * TQ + t - s, 0))
    for h in range(N_HEADS):
        acc = jnp.zeros((TQ, TQ), F32)
        for beta in range(NUM_BUCKETS):
            acc = jnp.where(bucket == beta, rb_ref[beta, h], acc)
        o_ref[0, h] = acc


def _bias_tiles(rel_bias):
    return pl.pallas_call(
        _bias_kernel,
        grid=(3,),
        in_specs=[pl.BlockSpec(memory_space=pltpu.SMEM)],
        out_specs=pl.BlockSpec((1, N_HEADS, TQ, TQ), lambda d: (d, 0, 0, 0)),
        out_shape=jax.ShapeDtypeStruct((3, N_HEADS, TQ, TQ), F32),
        compiler_params=_cparams(("arbitrary",)),
        name="bias_tiles",
    )(rel_bias)


def _rmsnorm_kernel(x_ref, g_ref, o_ref):
    x = x_ref[...]
    ms = jnp.mean(x * x, axis=-1, keepdims=True)
    o_ref[...] = (x * lax.rsqrt(ms + EPS) * g_ref[...]).astype(o_ref.dtype)


def _rmsnorm(x, g):
    m, d = x.shape
    return pl.pallas_call(
        _rmsnorm_kernel,
        grid=(m // TM_NORM,),
        in_specs=[pl.BlockSpec((TM_NORM, d), lambda i: (i, 0)),
                  pl.BlockSpec((1, d), lambda i: (0, 0))],
        out_specs=pl.BlockSpec((TM_NORM, d), lambda i: (i, 0)),
        out_shape=jax.ShapeDtypeStruct((m, d), BF16),
        compiler_params=_cparams(("arbitrary",)),
        name="rmsnorm",
    )(x, g.reshape(1, d))


def _wspec(k, layer):
    return pl.BlockSpec((None, k, TN), lambda i, j: (layer, 0, j))


def _in_proj_kernel(a_ref, wt_ref, o_ref):
    o_ref[...] = lax.dot_general(a_ref[...], wt_ref[...], (((1,), (1,)), ((), ())),
                                 preferred_element_type=F32)


def _in_proj(xn, w_in_p, layer):
    return pl.pallas_call(
        _in_proj_kernel,
        grid=(M_ALL // TM, NP_IN // TN),
        in_specs=[pl.BlockSpec((TM, D_MODEL), lambda i, j: (i, 0)),
                  pl.BlockSpec((None, TN, D_MODEL), lambda i, j: (layer, j, 0))],
        out_specs=pl.BlockSpec((TM, TN), lambda i, j: (i, j)),
        out_shape=jax.ShapeDtypeStruct((M_ALL, NP_IN), F32),
        compiler_params=_cparams(("arbitrary", "arbitrary")),
        name="in_proj",
    )(xn, w_in_p)


def _merge_kernel(oa_ref, ob_ref, wa_ref, wb_ref, ga_ref, gb_ref, o_ref):
    a = jnp.dot(oa_ref[...], wa_ref[...], preferred_element_type=F32)
    b = jnp.dot(ob_ref[...], wb_ref[...], preferred_element_type=F32)
    o_ref[...] = (_sigmoid(ga_ref[...]) * a + _sigmoid(gb_ref[...]) * b).astype(o_ref.dtype)


def _merge(oa, ob, wa, wb, z, layer):
    return pl.pallas_call(
        _merge_kernel,
        grid=(M_ALL // TM, D_MODEL // TN),
        in_specs=[pl.BlockSpec((TM, WIDTH), lambda i, j: (i, 0)),
                  pl.BlockSpec((TM, WIDTH), lambda i, j: (i, 0)),
                  _wspec(WIDTH, layer), _wspec(WIDTH, layer),
                  pl.BlockSpec((TM, TN), lambda i, j: (i, CB_GA + j)),
                  pl.BlockSpec((TM, TN), lambda i, j: (i, CB_GB + j))],
        out_specs=pl.BlockSpec((TM, TN), lambda i, j: (i, j)),
        out_shape=jax.ShapeDtypeStruct((M_ALL, D_MODEL), BF16),
        compiler_params=_cparams(("arbitrary", "arbitrary")),
        name="merge",
    )(oa, ob, wa, wb, z, z)


def _residual_proj_kernel(a_ref, w_ref, x_ref, o_ref):
    o_ref[...] = x_ref[...] + jnp.dot(a_ref[...], w_ref[...], preferred_element_type=F32)


def _residual_proj(a, w, x, layer, tm, name):
    k = a.shape[1]
    return pl.pallas_call(
        _residual_proj_kernel,
        grid=(M_ALL // tm, D_MODEL // TN),
        in_specs=[pl.BlockSpec((tm, k), lambda i, j: (i, 0)),
                  _wspec(k, layer),
                  pl.BlockSpec((tm, TN), lambda i, j: (i, j))],
        out_specs=pl.BlockSpec((tm, TN), lambda i, j: (i, j)),
        out_shape=jax.ShapeDtypeStruct((M_ALL, D_MODEL), F32),
        compiler_params=_cparams(("arbitrary", "arbitrary")),
        name=name,
    )(a, w, x)


def _ffn_up_kernel(a_ref, wg_ref, wu_ref, o_ref):
    a = a_ref[...]
    g = jnp.dot(a, wg_ref[...], preferred_element_type=F32)
    u = jnp.dot(a, wu_ref[...], preferred_element_type=F32)
    o_ref[...] = (_silu(g) * u).astype(o_ref.dtype)


def _ffn_up(hn, wg, wu, layer):
    return pl.pallas_call(
        _ffn_up_kernel,
        grid=(M_ALL // TM, D_FF // TN),
        in_specs=[pl.BlockSpec((TM, D_MODEL), lambda i, j: (i, 0)),
                  _wspec(D_MODEL, layer), _wspec(D_MODEL, layer)],
        out_specs=pl.BlockSpec((TM, TN), lambda i, j: (i, j)),
        out_shape=jax.ShapeDtypeStruct((M_ALL, D_FF), BF16),
        compiler_params=_cparams(("arbitrary", "arbitrary")),
        name="ffn_up",
    )(hn, wg, wu)


def _head_norm(x, g):
    ms = jnp.mean(x * x, axis=-1, keepdims=True)
    return x * lax.rsqrt(ms + EPS) * g


def _kidx_norm(kiw, g_pad):
    lane = lax.broadcasted_iota(I32, kiw.shape, 1)
    ki = jnp.where(lane < IDX_DIM, kiw, 0.0)
    ms = jnp.sum(ki * ki, axis=-1, keepdims=True) * (1.0 / IDX_DIM)
    return ki * lax.rsqrt(ms + EPS) * g_pad


def _post_kernel(zq_ref, zk_ref, zv_ref, zqi_ref, zkiw_ref, gq_ref, gk_ref, gki_ref, *rest):
    qT_ref, qiT_ref, wT_ref, knb_ref, vT_ref, kn_ref, v_ref, ki_ref = rest[-8:]
    gq = gq_ref[...]
    gk = gk_ref[...]
    for h in range(N_HEADS):
        hs = slice(h * HEAD_DIM, (h + 1) * HEAD_DIM)
        qn = _head_norm(zq_ref[:, hs], gq)
        qT_ref[0, hs, :] = qn.T.astype(BF16)
        kn = _head_norm(zk_ref[:, hs], gk)
        kn_ref[:, hs] = kn
        knb_ref[:, hs] = kn.astype(BF16)
        v = zv_ref[:, hs]
        v_ref[:, hs] = v
        vT_ref[0, 0, hs, :] = v.T.astype(BF16)
    for p in range(N_HEADS * IDX_DIM // LANES):
        ps = slice(p * LANES, (p + 1) * LANES)
        qiT_ref[0, ps, :] = zqi_ref[:, ps].T
    kiw = zkiw_ref[:, 0:LANES]
    ki_ref[...] = _kidx_norm(kiw, gki_ref[...])
    wT_ref[0] = (kiw * IDX_WEIGHT_SCALE).T[IDX_DIM:IDX_DIM + N_HEADS, :]


def _post(z, gq, gk, gki_pad, layer, stacked):
    nb = SEQ // LANES
    zspec = lambda cb, w: pl.BlockSpec((LANES, w), lambda r: (r, cb * COL // w))
    gspec = pl.BlockSpec((1, LANES), lambda r: (0, 0))
    in_specs = [zspec(CB_Q, WIDTH), zspec(CB_K, WIDTH), zspec(CB_V, WIDTH),
                zspec(CB_QI, COL), zspec(CB_KIW, COL), gspec, gspec, gspec]
    args = [z, z, z, z, z, gq, gk, gki_pad]
    aliases = {}
    for n, buf in enumerate(stacked):
        aliases[len(args)] = 5 + n
        in_specs.append(pl.BlockSpec(memory_space=pl.ANY))
        args.append(buf)
    return pl.pallas_call(
        _post_kernel,
        grid=(N_PROMPT // LANES,),
        in_specs=in_specs,
        out_specs=[
            pl.BlockSpec((1, WIDTH, LANES), lambda r: (r // nb, 0, r % nb)),
            pl.BlockSpec((1, N_HEADS * IDX_DIM, LANES), lambda r: (r // nb, 0, r % nb)),
            pl.BlockSpec((1, N_HEADS, LANES), lambda r: (r // nb, 0, r % nb)),
            pl.BlockSpec((LANES, WIDTH), lambda r: (r, 0)),
            pl.BlockSpec((1, 1, WIDTH, LANES),
                         lambda r: (r // nb, (r % nb) // (TQ // LANES), 0, r % (TQ // LANES))),
            pl.BlockSpec((None, LANES, WIDTH), lambda r: (layer, r, 0)),
            pl.BlockSpec((None, LANES, WIDTH), lambda r: (layer, r, 0)),
            pl.BlockSpec((None, LANES, LANES), lambda r: (layer, r, 0)),
        ],
        out_shape=[
            jax.ShapeDtypeStruct((BATCH, WIDTH, SEQ), BF16),
            jax.ShapeDtypeStruct((BATCH, N_HEADS * IDX_DIM, SEQ), F32),
            jax.ShapeDtypeStruct((BATCH, N_HEADS, SEQ), F32),
            jax.ShapeDtypeStruct((N_PROMPT, WIDTH), BF16),
            jax.ShapeDtypeStruct((BATCH, NQB, WIDTH, TQ), BF16),
            jax.ShapeDtypeStruct((DEPTH, N_PROMPT, WIDTH), F32),
            jax.ShapeDtypeStruct((DEPTH, N_PROMPT, WIDTH), F32),
            jax.ShapeDtypeStruct((DEPTH, N_PROMPT, LANES), F32),
        ],
        input_output_aliases=aliases,
        compiler_params=_cparams(("arbitrary",)),
        name="post_a",
    )(*args)


def _sortable_key(score):
    score = jnp.where(score == 0.0, 0.0, score)
    bits = pltpu.bitcast(score, I32)
    return jnp.where(bits < 0, bits ^ 0x7FFFFFFF, bits)


def _attn_kernel(qT_ref, qiT_ref, wT_ref, k_ref, vT_ref, ki_ref, bias_ref, _rows_ref, o_ref,
                 keys_ref, hi_ref, lo_ref, eq_ref, sel_ref, acc_ref, p_ref, cut_ref):
    qb = pl.program_id(1)
    nkb = qb + 1
    t_pos = qb * TQ + lax.broadcasted_iota(I32, (1, TQ), 1)
    s_loc = lax.broadcasted_iota(I32, (TQ, TQ), 0)

    qi = qiT_ref[0].astype(BF16)
    w = wT_ref[0]

    def score_blk(kb, carry):
        r0 = pl.multiple_of(kb * TQ, TQ)
        ki = ki_ref[pl.ds(r0, TQ), :][:, 0:IDX_DIM].astype(BF16)
        acc = jnp.zeros((TQ, TQ), F32)
        for h in range(N_HEADS):
            s = jnp.dot(ki, qi[h * IDX_DIM:(h + 1) * IDX_DIM, :], preferred_element_type=F32)
            acc = acc + jnp.maximum(s, 0.0) * w[h:h + 1, :]
        key = jnp.where(kb * TQ + s_loc <= t_pos, _sortable_key(acc), INT_MIN)
        keys_ref[kb] = key
        hi_ref[kb] = (key >> 16).astype(I16)
        lo_ref[kb] = ((key & 0xFFFF) - HALF16).astype(I16)
        return carry

    lax.fori_loop(0, nkb, score_blk, 0)

    @pl.when(nkb % 2 == 1)
    def _():
        keys_ref[nkb] = jnp.full((TQ, TQ), INT_MIN, I32)
        hi_ref[nkb] = jnp.full((TQ, TQ), -HALF16, I16)
        lo_ref[nkb] = jnp.full((TQ, TQ), -HALF16, I16)

    n_pairs = (nkb + 1) // 2

    def count(indicator):
        def body(i, acc):
            for d in range(2):
                kb = 2 * i + d
                x = indicator(keys_ref[kb], kb * TQ + s_loc)
                acc = acc + jnp.sum(x.reshape(TQ // SUBLANES, SUBLANES, TQ), axis=0)
            return acc
        acc = lax.fori_loop(0, n_pairs, body, jnp.zeros((SUBLANES, TQ), F32))
        return jnp.sum(acc, axis=0, keepdims=True)

    def count16(indicator):
        rows = 2 * SUBLANES
        def body(i, acc):
            parts = []
            for d in range(2):
                x = indicator(2 * i + d).reshape(TQ // rows, rows, TQ)
                parts += [x[r] for r in range(TQ // rows)]
            while len(parts) > 1:
                parts = [parts[n] + parts[n + 1] for n in range(0, len(parts), 2)]
            return acc + parts[0]
        acc = lax.fori_loop(0, n_pairs, body, jnp.zeros((rows, TQ), I16))
        return jnp.sum(acc.astype(I32), axis=0, keepdims=True)

    one16 = jnp.int16(1)
    zero16 = jnp.int16(0)

    def bisect16(count_ge, target):
        cnt = count_ge(jnp.zeros((1, TQ), I16))
        ok = cnt >= target
        v0 = jnp.where(ok, 0, -HALF16).astype(I32)

        def step(i, carry):
            v, c_at = carry
            cand = v + jnp.left_shift(jnp.int32(1), 14 - i)
            cnt = count_ge(cand.astype(I16))
            ok = cnt >= target
            return jnp.where(ok, cand, v), jnp.where(ok, cnt, c_at)

        return lax.fori_loop(0, 15, step, (v0, jnp.where(ok, cnt, -1)))

    k_eff = jnp.minimum(TOPK, t_pos + 1)
    tau_hi, _ = bisect16(
        lambda c: count16(lambda kb: jnp.where(hi_ref[kb] >= c, one16, zero16)), k_eff)
    tau_hi16 = tau_hi.astype(I16)
    n_gt = count16(lambda kb: jnp.where(hi_ref[kb] > tau_hi16, one16, zero16))

    def eq_blk(i, carry):
        for d in range(2):
            kb = 2 * i + d
            eq_ref[kb] = jnp.where(hi_ref[kb] == tau_hi16, one16, zero16)
        return carry

    lax.fori_loop(0, n_pairs, eq_blk, 0)
    n_eq_hi = count16(lambda kb: eq_ref[kb])
    need_lo = k_eff - n_gt
    tau_lo, n_ge_lo = bisect16(
        lambda c: count16(lambda kb: jnp.where(lo_ref[kb] >= c, eq_ref[kb], zero16)), need_lo)
    n_ge_lo = jnp.where(n_ge_lo < 0, n_eq_hi, n_ge_lo)
    tau = tau_hi * (2 * HALF16) + (tau_lo + HALF16)

    cut_ref[...] = jnp.full((1, TQ), BIG_IDX, I32)

    @pl.when(jnp.max((n_gt + n_ge_lo - k_eff).astype(F32)) > 0.0)
    def _():
        need = k_eff.astype(F32) - count(lambda key, s: jnp.where(key > tau, 1.0, 0.0))

        def idx_step(i, c):
            trial = c + jnp.left_shift(jnp.int32(1), 10 - i)
            cnt = count(lambda key, s: jnp.where(key == tau, jnp.where(s < trial, 1.0, 0.0), 0.0))
            return jnp.where(cnt < need, trial, c)

        cut_ref[...] = lax.fori_loop(0, 11, idx_step, jnp.zeros((1, TQ), I32))

    cut = cut_ref[...]

    def sel_blk(kb, carry):
        key = keys_ref[kb]
        s_pos = kb * TQ + s_loc
        sel_ref[kb] = jnp.where(
            key > tau, 0.0,
            jnp.where(key == tau, jnp.where(s_pos <= cut, 0.0, NEG_LARGE), NEG_LARGE))
        return carry

    lax.fori_loop(0, nkb, sel_blk, 0)

    n_sub = TQ // TK_SUB

    def logits(h, kb, u):
        hs = slice(h * HEAD_DIM, (h + 1) * HEAD_DIM)
        r0 = pl.multiple_of(kb * TQ + u * TK_SUB, TK_SUB)
        us = slice(u * TK_SUB, (u + 1) * TK_SUB)
        dc = jnp.minimum(qb - kb, 2)
        s = jnp.dot(k_ref[pl.ds(r0, TK_SUB), hs], qT_ref[0, hs, :], preferred_element_type=F32)
        return s * ATTN_SCALE + bias_ref[dc, h, us, :] + sel_ref[kb, us, :]

    def fold8(x):
        return x.reshape(TK_SUB // SUBLANES, SUBLANES, TQ)

    def max_blk(kb, ms):
        out = []
        for h in range(N_HEADS):
            m8 = ms[h]
            for u in range(n_sub):
                m8 = jnp.maximum(m8, jnp.max(fold8(logits(h, kb, u)), axis=0))
            out.append(m8)
        return tuple(out)

    ms = lax.fori_loop(0, nkb, max_blk,
                       tuple(jnp.full((SUBLANES, TQ), NEG_LARGE, F32) for _ in range(N_HEADS)))
    m = [jnp.max(m8, axis=0, keepdims=True) for m8 in ms]
    acc_ref[...] = jnp.zeros(acc_ref.shape, F32)

    def pv_blk(kb, ls):
        out = []
        for h in range(N_HEADS):
            l8 = ls[h]
            for u in range(n_sub):
                p = jnp.exp(logits(h, kb, u) - m[h])
                l8 = l8 + jnp.sum(fold8(p), axis=0)
                p_ref[h, u * TK_SUB:(u + 1) * TK_SUB, :] = p.astype(BF16)
            out.append(l8)
        for h in range(N_HEADS):
            hs = slice(h * HEAD_DIM, (h + 1) * HEAD_DIM)
            acc_ref[h] += jnp.dot(vT_ref[0, kb, hs, :], p_ref[h], preferred_element_type=F32)
        return tuple(out)

    ls = lax.fori_loop(0, nkb, pv_blk,
                       tuple(jnp.zeros((SUBLANES, TQ), F32) for _ in range(N_HEADS)))
    for h in range(N_HEADS):
        o = acc_ref[h] / jnp.sum(ls[h], axis=0, keepdims=True)
        o_ref[:, h * HEAD_DIM:(h + 1) * HEAD_DIM] = o.T.astype(o_ref.dtype)


def _attn_prompt(qT, qiT, wT, knb, vT, ki_all, bias, layer):
    return pl.pallas_call(
        _attn_kernel,
        grid=(BATCH, NQB),
        in_specs=[
            pl.BlockSpec((1, WIDTH, TQ), lambda b, q: (b, 0, q)),
            pl.BlockSpec((1, N_HEADS * IDX_DIM, TQ), lambda b, q: (b, 0, q)),
            pl.BlockSpec((1, N_HEADS, TQ), lambda b, q: (b, 0, q)),
            pl.BlockSpec((SEQ, WIDTH), lambda b, q: (b, 0)),
            pl.BlockSpec((1, NQB, WIDTH, TQ), lambda b, q: (b, 0, 0, 0)),
            pl.BlockSpec((None, SEQ, LANES), lambda b, q: (layer, b, 0)),
            pl.BlockSpec((3, N_HEADS, TQ, TQ), lambda b, q: (0, 0, 0, 0)),
            pl.BlockSpec(memory_space=pl.ANY),
        ],
        out_specs=pl.BlockSpec((TQ, WIDTH), lambda b, q: (b * NQB + q, 0)),
        out_shape=jax.ShapeDtypeStruct((M_ALL, WIDTH), BF16),
        input_output_aliases={7: 0},
        scratch_shapes=[
            pltpu.VMEM((NQB, TQ, TQ), I32),
            pltpu.VMEM((NQB, TQ, TQ), I16),
            pltpu.VMEM((NQB, TQ, TQ), I16),
            pltpu.VMEM((NQB, TQ, TQ), I16),
            pltpu.VMEM((NQB, TQ, TQ), F32),
            pltpu.VMEM((N_HEADS, HEAD_DIM, TQ), F32),
            pltpu.VMEM((N_HEADS, TQ, TQ), BF16),
            pltpu.VMEM((1, TQ), I32),
        ],
        compiler_params=_cparams(("arbitrary", "arbitrary")),
        name="attn_prompt",
    )(qT, qiT, wT, knb, vT, ki_all, bias, jnp.zeros((M_ALL, WIDTH), BF16))


def _hgrn_kernel(zq_ref, zf_ref, zi_ref, zg_ref, lb_ref, og_ref, _rows_ref, ob_ref, st_ref,
                 ST_ref, b_ref, k_ref, q_ref, i_ref, oi_ref):
    ST_ref[...] = jnp.zeros(ST_ref.shape, F32)
    og = og_ref[...]
    row = lax.broadcasted_iota(I32, (CHUNK, LANES), 0)
    lane = lax.broadcasted_iota(I32, (CHUNK, LANES), 1)
    sub_row = lax.broadcasted_iota(I32, (SUBLANES, LANES), 0)

    def chunk(c, carry):
        r0 = pl.multiple_of(c * CHUNK, CHUNK)
        b_last = []
        for hh in range(HPB):
            hs = slice(hh * LANES, (hh + 1) * LANES)
            lb = lb_ref[hh]
            f = lb + (1.0 - lb) * _sigmoid(zf_ref[pl.ds(r0, CHUNK), hs])
            b = jnp.log(f)
            for d in (1, 2, 4, 8):
                b = b + jnp.where(row % SUB >= d, pltpu.roll(b, d, axis=0), 0.0)
            b_ref[hh] = b
            k_ref[hh] = 1.0 - f
            q_ref[hh] = _silu(zq_ref[pl.ds(r0, CHUNK), hs])
            i_ref[hh] = zi_ref[pl.ds(r0, CHUNK), hs]
            b_last.append(b_ref[hh, pl.ds(SUB - 1, N_SUB, stride=SUB), :])

        def sub_block(j, carry2):
            j0 = pl.multiple_of(j * SUB, SUB)
            for hh in range(HPB):
                hs = slice(hh * LANES, (hh + 1) * LANES)
                b_lo = b_ref[hh, pl.ds(j0, SUBLANES), :]
                b_hi = b_ref[hh, pl.ds(j0 + SUBLANES, SUBLANES), :]
                q_lo = q_ref[hh, pl.ds(j0, SUBLANES), :]
                q_hi = q_ref[hh, pl.ds(j0 + SUBLANES, SUBLANES), :]
                o_lo = jnp.zeros((SUBLANES, LANES), F32)
                o_hi = jnp.zeros((SUBLANES, LANES), F32)
                for s in range(SUB):
                    bs = jnp.broadcast_to(b_ref[hh, pl.ds(j0 + s, 1), :], (SUBLANES, LANES))
                    ks = jnp.broadcast_to(k_ref[hh, pl.ds(j0 + s, 1), :], (SUBLANES, LANES))
                    iv = jnp.broadcast_to(i_ref[hh, pl.ds(j0 + s, 1), :], (SUBLANES, LANES))
                    if s < SUBLANES:
                        ok = sub_row >= s
                        x = jnp.where(ok, q_lo * ks * jnp.exp(jnp.where(ok, b_lo - bs, 0.0)), 0.0)
                        o_lo = o_lo + jnp.sum(x, axis=-1, keepdims=True) * iv
                        x = q_hi * ks * jnp.exp(b_hi - bs)
                        o_hi = o_hi + jnp.sum(x, axis=-1, keepdims=True) * iv
                    else:
                        ok = sub_row >= s - SUBLANES
                        x = jnp.where(ok, q_hi * ks * jnp.exp(jnp.where(ok, b_hi - bs, 0.0)), 0.0)
                        o_hi = o_hi + jnp.sum(x, axis=-1, keepdims=True) * iv
                oi_ref[hh, pl.ds(j0, SUBLANES), :] = o_lo
                oi_ref[hh, pl.ds(j0 + SUBLANES, SUBLANES), :] = o_hi
            return carry2

        lax.fori_loop(0, N_SUB, sub_block, 0)

        for hh in range(HPB):
            hs = slice(hh * LANES, (hh + 1) * LANES)
            b = b_ref[hh]
            bl_b = jnp.concatenate(
                [jnp.broadcast_to(b_last[hh][j:j + 1, :], (SUB, LANES)) for j in range(N_SUB)], axis=0)
            q_dec = (q_ref[hh] * jnp.exp(b)).astype(BF16)
            k_end = (k_ref[hh] * jnp.exp(bl_b - b)).astype(BF16)
            decay = jnp.exp(b_last[hh])
            iT = i_ref[hh].T
            it_stack = jnp.concatenate(
                [jnp.where(lane // SUB == j, iT, 0.0).astype(BF16) for j in range(N_SUB)], axis=0)
            pT = jnp.dot(it_stack, k_end, preferred_element_type=F32)
            r = ST_ref[hh]
            r_list = []
            for j in range(N_SUB):
                r_list.append(r.astype(BF16))
                r = r * decay[j:j + 1, :] + pT[j * LANES:(j + 1) * LANES, :]
            ST_ref[hh] = r
            r_stack = jnp.concatenate(r_list, axis=0)
            oT_all = lax.dot_general(r_stack, q_dec, (((1,), (1,)), ((), ())),
                                     preferred_element_type=F32)
            oT = jnp.zeros((LANES, CHUNK), F32)
            for j in range(N_SUB):
                oT = oT + jnp.where(lane // SUB == j, oT_all[j * LANES:(j + 1) * LANES, :], 0.0)
            o = oT.T + oi_ref[hh]
            zg = zg_ref[pl.ds(r0, CHUNK), hs]
            ob_ref[pl.ds(r0, CHUNK), hs] = (_head_norm(o, og) * _silu(zg)).astype(ob_ref.dtype)
        return carry

    lax.fori_loop(0, SEQ // CHUNK, chunk, 0)
    for hh in range(HPB):
        st_ref[hh] = ST_ref[hh].T


def _hgrn_prompt(z, lb3, og):
    w = HPB * LANES
    zspec = lambda cb: pl.BlockSpec((SEQ, w), lambda b, h: (b, cb * COL // w + h))
    return pl.pallas_call(
        _hgrn_kernel,
        grid=(BATCH, N_HEADS // HPB),
        in_specs=[zspec(CB_HQ), zspec(CB_HF), zspec(CB_HI), zspec(CB_HG),
                  pl.BlockSpec((HPB, 1, LANES), lambda b, h: (h, 0, 0)),
                  pl.BlockSpec((1, LANES), lambda b, h: (0, 0)),
                  pl.BlockSpec(memory_space=pl.ANY)],
        out_specs=[pl.BlockSpec((SEQ, w), lambda b, h: (b, h)),
                   pl.BlockSpec((None, HPB, LANES, LANES), lambda b, h: (b, h, 0, 0))],
        out_shape=[jax.ShapeDtypeStruct((M_ALL, WIDTH), BF16),
                   jax.ShapeDtypeStruct((BATCH, N_HEADS, LANES, LANES), F32)],
        input_output_aliases={6: 0},
        scratch_shapes=[pltpu.VMEM((HPB, LANES, LANES), F32),
                        pltpu.VMEM((HPB, CHUNK, LANES), F32),
                        pltpu.VMEM((HPB, CHUNK, LANES), F32),
                        pltpu.VMEM((HPB, CHUNK, LANES), F32),
                        pltpu.VMEM((HPB, CHUNK, LANES), F32),
                        pltpu.VMEM((HPB, CHUNK, LANES), F32)],
        compiler_params=_cparams(("arbitrary", "arbitrary")),
        name="hgrn_prompt",
    )(z, z, z, z, lb3, og, jnp.zeros((M_ALL, WIDTH), BF16))


def _rows_to_cols(x):
    pad = jnp.zeros((LANES - x.shape[0], LANES), F32)
    return jnp.concatenate([x, pad], axis=0).T


def _hgrn_sample_kernel(zq_ref, zf_ref, zi_ref, zg_ref, lb_ref, og_ref, s0_ref, ob_any_ref,
                        ob_ref, s1_ref):
    del ob_any_ref
    lb = lb_ref[...]
    f = lb + (1.0 - lb) * _sigmoid(zf_ref[...])
    fT = _rows_to_cols(f)
    kT = _rows_to_cols(1.0 - f)
    qT = _rows_to_cols(_silu(zq_ref[...]))
    iv = zi_ref[...]
    rows = []
    for b in range(DEC_BATCH):
        s1 = fT[:, b:b + 1] * s0_ref[b] + kT[:, b:b + 1] * iv[b:b + 1, :]
        s1_ref[b] = s1
        rows.append(jnp.sum(qT[:, b:b + 1] * s1, axis=0, keepdims=True))
    o = jnp.concatenate(rows, axis=0)
    o = _head_norm(o, og_ref[...]) * _silu(zg_ref[...])
    pad = jnp.zeros((LANES - DEC_BATCH, LANES), F32)
    ob_ref[...] = jnp.concatenate([o, pad], axis=0).astype(ob_ref.dtype)


def _hgrn_sample(z, lb3, og, state_hgrn, ob_all, layer):
    cpb = COL // LANES
    zspec = lambda cb: pl.BlockSpec((DEC_BATCH, LANES), lambda h: (SAMPLE_ROW_BLOCK, cb * cpb + h))
    return pl.pallas_call(
        _hgrn_sample_kernel,
        grid=(N_HEADS,),
        in_specs=[zspec(CB_HQ), zspec(CB_HF), zspec(CB_HI), zspec(CB_HG),
                  pl.BlockSpec((None, 1, LANES), lambda h: (h, 0, 0)),
                  pl.BlockSpec((1, LANES), lambda h: (0, 0)),
                  pl.BlockSpec((None, DEC_BATCH, None, LANES, LANES), lambda h: (layer, 0, h, 0, 0)),
                  pl.BlockSpec(memory_space=pl.ANY)],
        out_specs=[pl.BlockSpec((LANES, LANES), lambda h: (SAMPLE_TILE, h)),
                   pl.BlockSpec((DEC_BATCH, None, LANES, LANES), lambda h: (0, h, 0, 0))],
        out_shape=[jax.ShapeDtypeStruct((M_ALL, WIDTH), BF16),
                   jax.ShapeDtypeStruct((DEC_BATCH, N_HEADS, LANES, LANES), F32)],
        input_output_aliases={7: 0},
        compiler_params=_cparams(("arbitrary",)),
        name="hgrn_sample",
    )(z, z, z, z, lb3, og, state_hgrn, ob_all)


def _sample_norm_kernel(zq_ref, zk_ref, zkiw_ref, gq_ref, gk_ref, gki_ref, qn_ref, kn_ref, kiw_ref):
    gq = gq_ref[...]
    gk = gk_ref[...]
    for h in range(N_HEADS):
        hs = slice(h * HEAD_DIM, (h + 1) * HEAD_DIM)
        qn_ref[:, hs] = _head_norm(zq_ref[:, hs], gq)
        kn_ref[:, hs] = _head_norm(zk_ref[:, hs], gk)
    kiw = zkiw_ref[:, 0:LANES]
    lane = lax.broadcasted_iota(I32, kiw.shape, 1)
    kiw_ref[...] = _kidx_norm(kiw, gki_ref[...]) + jnp.where(
        (lane >= IDX_DIM) & (lane < IDX_DIM + N_HEADS), kiw * IDX_WEIGHT_SCALE, 0.0)


def _sample_norms(z, gq, gk, gki_pad):
    zspec = lambda cb, w: pl.BlockSpec((DEC_BATCH, w), lambda i: (SAMPLE_ROW_BLOCK, cb * COL // w))
    gspec = pl.BlockSpec((1, LANES), lambda i: (0, 0))
    return pl.pallas_call(
        _sample_norm_kernel,
        grid=(1,),
        in_specs=[zspec(CB_Q, WIDTH), zspec(CB_K, WIDTH), zspec(CB_KIW, COL), gspec, gspec, gspec],
        out_specs=[pl.BlockSpec((DEC_BATCH, WIDTH), lambda i: (0, 0)),
                   pl.BlockSpec((DEC_BATCH, WIDTH), lambda i: (0, 0)),
                   pl.BlockSpec((DEC_BATCH, LANES), lambda i: (0, 0))],
        out_shape=[jax.ShapeDtypeStruct((DEC_BATCH, WIDTH), F32),
                   jax.ShapeDtypeStruct((DEC_BATCH, WIDTH), F32),
                   jax.ShapeDtypeStruct((DEC_BATCH, LANES), F32)],
        compiler_params=_cparams(("arbitrary",)),
        name="sample_norms",
    )(z, z, z, gq, gk, gki_pad)


KEY_ROWS = N_PAGES + SUBLANES


def _select_row(x, b):
    sub = lax.broadcasted_iota(I32, x.shape, 0)
    return jnp.sum(jnp.where(sub == b, x, 0.0), axis=0, keepdims=True)


def _sample_topk_kernel(pt_ref, zqi_ref, kiw_ref, cache_ref, sel_ref, buf_ref, sem_ref, key_ref,
                        pos_ref, cut_ref, *, layer):
    b = pl.program_id(0)

    def page_copy(p):
        return pltpu.make_async_copy(cache_ref.at[layer, pt_ref[b, p]], buf_ref.at[p], sem_ref.at[0])

    def start(p, c):
        page_copy(p).start()
        return c

    lax.fori_loop(0, N_PAGES, start, 0)

    qrow = _select_row(zqi_ref[...], b)
    head = lax.broadcasted_iota(I32, (N_HEADS, N_HEADS * IDX_DIM), 0)
    col = lax.broadcasted_iota(I32, (N_HEADS, N_HEADS * IDX_DIM), 1)
    qm = jnp.where(col // IDX_DIM == head, jnp.broadcast_to(qrow, head.shape), 0.0)
    q128 = qm[:, 0:LANES]
    for p in range(1, N_HEADS * IDX_DIM // LANES):
        q128 = q128 + qm[:, p * LANES:(p + 1) * LANES]
    lane8 = lax.broadcasted_iota(I32, (N_HEADS, LANES), 1)
    q8 = jnp.where(lane8 < IDX_DIM, q128 + pltpu.roll(q128, IDX_DIM, axis=1), 0.0)
    qT = _rows_to_cols(q8)[0:IDX_DIM, :]

    kiw_row = _select_row(kiw_ref[...], b)
    r8 = lax.broadcasted_iota(I32, (N_HEADS, LANES), 0)
    wcol = jnp.sum(jnp.where(lane8 == IDX_DIM + r8, jnp.broadcast_to(kiw_row, (N_HEADS, LANES)), 0.0),
                   axis=-1, keepdims=True)
    knew = jnp.where(lane8[0:1] < IDX_DIM, kiw_row, 0.0)
    sn = jnp.sum(q8 * knew, axis=-1, keepdims=True)
    s_new = jnp.sum(jnp.maximum(sn, 0.0) * wcol, axis=0, keepdims=True)

    def wait(p, c):
        page_copy(p).wait()
        return c

    lax.fori_loop(0, N_PAGES, wait, 0)

    qcols = [jnp.broadcast_to(qT[:, h:h + 1], (IDX_DIM, PAGE_SIZE))[None] for h in range(N_HEADS)]

    def page_group(g, carry):
        r0 = pl.multiple_of(g * SUBLANES, SUBLANES)
        pages = buf_ref[pl.ds(r0, SUBLANES)]
        s = jnp.zeros((SUBLANES, PAGE_SIZE), F32)
        for h in range(N_HEADS):
            sh = jnp.sum(pages * qcols[h], axis=1)
            s = s + jnp.maximum(sh, 0.0) * wcol[h:h + 1, :]
        key_ref[b, pl.ds(r0, SUBLANES), :] = _sortable_key(s)
        return carry

    lax.fori_loop(0, N_PAGES // SUBLANES, page_group, 0)
    row_t = lax.broadcasted_iota(I32, (SUBLANES, LANES), 0)
    lane_t = lax.broadcasted_iota(I32, (SUBLANES, LANES), 1)
    key_ref[b, pl.ds(N_PAGES, SUBLANES), :] = jnp.where(
        (row_t == 0) & (lane_t == 0),
        _sortable_key(jnp.broadcast_to(s_new, (SUBLANES, LANES))), INT_MIN)

    @pl.when(b == DEC_BATCH - 1)
    def _():
        shape = (DEC_BATCH, KEY_ROWS, LANES)
        pos = lax.broadcasted_iota(I32, shape, 1) * LANES + lax.broadcasted_iota(I32, shape, 2)

        def count(indicator):
            x = indicator(key_ref[...])
            return jnp.sum(jnp.sum(x, axis=1), axis=1, keepdims=True)

        cnt = count(lambda key: jnp.where(key >= 0, 1.0, 0.0))
        tau0 = jnp.where(cnt >= TOPK, 0, INT_MIN).astype(I32)

        def bit_step(i, tau):
            cand = tau + jnp.left_shift(jnp.int32(1), 30 - i)
            cnt = count(lambda key: jnp.where(key >= cand[:, :, None], 1.0, 0.0))
            return jnp.where(cnt >= TOPK, cand, tau)

        tau = lax.fori_loop(0, 31, bit_step, tau0)[:, :, None]

        need = TOPK - count(lambda key: jnp.where(key > tau, 1.0, 0.0))
        n_eq = count(lambda key: jnp.where(key == tau, 1.0, 0.0))
        cut_ref[...] = jnp.full(cut_ref.shape, BIG_IDX, I32)

        @pl.when(jnp.max(n_eq - need) > 0.0)
        def _():
            def idx_step(i, c):
                trial = c + jnp.left_shift(jnp.int32(1), 14 - i)
                cnt = count(lambda key: jnp.where(key == tau, jnp.where(pos < trial[:, :, None], 1.0, 0.0), 0.0))
                return jnp.where(cnt < need, trial, c)
            c = lax.fori_loop(0, 15, idx_step, jnp.zeros((DEC_BATCH, 1), I32))
            cut_ref[...] = jnp.broadcast_to(c, cut_ref.shape)

        cut = cut_ref[:, 0:1][:, :, None]
        key = key_ref[...]
        chosen = jnp.where(key > tau, pos,
                           jnp.where(key == tau, jnp.where(pos <= cut, pos, BIG_IDX), BIG_IDX))
        pos_ref[...] = chosen.astype(F32)

        out_lane = lax.broadcasted_iota(I32, (DEC_BATCH, TOPK), 1)

        def pick(it, carry):
            sel, last = carry
            p = pos_ref[...]
            first = jnp.min(jnp.min(jnp.where(p > last[:, :, None], p, float(BIG_IDX)), axis=1),
                            axis=1, keepdims=True)
            return jnp.where(out_lane == it, first.astype(I32), sel), first

        sel, _ = lax.fori_loop(0, TOPK, pick, (jnp.zeros((DEC_BATCH, TOPK), I32),
                                               jnp.full((DEC_BATCH, 1), -1.0, F32)))
        sel_ref[...] = sel


def _sample_topk(page_table, z, kiw_s, cache_kidx, layer):
    grid_spec = pltpu.PrefetchScalarGridSpec(
        num_scalar_prefetch=1,
        grid=(DEC_BATCH,),
        in_specs=[pl.BlockSpec((DEC_BATCH, COL), lambda b, pt: (SAMPLE_ROW_BLOCK, CB_QI)),
                  pl.BlockSpec((DEC_BATCH, LANES), lambda b, pt: (0, 0)),
                  pl.BlockSpec(memory_space=pl.ANY)],
        out_specs=pl.BlockSpec((DEC_BATCH, TOPK), lambda b, pt: (0, 0)),
        scratch_shapes=[pltpu.VMEM((N_PAGES, IDX_DIM, PAGE_SIZE), F32),
                        pltpu.SemaphoreType.DMA((1,)),
                        pltpu.VMEM((DEC_BATCH, KEY_ROWS, LANES), I32),
                        pltpu.VMEM((DEC_BATCH, KEY_ROWS, LANES), F32),
                        pltpu.VMEM((DEC_BATCH, LANES), I32)],
    )
    cache_t = jnp.swapaxes(cache_kidx, 2, 3)
    return pl.pallas_call(
        functools.partial(_sample_topk_kernel, layer=layer),
        grid_spec=grid_spec,
        out_shape=jax.ShapeDtypeStruct((DEC_BATCH, TOPK), I32),
        compiler_params=_cparams(("arbitrary",)),
        name="sample_topk",
    )(page_table, z, kiw_s, cache_t)


def _sample_attn_kernel(sel_s_ref, pt_ref, sel_ref, qn_ref, kn_ref, zv_ref, rbT_ref,
                        ck_ref, cv_ref, oa_any_ref, o_ref, kbuf_ref, vbuf_ref, sem_ref, rows_ref,
                        *, layer):
    del oa_any_ref
    b = pl.program_id(0)

    def copies(j):
        sp = jnp.minimum(sel_s_ref[b, j], PAST_LEN - 1)
        page = pt_ref[b, sp // PAGE_SIZE]
        off = sp % PAGE_SIZE
        dst = pl.ds(pl.multiple_of(j * N_HEADS, N_HEADS), N_HEADS)
        return (pltpu.make_async_copy(ck_ref.at[layer, page, off], kbuf_ref.at[dst], sem_ref.at[0]),
                pltpu.make_async_copy(cv_ref.at[layer, page, off], vbuf_ref.at[dst], sem_ref.at[1]))

    def start(j, c):
        ck, cv = copies(j)
        ck.start()
        cv.start()
        return c

    lax.fori_loop(0, TOPK, start, 0)

    def heads_on_rows(x):
        row = _select_row(x, b)
        return jnp.concatenate([row[:, h * HEAD_DIM:(h + 1) * HEAD_DIM] for h in range(N_HEADS)],
                               axis=0)

    q = heads_on_rows(qn_ref[...])
    k_new = heads_on_rows(kn_ref[...])
    v_new = heads_on_rows(zv_ref[...])
    sel_b = jnp.sum(jnp.where(lax.broadcasted_iota(I32, (DEC_BATCH, TOPK), 0) == b, sel_ref[...], 0),
                    axis=0, keepdims=True)
    is_new = sel_b >= PAST_LEN
    bucket = _t5_bucket(jnp.maximum(PAST_LEN - sel_b, 0))
    bias = jnp.zeros((N_HEADS, TOPK), F32)
    for beta in range(NUM_BUCKETS):
        bias = jnp.where(bucket == beta, rbT_ref[:, beta:beta + 1], bias)

    def wait(j, c):
        ck, cv = copies(j)
        ck.wait()
        cv.wait()
        return c

    lax.fori_loop(0, TOPK, wait, 0)

    nt = (((1,), (1,)), ((), ()))
    hrow = lax.broadcasted_iota(I32, (N_HEADS, TOPK), 0)
    qb16 = q.astype(BF16)
    logits = jnp.zeros((N_HEADS, TOPK), F32)
    for h in range(N_HEADS):
        kh = kbuf_ref[pl.ds(h, TOPK, stride=N_HEADS), :].astype(BF16)
        res = lax.dot_general(qb16, kh, nt, preferred_element_type=F32)
        logits = jnp.where(hrow == h, res, logits)
    logit_new = jnp.sum(qb16.astype(F32) * k_new.astype(BF16).astype(F32), axis=-1, keepdims=True)
    logits = jnp.where(is_new, logit_new, logits) * ATTN_SCALE + bias
    m = jnp.max(logits, axis=1, keepdims=True)
    e = jnp.exp(logits - m)
    p = e / jnp.sum(e, axis=1, keepdims=True)
    p_new = jnp.sum(jnp.where(is_new, p, 0.0), axis=1, keepdims=True)
    p_mm = jnp.where(is_new, 0.0, p).astype(BF16)
    hrow_o = lax.broadcasted_iota(I32, (N_HEADS, HEAD_DIM), 0)
    o = jnp.zeros((N_HEADS, HEAD_DIM), F32)
    for h in range(N_HEADS):
        vh = vbuf_ref[pl.ds(h, TOPK, stride=N_HEADS), :].astype(BF16)
        res = jnp.dot(p_mm, vh, preferred_element_type=F32)
        o = jnp.where(hrow_o == h, res, o)
    o = o + p_new.astype(BF16).astype(F32) * v_new.astype(BF16).astype(F32)
    orow = jnp.concatenate([o[h:h + 1, :] for h in range(N_HEADS)], axis=1)

    @pl.when(b == 0)
    def _():
        rows_ref[...] = jnp.zeros(rows_ref.shape, F32)

    sub = lax.broadcasted_iota(I32, (DEC_BATCH, WIDTH), 0)
    rows_ref[...] = jnp.where(sub == b, jnp.broadcast_to(orow, (DEC_BATCH, WIDTH)), rows_ref[...])

    @pl.when(b == DEC_BATCH - 1)
    def _():
        pad = jnp.zeros((LANES - DEC_BATCH, WIDTH), F32)
        o_ref[...] = jnp.concatenate([rows_ref[...], pad], axis=0).astype(o_ref.dtype)


def _sample_attn(sel, page_table, qn_s, kn_s, z, rbT, cache_k, cache_v, oa_all, layer):
    grid_spec = pltpu.PrefetchScalarGridSpec(
        num_scalar_prefetch=2,
        grid=(DEC_BATCH,),
        in_specs=[pl.BlockSpec((DEC_BATCH, TOPK), lambda b, s, pt: (0, 0)),
                  pl.BlockSpec((DEC_BATCH, WIDTH), lambda b, s, pt: (0, 0)),
                  pl.BlockSpec((DEC_BATCH, WIDTH), lambda b, s, pt: (0, 0)),
                  pl.BlockSpec((DEC_BATCH, WIDTH), lambda b, s, pt: (SAMPLE_ROW_BLOCK, CB_V * COL // WIDTH)),
                  pl.BlockSpec((N_HEADS, NUM_BUCKETS), lambda b, s, pt: (0, 0)),
                  pl.BlockSpec(memory_space=pl.ANY),
                  pl.BlockSpec(memory_space=pl.ANY),
                  pl.BlockSpec(memory_space=pl.ANY)],
        out_specs=pl.BlockSpec((LANES, WIDTH), lambda b, s, pt: (SAMPLE_TILE, 0)),
        scratch_shapes=[pltpu.VMEM((TOPK * N_HEADS, HEAD_DIM), F32),
                        pltpu.VMEM((TOPK * N_HEADS, HEAD_DIM), F32),
                        pltpu.SemaphoreType.DMA((2,)),
                        pltpu.VMEM((DEC_BATCH, WIDTH), F32)],
    )
    return pl.pallas_call(
        functools.partial(_sample_attn_kernel, layer=layer),
        grid_spec=grid_spec,
        out_shape=jax.ShapeDtypeStruct((M_ALL, WIDTH), BF16),
        input_output_aliases={9: 0},
        compiler_params=_cparams(("arbitrary",)),
        name="sample_attn",
    )(sel, page_table, sel, qn_s, kn_s, z, rbT, cache_k, cache_v, oa_all)


def _permute_w_in(w_in):
    wt = jnp.swapaxes(w_in, 1, 2)
    parts = []
    off = 0
    for w in IN_WIDTHS:
        parts.append(wt[:, off:off + w])
        off += w
    q, k, v, qi, ki, wi, hq, hf, hi, hg, ga, gb = parts
    pad = jnp.zeros((w_in.shape[0], COL - IDX_DIM - N_HEADS, w_in.shape[1]), w_in.dtype)
    return jnp.concatenate([q, k, v, qi, ki, wi, pad, hq, hf, hi, hg, ga, gb], axis=1).astype(BF16)


def _stack_rows(prompt_rows, sample_rows):
    pad = jnp.zeros((M_ALL - N_PROMPT - DEC_BATCH, prompt_rows.shape[1]), prompt_rows.dtype)
    return jnp.concatenate([prompt_rows, sample_rows.astype(prompt_rows.dtype), pad], axis=0)


def kernel(x_prompt, x_sample, cache_k, cache_v, cache_kidx, state_hgrn, page_table, norm_mix_g, w_in,
           q_norm_g, k_norm_g, kidx_norm_g, rel_bias, hgrn_lb, hgrn_out_g, w_up_a, w_up_b, w_out,
           norm_ffn_g, w_ffn_gate, w_ffn_up, w_ffn_down):
    w_in_p = _permute_w_in(w_in)
    w_up_a, w_up_b, w_out = (w.astype(BF16) for w in (w_up_a, w_up_b, w_out))
    w_ffn_gate, w_ffn_up, w_ffn_down = (w.astype(BF16) for w in (w_ffn_gate, w_ffn_up, w_ffn_down))
    lower = _lower_bounds(hgrn_lb)
    bias = _bias_tiles(rel_bias)
    rbT = rel_bias.T
    gki_pad = jnp.pad(kidx_norm_g, ((0, 0), (0, LANES - IDX_DIM)))

    x = _stack_rows(x_prompt.reshape(N_PROMPT, D_MODEL), x_sample.reshape(DEC_BATCH, D_MODEL))
    stacked = (jnp.zeros((DEPTH, N_PROMPT, WIDTH), F32), jnp.zeros((DEPTH, N_PROMPT, WIDTH), F32),
               jnp.zeros((DEPTH, N_PROMPT, LANES), F32))
    outs = [[] for _ in range(5)]
    for l in range(DEPTH):
        gq = q_norm_g[l].reshape(1, HEAD_DIM)
        gk = k_norm_g[l].reshape(1, HEAD_DIM)
        gki = gki_pad[l].reshape(1, LANES)
        og = hgrn_out_g[l].reshape(1, LANES)
        lb3 = lower[l].reshape(N_HEADS, 1, LANES)

        xn = _rmsnorm(x, norm_mix_g[l])
        z = _in_proj(xn, w_in_p, l)

        qT, qiT, wT, knb, vT, *stacked = _post(z, gq, gk, gki, l, stacked)
        oa = _attn_prompt(qT, qiT, wT, knb, vT, stacked[2], bias, l)
        qn_s, kn_s, kiw_s = _sample_norms(z, gq, gk, gki)
        sel = _sample_topk(page_table, z, kiw_s, cache_kidx, l)
        oa = _sample_attn(sel, page_table, qn_s, kn_s, z, rbT, cache_k, cache_v, oa, l)

        ob, st_p = _hgrn_prompt(z, lb3, og)
        ob, st_s = _hgrn_sample(z, lb3, og, state_hgrn, ob, l)

        mixed = _merge(oa, ob, w_up_a, w_up_b, z, l)
        x = _residual_proj(mixed, w_out, x, l, TM, "out_proj")
        hn = _rmsnorm(x, norm_ffn_g[l])
        x = _residual_proj(_ffn_up(hn, w_ffn_gate, w_ffn_up, l), w_ffn_down, x, l, TM_DOWN, "ffn_down")

        v_cols = slice(CB_V * COL, CB_V * COL + WIDTH)
        outs[0].append(st_p)
        outs[1].append(kn_s.reshape(DEC_BATCH, 1, N_HEADS, HEAD_DIM))
        outs[2].append(z[N_PROMPT:N_PROMPT + DEC_BATCH, v_cols].reshape(DEC_BATCH, 1, N_HEADS, HEAD_DIM))
        outs[3].append(kiw_s[:, :IDX_DIM].reshape(DEC_BATCH, 1, IDX_DIM))
        outs[4].append(st_s)

    k_all, v_all, ki_all = stacked
    y_prompt = x[:N_PROMPT].reshape(BATCH, SEQ, D_MODEL)
    y_sample = x[N_PROMPT:N_PROMPT + DEC_BATCH].reshape(DEC_BATCH, 1, D_MODEL)
    st_p, k_s, v_s, ki_s, st_s = (jnp.stack(o) for o in outs)
    return (y_prompt, y_sample,
            k_all.reshape(DEPTH, BATCH, SEQ, N_HEADS, HEAD_DIM),
            v_all.reshape(DEPTH, BATCH, SEQ, N_HEADS, HEAD_DIM),
            ki_all[:, :, :IDX_DIM].reshape(DEPTH, BATCH, SEQ, IDX_DIM),
            st_p, k_s, v_s, ki_s, st_s)
```

```python
import functools
import math

import jax
import jax.numpy as jnp
from jax import lax
from jax.experimental import pallas as pl
from jax.experimental.pallas import tpu as pltpu

F32 = jnp.float32
BF16 = jnp.bfloat16
I32 = jnp.int32
I16 = jnp.int16

D_MODEL = 2048
BATCH = 4
SEQ = 2048
DEPTH = 4
DEC_BATCH = 8
PAST_LEN = 16384
PAGE_SIZE = 128
N_PAGES = PAST_LEN // PAGE_SIZE
N_HEADS = 8
HEAD_DIM = 128
WIDTH = N_HEADS * HEAD_DIM
IDX_DIM = 64
TOPK = 256
ATTN_SCALE = HEAD_DIM ** -0.5
IDX_WEIGHT_SCALE = (N_HEADS ** -0.5) * (IDX_DIM ** -0.5)
NEG_LARGE = -1e30
NUM_BUCKETS = 32
MAX_DISTANCE = 128
D_FF = ((8 * D_MODEL // 3 + 255) // 256) * 256
EPS = 1e-6
IN_WIDTHS = (WIDTH, WIDTH, WIDTH, N_HEADS * IDX_DIM, IDX_DIM, N_HEADS,
             WIDTH, WIDTH, WIDTH, WIDTH, D_MODEL, D_MODEL)

SUBLANES = 8
LANES = 128
VMEM_LIMIT = 56 * 1024 * 1024

N_PROMPT = BATCH * SEQ
M_ALL = N_PROMPT + LANES
SAMPLE_ROW_BLOCK = N_PROMPT // SUBLANES
SAMPLE_TILE = N_PROMPT // LANES

COL = 512
NP_IN = 24 * COL
CB_Q, CB_K, CB_V, CB_QI, CB_KIW = 0, 2, 4, 6, 7
CB_HQ, CB_HF, CB_HI, CB_HG, CB_GA, CB_GB = 8, 10, 12, 14, 16, 20

TM = M_ALL // 5
TM_DOWN = M_ALL // 10
TM_NORM = M_ALL // 20
TN = 512

TQ = 256
NQB = SEQ // TQ
TK_SUB = 128
INT_MIN = -2 ** 31
HALF16 = 2 ** 15
BIG_IDX = 2 ** 30

SUB = 16
CHUNK = 128
N_SUB = CHUNK // SUB
HPB = 4


def _cparams(sem):
    return pltpu.CompilerParams(dimension_semantics=sem, vmem_limit_bytes=VMEM_LIMIT)


def _sigmoid(x):
    return 1.0 / (1.0 + jnp.exp(-x))


def _silu(x):
    return x * _sigmoid(x)


def _lb_kernel(lb_ref, o_ref):
    x = lb_ref[...]
    m = jnp.max(x, axis=0, keepdims=True)
    e = jnp.exp(x - m)
    p = e / jnp.sum(e, axis=0, keepdims=True)
    acc = jnp.zeros_like(p[0:1])
    rows = []
    for l in range(DEPTH):
        acc = acc + p[l:l + 1]
        rows.append(acc - p[0:1])
    o_ref[...] = jnp.concatenate(rows, axis=0)


def _lower_bounds(hgrn_lb):
    return pl.pallas_call(
        _lb_kernel,
        out_shape=jax.ShapeDtypeStruct((DEPTH, WIDTH), F32),
        name="hgrn_lower_bounds",
    )(hgrn_lb)


def _t5_bucket(n):
    max_exact = NUM_BUCKETS // 2
    nf = jnp.maximum(n, 1).astype(F32)
    large = max_exact + (jnp.log(nf / max_exact) / math.log(MAX_DISTANCE / max_exact)
                         * (NUM_BUCKETS - max_exact)).astype(I32)
    large = jnp.minimum(large, NUM_BUCKETS - 1)
    return jnp.where(n < max_exact, n, large)


def _bias_kernel(rb_ref, o_ref):
    dc = pl.program_id(0)
    s = lax.broadcasted_iota(I32, (TQ, TQ), 0)
    t = lax.broadcasted_iota(I32, (TQ, TQ), 1)
    bucket = _t5_bucket(jnp.maximum(dc * TQ + t - s, 0))
    for h in range(N_HEADS):
        acc = jnp.zeros((TQ, TQ), F32)
        for beta in range(NUM_BUCKETS):
            acc = jnp.where(bucket == beta, rb_ref[beta, h], acc)
        o_ref[0, h] = acc


def _bias_tiles(rel_bias):
    return pl.pallas_call(
        _bias_kernel,
        grid=(3,),
        in_specs=[pl.BlockSpec(memory_space=pltpu.SMEM)],
        out_specs=pl.BlockSpec((1, N_HEADS, TQ, TQ), lambda d: (d, 0, 0, 0)),
        out_shape=jax.ShapeDtypeStruct((3, N_HEADS, TQ, TQ), F32),
        compiler_params=_cparams(("arbitrary",)),
        name="bias_tiles",
    )(rel_bias)


def _rmsnorm_kernel(x_ref, g_ref, o_ref):
    x = x_ref[...]
    ms = jnp.mean(x * x, axis=-1, keepdims=True)
    o_ref[...] = (x * lax.rsqrt(ms + EPS) * g_ref[...]).astype(o_ref.dtype)


def _rmsnorm(x, g):
    m, d = x.shape
    return pl.pallas_call(
        _rmsnorm_kernel,
        grid=(m // TM_NORM,),
        in_specs=[pl.BlockSpec((TM_NORM, d), lambda i: (i, 0)),
                  pl.BlockSpec((1, d), lambda i: (0, 0))],
        out_specs=pl.BlockSpec((TM_NORM, d), lambda i: (i, 0)),
        out_shape=jax.ShapeDtypeStruct((m, d), BF16),
        compiler_params=_cparams(("arbitrary",)),
        name="rmsnorm",
    )(x, g.reshape(1, d))


def _wspec(k, layer):
    return pl.BlockSpec((None, k, TN), lambda i, j: (layer, 0, j))


def _in_proj_kernel(a_ref, wt_ref, o_ref):
    o_ref[...] = lax.dot_general(a_ref[...], wt_ref[...], (((1,), (1,)), ((), ())),
                                 preferred_element_type=F32)


def _in_proj(xn, w_in_p, layer):
    return pl.pallas_call(
        _in_proj_kernel,
        grid=(M_ALL // TM, NP_IN // TN),
        in_specs=[pl.BlockSpec((TM, D_MODEL), lambda i, j: (i, 0)),
                  pl.BlockSpec((None, TN, D_MODEL), lambda i, j: (layer, j, 0))],
        out_specs=pl.BlockSpec((TM, TN), lambda i, j: (i, j)),
        out_shape=jax.ShapeDtypeStruct((M_ALL, NP_IN), F32),
        compiler_params=_cparams(("arbitrary", "arbitrary")),
        name="in_proj",
    )(xn, w_in_p)


def _merge_kernel(oa_ref, ob_ref, wa_ref, wb_ref, ga_ref, gb_ref, o_ref):
    a = jnp.dot(oa_ref[...], wa_ref[...], preferred_element_type=F32)
    b = jnp.dot(ob_ref[...], wb_ref[...], preferred_element_type=F32)
    o_ref[...] = (_sigmoid(ga_ref[...]) * a + _sigmoid(gb_ref[...]) * b).astype(o_ref.dtype)


def _merge(oa, ob, wa, wb, z, layer):
    return pl.pallas_call(
        _merge_kernel,
        grid=(M_ALL // TM, D_MODEL // TN),
        in_specs=[pl.BlockSpec((TM, WIDTH), lambda i, j: (i, 0)),
                  pl.BlockSpec((TM, WIDTH), lambda i, j: (i, 0)),
                  _wspec(WIDTH, layer), _wspec(WIDTH, layer),
                  pl.BlockSpec((TM, TN), lambda i, j: (i, CB_GA + j)),
                  pl.BlockSpec((TM, TN), lambda i, j: (i, CB_GB + j))],
        out_specs=pl.BlockSpec((TM, TN), lambda i, j: (i, j)),
        out_shape=jax.ShapeDtypeStruct((M_ALL, D_MODEL), BF16),
        compiler_params=_cparams(("arbitrary", "arbitrary")),
        name="merge",
    )(oa, ob, wa, wb, z, z)


def _residual_proj_kernel(a_ref, w_ref, x_ref, o_ref):
    o_ref[...] = x_ref[...] + jnp.dot(a_ref[...], w_ref[...], preferred_element_type=F32)


def _residual_proj(a, w, x, layer, tm, name):
    k = a.shape[1]
    return pl.pallas_call(
        _residual_proj_kernel,
        grid=(M_ALL // tm, D_MODEL // TN),
        in_specs=[pl.BlockSpec((tm, k), lambda i, j: (i, 0)),
                  _wspec(k, layer),
                  pl.BlockSpec((tm, TN), lambda i, j: (i, j))],
        out_specs=pl.BlockSpec((tm, TN), lambda i, j: (i, j)),
        out_shape=jax.ShapeDtypeStruct((M_ALL, D_MODEL), F32),
        compiler_params=_cparams(("arbitrary", "arbitrary")),
        name=name,
    )(a, w, x)


def _ffn_up_kernel(a_ref, wg_ref, wu_ref, o_ref):
    a = a_ref[...]
    g = jnp.dot(a, wg_ref[...], preferred_element_type=F32)
    u = jnp.dot(a, wu_ref[...], preferred_element_type=F32)
    o_ref[...] = (_silu(g) * u).astype(o_ref.dtype)


def _ffn_up(hn, wg, wu, layer):
    return pl.pallas_call(
        _ffn_up_kernel,
        grid=(M_ALL // TM, D_FF // TN),
        in_specs=[pl.BlockSpec((TM, D_MODEL), lambda i, j: (i, 0)),
                  _wspec(D_MODEL, layer), _wspec(D_MODEL, layer)],
        out_specs=pl.BlockSpec((TM, TN), lambda i, j: (i, j)),
        out_shape=jax.ShapeDtypeStruct((M_ALL, D_FF), BF16),
        compiler_params=_cparams(("arbitrary", "arbitrary")),
        name="ffn_up",
    )(hn, wg, wu)


def _head_norm(x, g):
    ms = jnp.mean(x * x, axis=-1, keepdims=True)
    return x * lax.rsqrt(ms + EPS) * g


def _kidx_norm(kiw, g_pad):
    lane = lax.broadcasted_iota(I32, kiw.shape, 1)
    ki = jnp.where(lane < IDX_DIM, kiw, 0.0)
    ms = jnp.sum(ki * ki, axis=-1, keepdims=True) * (1.0 / IDX_DIM)
    return ki * lax.rsqrt(ms + EPS) * g_pad


def _post_kernel(zq_ref, zk_ref, zv_ref, zqi_ref, zkiw_ref, gq_ref, gk_ref, gki_ref, *rest):
    qT_ref, qiT_ref, wT_ref, knb_ref, vT_ref, kn_ref, v_ref, ki_ref = rest[-8:]
    gq = gq_ref[...]
    gk = gk_ref[...]
    for h in range(N_HEADS):
        hs = slice(h * HEAD_DIM, (h + 1) * HEAD_DIM)
        qn = _head_norm(zq_ref[:, hs], gq)
        qT_ref[0, hs, :] = qn.T.astype(BF16)
        kn = _head_norm(zk_ref[:, hs], gk)
        kn_ref[:, hs] = kn
        knb_ref[:, hs] = kn.astype(BF16)
        v = zv_ref[:, hs]
        v_ref[:, hs] = v
        vT_ref[0, 0, hs, :] = v.T.astype(BF16)
    for p in range(N_HEADS * IDX_DIM // LANES):
        ps = slice(p * LANES, (p + 1) * LANES)
        qiT_ref[0, ps, :] = zqi_ref[:, ps].T
    kiw = zkiw_ref[:, 0:LANES]
    ki_ref[...] = _kidx_norm(kiw, gki_ref[...])
    wT_ref[0] = (kiw * IDX_WEIGHT_SCALE).T[IDX_DIM:IDX_DIM + N_HEADS, :]


def _post(z, gq, gk, gki_pad, layer, stacked):
    nb = SEQ // LANES
    zspec = lambda cb, w: pl.BlockSpec((LANES, w), lambda r: (r, cb * COL // w))
    gspec = pl.BlockSpec((1, LANES), lambda r: (0, 0))
    in_specs = [zspec(CB_Q, WIDTH), zspec(CB_K, WIDTH), zspec(CB_V, WIDTH),
                zspec(CB_QI, COL), zspec(CB_KIW, COL), gspec, gspec, gspec]
    args = [z, z, z, z, z, gq, gk, gki_pad]
    aliases = {}
    for n, buf in enumerate(stacked):
        aliases[len(args)] = 5 + n
        in_specs.append(pl.BlockSpec(memory_space=pl.ANY))
        args.append(buf)
    return pl.pallas_call(
        _post_kernel,
        grid=(N_PROMPT // LANES,),
        in_specs=in_specs,
        out_specs=[
            pl.BlockSpec((1, WIDTH, LANES), lambda r: (r // nb, 0, r % nb)),
            pl.BlockSpec((1, N_HEADS * IDX_DIM, LANES), lambda r: (r // nb, 0, r % nb)),
            pl.BlockSpec((1, N_HEADS, LANES), lambda r: (r // nb, 0, r % nb)),
            pl.BlockSpec((LANES, WIDTH), lambda r: (r, 0)),
            pl.BlockSpec((1, 1, WIDTH, LANES),
                         lambda r: (r // nb, (r % nb) // (TQ // LANES), 0, r % (TQ // LANES))),
            pl.BlockSpec((None, LANES, WIDTH), lambda r: (layer, r, 0)),
            pl.BlockSpec((None, LANES, WIDTH), lambda r: (layer, r, 0)),
            pl.BlockSpec((None, LANES, LANES), lambda r: (layer, r, 0)),
        ],
        out_shape=[
            jax.ShapeDtypeStruct((BATCH, WIDTH, SEQ), BF16),
            jax.ShapeDtypeStruct((BATCH, N_HEADS * IDX_DIM, SEQ), F32),
            jax.ShapeDtypeStruct((BATCH, N_HEADS, SEQ), F32),
            jax.ShapeDtypeStruct((N_PROMPT, WIDTH), BF16),
            jax.ShapeDtypeStruct((BATCH, NQB, WIDTH, TQ), BF16),
            jax.ShapeDtypeStruct((DEPTH, N_PROMPT, WIDTH), F32),
            jax.ShapeDtypeStruct((DEPTH, N_PROMPT, WIDTH), F32),
            jax.ShapeDtypeStruct((DEPTH, N_PROMPT, LANES), F32),
        ],
        input_output_aliases=aliases,
        compiler_params=_cparams(("arbitrary",)),
        name="post_a",
    )(*args)


def _sortable_key(score):
    score = jnp.where(score == 0.0, 0.0, score)
    bits = pltpu.bitcast(score, I32)
    return jnp.where(bits < 0, bits ^ 0x7FFFFFFF, bits)


def _attn_kernel(qT_ref, qiT_ref, wT_ref, k_ref, vT_ref, ki_ref, bias_ref, _rows_ref, o_ref,
                 keys_ref, hi_ref, lo_ref, eq_ref, sel_ref, acc_ref, p_ref, lg_ref, cut_ref):
    qb = pl.program_id(1)
    nkb = qb + 1
    t_pos = qb * TQ + lax.broadcasted_iota(I32, (1, TQ), 1)
    s_loc = lax.broadcasted_iota(I32, (TQ, TQ), 0)

    qi = qiT_ref[0].astype(BF16)
    w = wT_ref[0]

    def score_blk(kb, carry):
        r0 = pl.multiple_of(kb * TQ, TQ)
        ki = ki_ref[pl.ds(r0, TQ), :][:, 0:IDX_DIM].astype(BF16)
        acc = jnp.zeros((TQ, TQ), F32)
        for h in range(N_HEADS):
            s = jnp.dot(ki, qi[h * IDX_DIM:(h + 1) * IDX_DIM, :], preferred_element_type=F32)
            acc = acc + jnp.maximum(s, 0.0) * w[h:h + 1, :]
        key = jnp.where(kb * TQ + s_loc <= t_pos, _sortable_key(acc), INT_MIN)
        keys_ref[kb] = key
        hi_ref[kb] = (key >> 16).astype(I16)
        lo_ref[kb] = ((key & 0xFFFF) - HALF16).astype(I16)
        return carry

    lax.fori_loop(0, nkb, score_blk, 0)

    @pl.when(nkb % 2 == 1)
    def _():
        keys_ref[nkb] = jnp.full((TQ, TQ), INT_MIN, I32)
        hi_ref[nkb] = jnp.full((TQ, TQ), -HALF16, I16)
        lo_ref[nkb] = jnp.full((TQ, TQ), -HALF16, I16)

    n_pairs = (nkb + 1) // 2

    def count(indicator):
        def body(i, acc):
            for d in range(2):
                kb = 2 * i + d
                x = indicator(keys_ref[kb], kb * TQ + s_loc)
                acc = acc + jnp.sum(x.reshape(TQ // SUBLANES, SUBLANES, TQ), axis=0)
            return acc
        acc = lax.fori_loop(0, n_pairs, body, jnp.zeros((SUBLANES, TQ), F32))
        return jnp.sum(acc, axis=0, keepdims=True)

    def count16(indicator):
        rows = 2 * SUBLANES
        def body(i, acc):
            parts = []
            for d in range(2):
                x = indicator(2 * i + d).reshape(TQ // rows, rows, TQ)
                parts += [x[r] for r in range(TQ // rows)]
            while len(parts) > 1:
                parts = [parts[n] + parts[n + 1] for n in range(0, len(parts), 2)]
            return acc + parts[0]
        acc = lax.fori_loop(0, n_pairs, body, jnp.zeros((rows, TQ), I16))
        return jnp.sum(acc.astype(I32), axis=0, keepdims=True)

    one16 = jnp.int16(1)
    zero16 = jnp.int16(0)

    def bisect16(count_ge, target):
        cnt = count_ge(jnp.zeros((1, TQ), I16))
        ok = cnt >= target
        v0 = jnp.where(ok, 0, -HALF16).astype(I32)

        def step(i, carry):
            v, c_at = carry
            cand = v + jnp.left_shift(jnp.int32(1), 14 - i)
            cnt = count_ge(cand.astype(I16))
            ok = cnt >= target
            return jnp.where(ok, cand, v), jnp.where(ok, cnt, c_at)

        return lax.fori_loop(0, 15, step, (v0, jnp.where(ok, cnt, -1)))

    k_eff = jnp.minimum(TOPK, t_pos + 1)
    tau_hi, _ = bisect16(
        lambda c: count16(lambda kb: jnp.where(hi_ref[kb] >= c, one16, zero16)), k_eff)
    tau_hi16 = tau_hi.astype(I16)
    n_gt = count16(lambda kb: jnp.where(hi_ref[kb] > tau_hi16, one16, zero16))

    def eq_blk(i, carry):
        for d in range(2):
            kb = 2 * i + d
            eq_ref[kb] = jnp.where(hi_ref[kb] == tau_hi16, one16, zero16)
        return carry

    lax.fori_loop(0, n_pairs, eq_blk, 0)
    n_eq_hi = count16(lambda kb: eq_ref[kb])
    need_lo = k_eff - n_gt
    tau_lo, n_ge_lo = bisect16(
        lambda c: count16(lambda kb: jnp.where(lo_ref[kb] >= c, eq_ref[kb], zero16)), need_lo)
    n_ge_lo = jnp.where(n_ge_lo < 0, n_eq_hi, n_ge_lo)
    tau = tau_hi * (2 * HALF16) + (tau_lo + HALF16)

    cut_ref[...] = jnp.full((1, TQ), BIG_IDX, I32)

    @pl.when(jnp.max((n_gt + n_ge_lo - k_eff).astype(F32)) > 0.0)
    def _():
        need = k_eff.astype(F32) - count(lambda key, s: jnp.where(key > tau, 1.0, 0.0))

        def idx_step(i, c):
            trial = c + jnp.left_shift(jnp.int32(1), 10 - i)
            cnt = count(lambda key, s: jnp.where(key == tau, jnp.where(s < trial, 1.0, 0.0), 0.0))
            return jnp.where(cnt < need, trial, c)

        cut_ref[...] = lax.fori_loop(0, 11, idx_step, jnp.zeros((1, TQ), I32))

    cut = cut_ref[...]

    def sel_blk(kb, carry):
        key = keys_ref[kb]
        s_pos = kb * TQ + s_loc
        sel_ref[kb] = jnp.where(
            key > tau, 0.0,
            jnp.where(key == tau, jnp.where(s_pos <= cut, 0.0, NEG_LARGE), NEG_LARGE))
        return carry

    lax.fori_loop(0, nkb, sel_blk, 0)

    n_sub = TQ // TK_SUB

    def logits(h, kb, u):
        hs = slice(h * HEAD_DIM, (h + 1) * HEAD_DIM)
        r0 = pl.multiple_of(kb * TQ + u * TK_SUB, TK_SUB)
        us = slice(u * TK_SUB, (u + 1) * TK_SUB)
        dc = jnp.minimum(qb - kb, 2)
        s = jnp.dot(k_ref[pl.ds(r0, TK_SUB), hs], qT_ref[0, hs, :], preferred_element_type=F32)
        return s * ATTN_SCALE + bias_ref[dc, h, us, :] + sel_ref[kb, us, :]

    def fold8(x):
        return x.reshape(TK_SUB // SUBLANES, SUBLANES, TQ)

    def rows(kb, u):
        return pl.ds(pl.multiple_of(kb * TQ + u * TK_SUB, TK_SUB), TK_SUB)

    def max_blk(kb, ms):
        out = []
        for h in range(N_HEADS):
            m8 = ms[h]
            for u in range(n_sub):
                s = logits(h, kb, u)
                lg_ref[h, rows(kb, u), :] = s
                m8 = jnp.maximum(m8, jnp.max(fold8(s), axis=0))
            out.append(m8)
        return tuple(out)

    ms = lax.fori_loop(0, nkb, max_blk,
                       tuple(jnp.full((SUBLANES, TQ), NEG_LARGE, F32) for _ in range(N_HEADS)))
    m = [jnp.max(m8, axis=0, keepdims=True) for m8 in ms]
    acc_ref[...] = jnp.zeros(acc_ref.shape, F32)

    def pv_blk(kb, ls):
        out = []
        for h in range(N_HEADS):
            l8 = ls[h]
            for u in range(n_sub):
                p = jnp.exp(lg_ref[h, rows(kb, u), :] - m[h])
                l8 = l8 + jnp.sum(fold8(p), axis=0)
                p_ref[h, u * TK_SUB:(u + 1) * TK_SUB, :] = p.astype(BF16)
            out.append(l8)
        for h in range(N_HEADS):
            hs = slice(h * HEAD_DIM, (h + 1) * HEAD_DIM)
            acc_ref[h] += jnp.dot(vT_ref[0, kb, hs, :], p_ref[h], preferred_element_type=F32)
        return tuple(out)

    ls = lax.fori_loop(0, nkb, pv_blk,
                       tuple(jnp.zeros((SUBLANES, TQ), F32) for _ in range(N_HEADS)))
    for h in range(N_HEADS):
        o = acc_ref[h] / jnp.sum(ls[h], axis=0, keepdims=True)
        o_ref[:, h * HEAD_DIM:(h + 1) * HEAD_DIM] = o.T.astype(o_ref.dtype)


def _attn_prompt(qT, qiT, wT, knb, vT, ki_all, bias, layer):
    return pl.pallas_call(
        _attn_kernel,
        grid=(BATCH, NQB),
        in_specs=[
            pl.BlockSpec((1, WIDTH, TQ), lambda b, q: (b, 0, q)),
            pl.BlockSpec((1, N_HEADS * IDX_DIM, TQ), lambda b, q: (b, 0, q)),
            pl.BlockSpec((1, N_HEADS, TQ), lambda b, q: (b, 0, q)),
            pl.BlockSpec((SEQ, WIDTH), lambda b, q: (b, 0)),
            pl.BlockSpec((1, NQB, WIDTH, TQ), lambda b, q: (b, 0, 0, 0)),
            pl.BlockSpec((None, SEQ, LANES), lambda b, q: (layer, b, 0)),
            pl.BlockSpec((3, N_HEADS, TQ, TQ), lambda b, q: (0, 0, 0, 0), pipeline_mode=pl.Buffered(1)),
            pl.BlockSpec(memory_space=pl.ANY),
        ],
        out_specs=pl.BlockSpec((TQ, WIDTH), lambda b, q: (b * NQB + q, 0)),
        out_shape=jax.ShapeDtypeStruct((M_ALL, WIDTH), BF16),
        input_output_aliases={7: 0},
        scratch_shapes=[
            pltpu.VMEM((NQB, TQ, TQ), I32),
            pltpu.VMEM((NQB, TQ, TQ), I16),
            pltpu.VMEM((NQB, TQ, TQ), I16),
            pltpu.VMEM((NQB, TQ, TQ), I16),
            pltpu.VMEM((NQB, TQ, TQ), F32),
            pltpu.VMEM((N_HEADS, HEAD_DIM, TQ), F32),
            pltpu.VMEM((N_HEADS, TQ, TQ), BF16),
            pltpu.VMEM((N_HEADS, SEQ, TQ), F32),
            pltpu.VMEM((1, TQ), I32),
        ],
        compiler_params=_cparams(("arbitrary", "arbitrary")),
        name="attn_prompt",
    )(qT, qiT, wT, knb, vT, ki_all, bias, jnp.zeros((M_ALL, WIDTH), BF16))


def _hgrn_kernel(zq_ref, zf_ref, zi_ref, zg_ref, lb_ref, og_ref, _rows_ref, ob_ref, st_ref,
                 ST_ref, b_ref, k_ref, q_ref, i_ref, oi_ref):
    ST_ref[...] = jnp.zeros(ST_ref.shape, F32)
    og = og_ref[...]
    row = lax.broadcasted_iota(I32, (CHUNK, LANES), 0)
    lane = lax.broadcasted_iota(I32, (CHUNK, LANES), 1)
    sub_row = lax.broadcasted_iota(I32, (SUBLANES, LANES), 0)

    def chunk(c, carry):
        r0 = pl.multiple_of(c * CHUNK, CHUNK)
        b_last = []
        for hh in range(HPB):
            hs = slice(hh * LANES, (hh + 1) * LANES)
            lb = lb_ref[hh]
            f = lb + (1.0 - lb) * _sigmoid(zf_ref[pl.ds(r0, CHUNK), hs])
            b = jnp.log(f)
            for d in (1, 2, 4, 8):
                b = b + jnp.where(row % SUB >= d, pltpu.roll(b, d, axis=0), 0.0)
            b_ref[hh] = b
            k_ref[hh] = 1.0 - f
            q_ref[hh] = _silu(zq_ref[pl.ds(r0, CHUNK), hs])
            i_ref[hh] = zi_ref[pl.ds(r0, CHUNK), hs]
            b_last.append(b_ref[hh, pl.ds(SUB - 1, N_SUB, stride=SUB), :])

        def sub_block(j, carry2):
            j0 = pl.multiple_of(j * SUB, SUB)
            for hh in range(HPB):
                hs = slice(hh * LANES, (hh + 1) * LANES)
                b_lo = b_ref[hh, pl.ds(j0, SUBLANES), :]
                b_hi = b_ref[hh, pl.ds(j0 + SUBLANES, SUBLANES), :]
                q_lo = q_ref[hh, pl.ds(j0, SUBLANES), :]
                q_hi = q_ref[hh, pl.ds(j0 + SUBLANES, SUBLANES), :]
                o_lo = jnp.zeros((SUBLANES, LANES), F32)
                o_hi = jnp.zeros((SUBLANES, LANES), F32)
                for s in range(SUB):
                    bs = jnp.broadcast_to(b_ref[hh, pl.ds(j0 + s, 1), :], (SUBLANES, LANES))
                    ks = jnp.broadcast_to(k_ref[hh, pl.ds(j0 + s, 1), :], (SUBLANES, LANES))
                    iv = jnp.broadcast_to(i_ref[hh, pl.ds(j0 + s, 1), :], (SUBLANES, LANES))
                    if s < SUBLANES:
                        ok = sub_row >= s
                        x = jnp.where(ok, q_lo * ks * jnp.exp(jnp.where(ok, b_lo - bs, 0.0)), 0.0)
                        o_lo = o_lo + jnp.sum(x, axis=-1, keepdims=True) * iv
                        x = q_hi * ks * jnp.exp(b_hi - bs)
                        o_hi = o_hi + jnp.sum(x, axis=-1, keepdims=True) * iv
                    else:
                        ok = sub_row >= s - SUBLANES
                        x = jnp.where(ok, q_hi * ks * jnp.exp(jnp.where(ok, b_hi - bs, 0.0)), 0.0)
                        o_hi = o_hi + jnp.sum(x, axis=-1, keepdims=True) * iv
                oi_ref[hh, pl.ds(j0, SUBLANES), :] = o_lo
                oi_ref[hh, pl.ds(j0 + SUBLANES, SUBLANES), :] = o_hi
            return carry2

        lax.fori_loop(0, N_SUB, sub_block, 0)

        for hh in range(HPB):
            hs = slice(hh * LANES, (hh + 1) * LANES)
            b = b_ref[hh]
            bl_b = jnp.concatenate(
                [jnp.broadcast_to(b_last[hh][j:j + 1, :], (SUB, LANES)) for j in range(N_SUB)], axis=0)
            q_dec = (q_ref[hh] * jnp.exp(b)).astype(BF16)
            k_end = (k_ref[hh] * jnp.exp(bl_b - b)).astype(BF16)
            decay = jnp.exp(b_last[hh])
            iT = i_ref[hh].T
            it_stack = jnp.concatenate(
                [jnp.where(lane // SUB == j, iT, 0.0).astype(BF16) for j in range(N_SUB)], axis=0)
            pT = jnp.dot(it_stack, k_end, preferred_element_type=F32)
            r = ST_ref[hh]
            r_list = []
            for j in range(N_SUB):
                r_list.append(r.astype(BF16))
                r = r * decay[j:j + 1, :] + pT[j * LANES:(j + 1) * LANES, :]
            ST_ref[hh] = r
            r_stack = jnp.concatenate(r_list, axis=0)
            oT_all = lax.dot_general(r_stack, q_dec, (((1,), (1,)), ((), ())),
                                     preferred_element_type=F32)
            oT = jnp.zeros((LANES, CHUNK), F32)
            for j in range(N_SUB):
                oT = oT + jnp.where(lane // SUB == j, oT_all[j * LANES:(j + 1) * LANES, :], 0.0)
            o = oT.T + oi_ref[hh]
            zg = zg_ref[pl.ds(r0, CHUNK), hs]
            ob_ref[pl.ds(r0, CHUNK), hs] = (_head_norm(o, og) * _silu(zg)).astype(ob_ref.dtype)
        return carry

    lax.fori_loop(0, SEQ // CHUNK, chunk, 0)
    for hh in range(HPB):
        st_ref[hh] = ST_ref[hh].T


def _hgrn_prompt(z, lb3, og):
    w = HPB * LANES
    zspec = lambda cb: pl.BlockSpec((SEQ, w), lambda b, h: (b, cb * COL // w + h))
    return pl.pallas_call(
        _hgrn_kernel,
        grid=(BATCH, N_HEADS // HPB),
        in_specs=[zspec(CB_HQ), zspec(CB_HF), zspec(CB_HI), zspec(CB_HG),
                  pl.BlockSpec((HPB, 1, LANES), lambda b, h: (h, 0, 0)),
                  pl.BlockSpec((1, LANES), lambda b, h: (0, 0)),
                  pl.BlockSpec(memory_space=pl.ANY)],
        out_specs=[pl.BlockSpec((SEQ, w), lambda b, h: (b, h)),
                   pl.BlockSpec((None, HPB, LANES, LANES), lambda b, h: (b, h, 0, 0))],
        out_shape=[jax.ShapeDtypeStruct((M_ALL, WIDTH), BF16),
                   jax.ShapeDtypeStruct((BATCH, N_HEADS, LANES, LANES), F32)],
        input_output_aliases={6: 0},
        scratch_shapes=[pltpu.VMEM((HPB, LANES, LANES), F32),
                        pltpu.VMEM((HPB, CHUNK, LANES), F32),
                        pltpu.VMEM((HPB, CHUNK, LANES), F32),
                        pltpu.VMEM((HPB, CHUNK, LANES), F32),
                        pltpu.VMEM((HPB, CHUNK, LANES), F32),
                        pltpu.VMEM((HPB, CHUNK, LANES), F32)],
        compiler_params=_cparams(("arbitrary", "arbitrary")),
        name="hgrn_prompt",
    )(z, z, z, z, lb3, og, jnp.zeros((M_ALL, WIDTH), BF16))


def _rows_to_cols(x):
    pad = jnp.zeros((LANES - x.shape[0], LANES), F32)
    return jnp.concatenate([x, pad], axis=0).T


def _hgrn_sample_kernel(zq_ref, zf_ref, zi_ref, zg_ref, lb_ref, og_ref, s0_ref, ob_any_ref,
                        ob_ref, s1_ref):
    del ob_any_ref
    lb = lb_ref[...]
    f = lb + (1.0 - lb) * _sigmoid(zf_ref[...])
    fT = _rows_to_cols(f)
    kT = _rows_to_cols(1.0 - f)
    qT = _rows_to_cols(_silu(zq_ref[...]))
    iv = zi_ref[...]
    rows = []
    for b in range(DEC_BATCH):
        s1 = fT[:, b:b + 1] * s0_ref[b] + kT[:, b:b + 1] * iv[b:b + 1, :]
        s1_ref[b] = s1
        rows.append(jnp.sum(qT[:, b:b + 1] * s1, axis=0, keepdims=True))
    o = jnp.concatenate(rows, axis=0)
    o = _head_norm(o, og_ref[...]) * _silu(zg_ref[...])
    pad = jnp.zeros((LANES - DEC_BATCH, LANES), F32)
    ob_ref[...] = jnp.concatenate([o, pad], axis=0).astype(ob_ref.dtype)


def _hgrn_sample(z, lb3, og, state_hgrn, ob_all, layer):
    cpb = COL // LANES
    zspec = lambda cb: pl.BlockSpec((DEC_BATCH, LANES), lambda h: (SAMPLE_ROW_BLOCK, cb * cpb + h))
    return pl.pallas_call(
        _hgrn_sample_kernel,
        grid=(N_HEADS,),
        in_specs=[zspec(CB_HQ), zspec(CB_HF), zspec(CB_HI), zspec(CB_HG),
                  pl.BlockSpec((None, 1, LANES), lambda h: (h, 0, 0)),
                  pl.BlockSpec((1, LANES), lambda h: (0, 0)),
                  pl.BlockSpec((None, DEC_BATCH, None, LANES, LANES), lambda h: (layer, 0, h, 0, 0)),
                  pl.BlockSpec(memory_space=pl.ANY)],
        out_specs=[pl.BlockSpec((LANES, LANES), lambda h: (SAMPLE_TILE, h)),
                   pl.BlockSpec((DEC_BATCH, None, LANES, LANES), lambda h: (0, h, 0, 0))],
        out_shape=[jax.ShapeDtypeStruct((M_ALL, WIDTH), BF16),
                   jax.ShapeDtypeStruct((DEC_BATCH, N_HEADS, LANES, LANES), F32)],
        input_output_aliases={7: 0},
        compiler_params=_cparams(("arbitrary",)),
        name="hgrn_sample",
    )(z, z, z, z, lb3, og, state_hgrn, ob_all)


def _sample_norm_kernel(zq_ref, zk_ref, zkiw_ref, gq_ref, gk_ref, gki_ref, qn_ref, kn_ref, kiw_ref):
    gq = gq_ref[...]
    gk = gk_ref[...]
    for h in range(N_HEADS):
        hs = slice(h * HEAD_DIM, (h + 1) * HEAD_DIM)
        qn_ref[:, hs] = _head_norm(zq_ref[:, hs], gq)
        kn_ref[:, hs] = _head_norm(zk_ref[:, hs], gk)
    kiw = zkiw_ref[:, 0:LANES]
    lane = lax.broadcasted_iota(I32, kiw.shape, 1)
    kiw_ref[...] = _kidx_norm(kiw, gki_ref[...]) + jnp.where(
        (lane >= IDX_DIM) & (lane < IDX_DIM + N_HEADS), kiw * IDX_WEIGHT_SCALE, 0.0)


def _sample_norms(z, gq, gk, gki_pad):
    zspec = lambda cb, w: pl.BlockSpec((DEC_BATCH, w), lambda i: (SAMPLE_ROW_BLOCK, cb * COL // w))
    gspec = pl.BlockSpec((1, LANES), lambda i: (0, 0))
    return pl.pallas_call(
        _sample_norm_kernel,
        grid=(1,),
        in_specs=[zspec(CB_Q, WIDTH), zspec(CB_K, WIDTH), zspec(CB_KIW, COL), gspec, gspec, gspec],
        out_specs=[pl.BlockSpec((DEC_BATCH, WIDTH), lambda i: (0, 0)),
                   pl.BlockSpec((DEC_BATCH, WIDTH), lambda i: (0, 0)),
                   pl.BlockSpec((DEC_BATCH, LANES), lambda i: (0, 0))],
        out_shape=[jax.ShapeDtypeStruct((DEC_BATCH, WIDTH), F32),
                   jax.ShapeDtypeStruct((DEC_BATCH, WIDTH), F32),
                   jax.ShapeDtypeStruct((DEC_BATCH, LANES), F32)],
        compiler_params=_cparams(("arbitrary",)),
        name="sample_norms",
    )(z, z, z, gq, gk, gki_pad)


KEY_ROWS = N_PAGES + SUBLANES


def _select_row(x, b):
    sub = lax.broadcasted_iota(I32, x.shape, 0)
    return jnp.sum(jnp.where(sub == b, x, 0.0), axis=0, keepdims=True)


def _sample_topk_kernel(pt_ref, zqi_ref, kiw_ref, cache_ref, sel_ref, buf_ref, sem_ref, key_ref,
                        pos_ref, cut_ref, *, layer):
    b = pl.program_id(0)

    def page_copy(p):
        return pltpu.make_async_copy(cache_ref.at[layer, pt_ref[b, p]], buf_ref.at[p], sem_ref.at[0])

    def start(p, c):
        page_copy(p).start()
        return c

    lax.fori_loop(0, N_PAGES, start, 0)

    qrow = _select_row(zqi_ref[...], b)
    head = lax.broadcasted_iota(I32, (N_HEADS, N_HEADS * IDX_DIM), 0)
    col = lax.broadcasted_iota(I32, (N_HEADS, N_HEADS * IDX_DIM), 1)
    qm = jnp.where(col // IDX_DIM == head, jnp.broadcast_to(qrow, head.shape), 0.0)
    q128 = qm[:, 0:LANES]
    for p in range(1, N_HEADS * IDX_DIM // LANES):
        q128 = q128 + qm[:, p * LANES:(p + 1) * LANES]
    lane8 = lax.broadcasted_iota(I32, (N_HEADS, LANES), 1)
    q8 = jnp.where(lane8 < IDX_DIM, q128 + pltpu.roll(q128, IDX_DIM, axis=1), 0.0)
    qT = _rows_to_cols(q8)[0:IDX_DIM, :]

    kiw_row = _select_row(kiw_ref[...], b)
    r8 = lax.broadcasted_iota(I32, (N_HEADS, LANES), 0)
    wcol = jnp.sum(jnp.where(lane8 == IDX_DIM + r8, jnp.broadcast_to(kiw_row, (N_HEADS, LANES)), 0.0),
                   axis=-1, keepdims=True)
    knew = jnp.where(lane8[0:1] < IDX_DIM, kiw_row, 0.0)
    sn = jnp.sum(q8 * knew, axis=-1, keepdims=True)
    s_new = jnp.sum(jnp.maximum(sn, 0.0) * wcol, axis=0, keepdims=True)

    def wait(p, c):
        page_copy(p).wait()
        return c

    lax.fori_loop(0, N_PAGES, wait, 0)

    qcols = [jnp.broadcast_to(qT[:, h:h + 1], (IDX_DIM, PAGE_SIZE))[None] for h in range(N_HEADS)]

    def page_group(g, carry):
        r0 = pl.multiple_of(g * SUBLANES, SUBLANES)
        pages = buf_ref[pl.ds(r0, SUBLANES)]
        s = jnp.zeros((SUBLANES, PAGE_SIZE), F32)
        for h in range(N_HEADS):
            sh = jnp.sum(pages * qcols[h], axis=1)
            s = s + jnp.maximum(sh, 0.0) * wcol[h:h + 1, :]
        key_ref[b, pl.ds(r0, SUBLANES), :] = _sortable_key(s)
        return carry

    lax.fori_loop(0, N_PAGES // SUBLANES, page_group, 0)
    row_t = lax.broadcasted_iota(I32, (SUBLANES, LANES), 0)
    lane_t = lax.broadcasted_iota(I32, (SUBLANES, LANES), 1)
    key_ref[b, pl.ds(N_PAGES, SUBLANES), :] = jnp.where(
        (row_t == 0) & (lane_t == 0),
        _sortable_key(jnp.broadcast_to(s_new, (SUBLANES, LANES))), INT_MIN)

    @pl.when(b == DEC_BATCH - 1)
    def _():
        shape = (DEC_BATCH, KEY_ROWS, LANES)
        pos = lax.broadcasted_iota(I32, shape, 1) * LANES + lax.broadcasted_iota(I32, shape, 2)

        def count(indicator):
            x = indicator(key_ref[...])
            return jnp.sum(jnp.sum(x, axis=1), axis=1, keepdims=True)

        cnt = count(lambda key: jnp.where(key >= 0, 1.0, 0.0))
        tau0 = jnp.where(cnt >= TOPK, 0, INT_MIN).astype(I32)

        def bit_step(i, tau):
            cand = tau + jnp.left_shift(jnp.int32(1), 30 - i)
            cnt = count(lambda key: jnp.where(key >= cand[:, :, None], 1.0, 0.0))
            return jnp.where(cnt >= TOPK, cand, tau)

        tau = lax.fori_loop(0, 31, bit_step, tau0)[:, :, None]

        need = TOPK - count(lambda key: jnp.where(key > tau, 1.0, 0.0))
        n_eq = count(lambda key: jnp.where(key == tau, 1.0, 0.0))
        cut_ref[...] = jnp.full(cut_ref.shape, BIG_IDX, I32)

        @pl.when(jnp.max(n_eq - need) > 0.0)
        def _():
            def idx_step(i, c):
                trial = c + jnp.left_shift(jnp.int32(1), 14 - i)
                cnt = count(lambda key: jnp.where(key == tau, jnp.where(pos < trial[:, :, None], 1.0, 0.0), 0.0))
                return jnp.where(cnt < need, trial, c)
            c = lax.fori_loop(0, 15, idx_step, jnp.zeros((DEC_BATCH, 1), I32))
            cut_ref[...] = jnp.broadcast_to(c, cut_ref.shape)

        cut = cut_ref[:, 0:1][:, :, None]
        key = key_ref[...]
        chosen = jnp.where(key > tau, pos,
                           jnp.where(key == tau, jnp.where(pos <= cut, pos, BIG_IDX), BIG_IDX))
        pos_ref[...] = chosen.astype(F32)

        out_lane = lax.broadcasted_iota(I32, (DEC_BATCH, TOPK), 1)

        def pick(it, carry):
            sel, last = carry
            p = pos_ref[...]
            first = jnp.min(jnp.min(jnp.where(p > last[:, :, None], p, float(BIG_IDX)), axis=1),
                            axis=1, keepdims=True)
            return jnp.where(out_lane == it, first.astype(I32), sel), first

        sel, _ = lax.fori_loop(0, TOPK, pick, (jnp.zeros((DEC_BATCH, TOPK), I32),
                                               jnp.full((DEC_BATCH, 1), -1.0, F32)))
        sel_ref[...] = sel


def _sample_topk(page_table, z, kiw_s, cache_kidx, layer):
    grid_spec = pltpu.PrefetchScalarGridSpec(
        num_scalar_prefetch=1,
        grid=(DEC_BATCH,),
        in_specs=[pl.BlockSpec((DEC_BATCH, COL), lambda b, pt: (SAMPLE_ROW_BLOCK, CB_QI)),
                  pl.BlockSpec((DEC_BATCH, LANES), lambda b, pt: (0, 0)),
                  pl.BlockSpec(memory_space=pl.ANY)],
        out_specs=pl.BlockSpec((DEC_BATCH, TOPK), lambda b, pt: (0, 0)),
        scratch_shapes=[pltpu.VMEM((N_PAGES, IDX_DIM, PAGE_SIZE), F32),
                        pltpu.SemaphoreType.DMA((1,)),
                        pltpu.VMEM((DEC_BATCH, KEY_ROWS, LANES), I32),
                        pltpu.VMEM((DEC_BATCH, KEY_ROWS, LANES), F32),
                        pltpu.VMEM((DEC_BATCH, LANES), I32)],
    )
    cache_t = jnp.swapaxes(cache_kidx, 2, 3)
    return pl.pallas_call(
        functools.partial(_sample_topk_kernel, layer=layer),
        grid_spec=grid_spec,
        out_shape=jax.ShapeDtypeStruct((DEC_BATCH, TOPK), I32),
        compiler_params=_cparams(("arbitrary",)),
        name="sample_topk",
    )(page_table, z, kiw_s, cache_t)


def _sample_attn_kernel(sel_s_ref, pt_ref, sel_ref, qn_ref, kn_ref, zv_ref, rbT_ref,
                        ck_ref, cv_ref, oa_any_ref, o_ref, kbuf_ref, vbuf_ref, sem_ref, rows_ref,
                        *, layer):
    del oa_any_ref
    b = pl.program_id(0)

    def copies(j):
        sp = jnp.minimum(sel_s_ref[b, j], PAST_LEN - 1)
        page = pt_ref[b, sp // PAGE_SIZE]
        off = sp % PAGE_SIZE
        dst = pl.ds(pl.multiple_of(j * N_HEADS, N_HEADS), N_HEADS)
        return (pltpu.make_async_copy(ck_ref.at[layer, page, off], kbuf_ref.at[dst], sem_ref.at[0]),
                pltpu.make_async_copy(cv_ref.at[layer, page, off], vbuf_ref.at[dst], sem_ref.at[1]))

    def start(j, c):
        ck, cv = copies(j)
        ck.start()
        cv.start()
        return c

    lax.fori_loop(0, TOPK, start, 0)

    def heads_on_rows(x):
        row = _select_row(x, b)
        return jnp.concatenate([row[:, h * HEAD_DIM:(h + 1) * HEAD_DIM] for h in range(N_HEADS)],
                               axis=0)

    q = heads_on_rows(qn_ref[...])
    k_new = heads_on_rows(kn_ref[...])
    v_new = heads_on_rows(zv_ref[...])
    sel_b = jnp.sum(jnp.where(lax.broadcasted_iota(I32, (DEC_BATCH, TOPK), 0) == b, sel_ref[...], 0),
                    axis=0, keepdims=True)
    is_new = sel_b >= PAST_LEN
    bucket = _t5_bucket(jnp.maximum(PAST_LEN - sel_b, 0))
    bias = jnp.zeros((N_HEADS, TOPK), F32)
    for beta in range(NUM_BUCKETS):
        bias = jnp.where(bucket == beta, rbT_ref[:, beta:beta + 1], bias)

    def wait(j, c):
        ck, cv = copies(j)
        ck.wait()
        cv.wait()
        return c

    lax.fori_loop(0, TOPK, wait, 0)

    nt = (((1,), (1,)), ((), ()))
    hrow = lax.broadcasted_iota(I32, (N_HEADS, TOPK), 0)
    qb16 = q.astype(BF16)
    logits = jnp.zeros((N_HEADS, TOPK), F32)
    for h in range(N_HEADS):
        kh = kbuf_ref[pl.ds(h, TOPK, stride=N_HEADS), :].astype(BF16)
        res = lax.dot_general(qb16, kh, nt, preferred_element_type=F32)
        logits = jnp.where(hrow == h, res, logits)
    logit_new = jnp.sum(qb16.astype(F32) * k_new.astype(BF16).astype(F32), axis=-1, keepdims=True)
    logits = jnp.where(is_new, logit_new, logits) * ATTN_SCALE + bias
    m = jnp.max(logits, axis=1, keepdims=True)
    e = jnp.exp(logits - m)
    p = e / jnp.sum(e, axis=1, keepdims=True)
    p_new = jnp.sum(jnp.where(is_new, p, 0.0), axis=1, keepdims=True)
    p_mm = jnp.where(is_new, 0.0, p).astype(BF16)
    hrow_o = lax.broadcasted_iota(I32, (N_HEADS, HEAD_DIM), 0)
    o = jnp.zeros((N_HEADS, HEAD_DIM), F32)
    for h in range(N_HEADS):
        vh = vbuf_ref[pl.ds(h, TOPK, stride=N_HEADS), :].astype(BF16)
        res = jnp.dot(p_mm, vh, preferred_element_type=F32)
        o = jnp.where(hrow_o == h, res, o)
    o = o + p_new.astype(BF16).astype(F32) * v_new.astype(BF16).astype(F32)
    orow = jnp.concatenate([o[h:h + 1, :] for h in range(N_HEADS)], axis=1)

    @pl.when(b == 0)
    def _():
        rows_ref[...] = jnp.zeros(rows_ref.shape, F32)

    sub = lax.broadcasted_iota(I32, (DEC_BATCH, WIDTH), 0)
    rows_ref[...] = jnp.where(sub == b, jnp.broadcast_to(orow, (DEC_BATCH, WIDTH)), rows_ref[...])

    @pl.when(b == DEC_BATCH - 1)
    def _():
        pad = jnp.zeros((LANES - DEC_BATCH, WIDTH), F32)
        o_ref[...] = jnp.concatenate([rows_ref[...], pad], axis=0).astype(o_ref.dtype)


def _sample_attn(sel, page_table, qn_s, kn_s, z, rbT, cache_k, cache_v, oa_all, layer):
    grid_spec = pltpu.PrefetchScalarGridSpec(
        num_scalar_prefetch=2,
        grid=(DEC_BATCH,),
        in_specs=[pl.BlockSpec((DEC_BATCH, TOPK), lambda b, s, pt: (0, 0)),
                  pl.BlockSpec((DEC_BATCH, WIDTH), lambda b, s, pt: (0, 0)),
                  pl.BlockSpec((DEC_BATCH, WIDTH), lambda b, s, pt: (0, 0)),
                  pl.BlockSpec((DEC_BATCH, WIDTH), lambda b, s, pt: (SAMPLE_ROW_BLOCK, CB_V * COL // WIDTH)),
                  pl.BlockSpec((N_HEADS, NUM_BUCKETS), lambda b, s, pt: (0, 0)),
                  pl.BlockSpec(memory_space=pl.ANY),
                  pl.BlockSpec(memory_space=pl.ANY),
                  pl.BlockSpec(memory_space=pl.ANY)],
        out_specs=pl.BlockSpec((LANES, WIDTH), lambda b, s, pt: (SAMPLE_TILE, 0)),
        scratch_shapes=[pltpu.VMEM((TOPK * N_HEADS, HEAD_DIM), F32),
                        pltpu.VMEM((TOPK * N_HEADS, HEAD_DIM), F32),
                        pltpu.SemaphoreType.DMA((2,)),
                        pltpu.VMEM((DEC_BATCH, WIDTH), F32)],
    )
    return pl.pallas_call(
        functools.partial(_sample_attn_kernel, layer=layer),
        grid_spec=grid_spec,
        out_shape=jax.ShapeDtypeStruct((M_ALL, WIDTH), BF16),
        input_output_aliases={9: 0},
        compiler_params=_cparams(("arbitrary",)),
        name="sample_attn",
    )(sel, page_table, sel, qn_s, kn_s, z, rbT, cache_k, cache_v, oa_all)


def _permute_w_in(w_in):
    wt = jnp.swapaxes(w_in, 1, 2)
    parts = []
    off = 0
    for w in IN_WIDTHS:
        parts.append(wt[:, off:off + w])
        off += w
    q, k, v, qi, ki, wi, hq, hf, hi, hg, ga, gb = parts
    pad = jnp.zeros((w_in.shape[0], COL - IDX_DIM - N_HEADS, w_in.shape[1]), w_in.dtype)
    return jnp.concatenate([q, k, v, qi, ki, wi, pad, hq, hf, hi, hg, ga, gb], axis=1).astype(BF16)


def _stack_rows(prompt_rows, sample_rows):
    pad = jnp.zeros((M_ALL - N_PROMPT - DEC_BATCH, prompt_rows.shape[1]), prompt_rows.dtype)
    return jnp.concatenate([prompt_rows, sample_rows.astype(prompt_rows.dtype), pad], axis=0)


def kernel(x_prompt, x_sample, cache_k, cache_v, cache_kidx, state_hgrn, page_table, norm_mix_g, w_in,
           q_norm_g, k_norm_g, kidx_norm_g, rel_bias, hgrn_lb, hgrn_out_g, w_up_a, w_up_b, w_out,
           norm_ffn_g, w_ffn_gate, w_ffn_up, w_ffn_down):
    w_in_p = _permute_w_in(w_in)
    w_up_a, w_up_b, w_out = (w.astype(BF16) for w in (w_up_a, w_up_b, w_out))
    w_ffn_gate, w_ffn_up, w_ffn_down = (w.astype(BF16) for w in (w_ffn_gate, w_ffn_up, w_ffn_down))
    lower = _lower_bounds(hgrn_lb)
    bias = _bias_tiles(rel_bias)
    rbT = rel_bias.T
    gki_pad = jnp.pad(kidx_norm_g, ((0, 0), (0, LANES - IDX_DIM)))

    x = _stack_rows(x_prompt.reshape(N_PROMPT, D_MODEL), x_sample.reshape(DEC_BATCH, D_MODEL))
    stacked = (jnp.zeros((DEPTH, N_PROMPT, WIDTH), F32), jnp.zeros((DEPTH, N_PROMPT, WIDTH), F32),
               jnp.zeros((DEPTH, N_PROMPT, LANES), F32))
    outs = [[] for _ in range(5)]
    for l in range(DEPTH):
        gq = q_norm_g[l].reshape(1, HEAD_DIM)
        gk = k_norm_g[l].reshape(1, HEAD_DIM)
        gki = gki_pad[l].reshape(1, LANES)
        og = hgrn_out_g[l].reshape(1, LANES)
        lb3 = lower[l].reshape(N_HEADS, 1, LANES)

        xn = _rmsnorm(x, norm_mix_g[l])
        z = _in_proj(xn, w_in_p, l)

        qT, qiT, wT, knb, vT, *stacked = _post(z, gq, gk, gki, l, stacked)
        oa = _attn_prompt(qT, qiT, wT, knb, vT, stacked[2], bias, l)
        qn_s, kn_s, kiw_s = _sample_norms(z, gq, gk, gki)
        sel = _sample_topk(page_table, z, kiw_s, cache_kidx, l)
        oa = _sample_attn(sel, page_table, qn_s, kn_s, z, rbT, cache_k, cache_v, oa, l)

        ob, st_p = _hgrn_prompt(z, lb3, og)
        ob, st_s = _hgrn_sample(z, lb3, og, state_hgrn, ob, l)

        mixed = _merge(oa, ob, w_up_a, w_up_b, z, l)
        x = _residual_proj(mixed, w_out, x, l, TM, "out_proj")
        hn = _rmsnorm(x, norm_ffn_g[l])
        x = _residual_proj(_ffn_up(hn, w_ffn_gate, w_ffn_up, l), w_ffn_down, x, l, TM_DOWN, "ffn_down")

        v_cols = slice(CB_V * COL, CB_V * COL + WIDTH)
        outs[0].append(st_p)
        outs[1].append(kn_s.reshape(DEC_BATCH, 1, N_HEADS, HEAD_DIM))
        outs[2].append(z[N_PROMPT:N_PROMPT + DEC_BATCH, v_cols].reshape(DEC_BATCH, 1, N_HEADS, HEAD_DIM))
        outs[3].append(kiw_s[:, :IDX_DIM].reshape(DEC_BATCH, 1, IDX_DIM))
        outs[4].append(st_s)

    k_all, v_all, ki_all = stacked
    y_prompt = x[:N_PROMPT].reshape(BATCH, SEQ, D_MODEL)
    y_sample = x[N_PROMPT:N_PROMPT + DEC_BATCH].reshape(DEC_BATCH, 1, D_MODEL)
    st_p, k_s, v_s, ki_s, st_s = (jnp.stack(o) for o in outs)
    return (y_prompt, y_sample,
            k_all.reshape(DEPTH, BATCH, SEQ, N_HEADS, HEAD_DIM),
            v_all.reshape(DEPTH, BATCH, SEQ, N_HEADS, HEAD_DIM),
            ki_all[:, :, :IDX_DIM].reshape(DEPTH, BATCH, SEQ, IDX_DIM),
            st_p, k_s, v_s, ki_s, st_s)
```

```python
import functools
import math

import jax
import jax.numpy as jnp
from jax import lax
from jax.experimental import pallas as pl
from jax.experimental.pallas import tpu as pltpu

F32 = jnp.float32
BF16 = jnp.bfloat16
I32 = jnp.int32
I16 = jnp.int16

D_MODEL = 2048
BATCH = 4
SEQ = 2048
DEPTH = 4
DEC_BATCH = 8
PAST_LEN = 16384
PAGE_SIZE = 128
N_PAGES = PAST_LEN // PAGE_SIZE
N_HEADS = 8
HEAD_DIM = 128
WIDTH = N_HEADS * HEAD_DIM
IDX_DIM = 64
TOPK = 256
ATTN_SCALE = HEAD_DIM ** -0.5
IDX_WEIGHT_SCALE = (N_HEADS ** -0.5) * (IDX_DIM ** -0.5)
NEG_LARGE = -1e30
NUM_BUCKETS = 32
MAX_DISTANCE = 128
D_FF = ((8 * D_MODEL // 3 + 255) // 256) * 256
EPS = 1e-6
IN_WIDTHS = (WIDTH, WIDTH, WIDTH, N_HEADS * IDX_DIM, IDX_DIM, N_HEADS,
             WIDTH, WIDTH, WIDTH, WIDTH, D_MODEL, D_MODEL)

SUBLANES = 8
LANES = 128
VMEM_LIMIT = 56 * 1024 * 1024

N_PROMPT = BATCH * SEQ
M_ALL = N_PROMPT + LANES
SAMPLE_ROW_BLOCK = N_PROMPT // SUBLANES
SAMPLE_TILE = N_PROMPT // LANES

COL = 512
NP_IN = 24 * COL
CB_Q, CB_K, CB_V, CB_QI, CB_KIW = 0, 2, 4, 6, 7
CB_HQ, CB_HF, CB_HI, CB_HG, CB_GA, CB_GB = 8, 10, 12, 14, 16, 20

TM = M_ALL // 5
TM_DOWN = M_ALL // 10
TM_NORM = M_ALL // 20
TN = 512

TQ = 256
NQB = SEQ // TQ
TK_SUB = 128
INT_MIN = -2 ** 31
HALF16 = 2 ** 15
BIG_IDX = 2 ** 30

SUB = 16
CHUNK = 128
N_SUB = CHUNK // SUB
HPB = 4


def _cparams(sem):
    return pltpu.CompilerParams(dimension_semantics=sem, vmem_limit_bytes=VMEM_LIMIT)


def _sigmoid(x):
    return 1.0 / (1.0 + jnp.exp(-x))


def _silu(x):
    return x * _sigmoid(x)


def _lb_kernel(lb_ref, o_ref):
    x = lb_ref[...]
    m = jnp.max(x, axis=0, keepdims=True)
    e = jnp.exp(x - m)
    p = e / jnp.sum(e, axis=0, keepdims=True)
    acc = jnp.zeros_like(p[0:1])
    rows = []
    for l in range(DEPTH):
        acc = acc + p[l:l + 1]
        rows.append(acc - p[0:1])
    o_ref[...] = jnp.concatenate(rows, axis=0)


def _lower_bounds(hgrn_lb):
    return pl.pallas_call(
        _lb_kernel,
        out_shape=jax.ShapeDtypeStruct((DEPTH, WIDTH), F32),
        name="hgrn_lower_bounds",
    )(hgrn_lb)


def _t5_bucket(n):
    max_exact = NUM_BUCKETS // 2
    nf = jnp.maximum(n, 1).astype(F32)
    large = max_exact + (jnp.log(nf / max_exact) / math.log(MAX_DISTANCE / max_exact)
                         * (NUM_BUCKETS - max_exact)).astype(I32)
    large = jnp.minimum(large, NUM_BUCKETS - 1)
    return jnp.where(n < max_exact, n, large)


def _bias_kernel(rb_ref, o_ref):
    dc = pl.program_id(0)
    s = lax.broadcasted_iota(I32, (TQ, TQ), 0)
    t = lax.broadcasted_iota(I32, (TQ, TQ), 1)
    bucket = _t5_bucket(jnp.maximum(dc * TQ + t - s, 0))
    for h in range(N_HEADS):
        acc = jnp.zeros((TQ, TQ), F32)
        for beta in range(NUM_BUCKETS):
            acc = jnp.where(bucket == beta, rb_ref[beta, h], acc)
        o_ref[0, h] = acc


def _bias_tiles(rel_bias):
    return pl.pallas_call(
        _bias_kernel,
        grid=(3,),
        in_specs=[pl.BlockSpec(memory_space=pltpu.SMEM)],
        out_specs=pl.BlockSpec((1, N_HEADS, TQ, TQ), lambda d: (d, 0, 0, 0)),
        out_shape=jax.ShapeDtypeStruct((3, N_HEADS, TQ, TQ), F32),
        compiler_params=_cparams(("arbitrary",)),
        name="bias_tiles",
    )(rel_bias)


def _rmsnorm_kernel(x_ref, g_ref, o_ref):
    x = x_ref[...]
    ms = jnp.mean(x * x, axis=-1, keepdims=True)
    o_ref[...] = (x * lax.rsqrt(ms + EPS) * g_ref[...]).astype(o_ref.dtype)


def _rmsnorm(x, g):
    m, d = x.shape
    return pl.pallas_call(
        _rmsnorm_kernel,
        grid=(m // TM_NORM,),
        in_specs=[pl.BlockSpec((TM_NORM, d), lambda i: (i, 0)),
                  pl.BlockSpec((1, d), lambda i: (0, 0))],
        out_specs=pl.BlockSpec((TM_NORM, d), lambda i: (i, 0)),
        out_shape=jax.ShapeDtypeStruct((m, d), BF16),
        compiler_params=_cparams(("arbitrary",)),
        name="rmsnorm",
    )(x, g.reshape(1, d))


def _wspec(k, layer):
    return pl.BlockSpec((None, k, TN), lambda i, j: (layer, 0, j))


def _in_proj_kernel(a_ref, wt_ref, o_ref):
    o_ref[...] = lax.dot_general(a_ref[...], wt_ref[...], (((1,), (1,)), ((), ())),
                                 preferred_element_type=F32)


def _in_proj(xn, w_in_p, layer):
    return pl.pallas_call(
        _in_proj_kernel,
        grid=(M_ALL // TM, NP_IN // TN),
        in_specs=[pl.BlockSpec((TM, D_MODEL), lambda i, j: (i, 0)),
                  pl.BlockSpec((None, TN, D_MODEL), lambda i, j: (layer, j, 0))],
        out_specs=pl.BlockSpec((TM, TN), lambda i, j: (i, j)),
        out_shape=jax.ShapeDtypeStruct((M_ALL, NP_IN), F32),
        compiler_params=_cparams(("arbitrary", "arbitrary")),
        name="in_proj",
    )(xn, w_in_p)


def _merge_kernel(oa_ref, ob_ref, wa_ref, wb_ref, ga_ref, gb_ref, o_ref):
    a = jnp.dot(oa_ref[...], wa_ref[...], preferred_element_type=F32)
    b = jnp.dot(ob_ref[...], wb_ref[...], preferred_element_type=F32)
    o_ref[...] = (_sigmoid(ga_ref[...]) * a + _sigmoid(gb_ref[...]) * b).astype(o_ref.dtype)


def _merge(oa, ob, wa, wb, z, layer):
    return pl.pallas_call(
        _merge_kernel,
        grid=(M_ALL // TM, D_MODEL // TN),
        in_specs=[pl.BlockSpec((TM, WIDTH), lambda i, j: (i, 0)),
                  pl.BlockSpec((TM, WIDTH), lambda i, j: (i, 0)),
                  _wspec(WIDTH, layer), _wspec(WIDTH, layer),
                  pl.BlockSpec((TM, TN), lambda i, j: (i, CB_GA + j)),
                  pl.BlockSpec((TM, TN), lambda i, j: (i, CB_GB + j))],
        out_specs=pl.BlockSpec((TM, TN), lambda i, j: (i, j)),
        out_shape=jax.ShapeDtypeStruct((M_ALL, D_MODEL), BF16),
        compiler_params=_cparams(("arbitrary", "arbitrary")),
        name="merge",
    )(oa, ob, wa, wb, z, z)


def _residual_proj_kernel(a_ref, w_ref, x_ref, o_ref):
    o_ref[...] = x_ref[...] + jnp.dot(a_ref[...], w_ref[...], preferred_element_type=F32)


def _residual_proj(a, w, x, layer, tm, name):
    k = a.shape[1]
    return pl.pallas_call(
        _residual_proj_kernel,
        grid=(M_ALL // tm, D_MODEL // TN),
        in_specs=[pl.BlockSpec((tm, k), lambda i, j: (i, 0)),
                  _wspec(k, layer),
                  pl.BlockSpec((tm, TN), lambda i, j: (i, j))],
        out_specs=pl.BlockSpec((tm, TN), lambda i, j: (i, j)),
        out_shape=jax.ShapeDtypeStruct((M_ALL, D_MODEL), F32),
        compiler_params=_cparams(("arbitrary", "arbitrary")),
        name=name,
    )(a, w, x)


def _ffn_up_kernel(a_ref, wg_ref, wu_ref, o_ref):
    a = a_ref[...]
    g = jnp.dot(a, wg_ref[...], preferred_element_type=F32)
    u = jnp.dot(a, wu_ref[...], preferred_element_type=F32)
    o_ref[...] = (_silu(g) * u).astype(o_ref.dtype)


def _ffn_up(hn, wg, wu, layer):
    return pl.pallas_call(
        _ffn_up_kernel,
        grid=(M_ALL // TM, D_FF // TN),
        in_specs=[pl.BlockSpec((TM, D_MODEL), lambda i, j: (i, 0)),
                  _wspec(D_MODEL, layer), _wspec(D_MODEL, layer)],
        out_specs=pl.BlockSpec((TM, TN), lambda i, j: (i, j)),
        out_shape=jax.ShapeDtypeStruct((M_ALL, D_FF), BF16),
        compiler_params=_cparams(("arbitrary", "arbitrary")),
        name="ffn_up",
    )(hn, wg, wu)


def _head_norm(x, g):
    ms = jnp.mean(x * x, axis=-1, keepdims=True)
    return x * lax.rsqrt(ms + EPS) * g


def _kidx_norm(kiw, g_pad):
    lane = lax.broadcasted_iota(I32, kiw.shape, 1)
    ki = jnp.where(lane < IDX_DIM, kiw, 0.0)
    ms = jnp.sum(ki * ki, axis=-1, keepdims=True) * (1.0 / IDX_DIM)
    return ki * lax.rsqrt(ms + EPS) * g_pad


def _post_kernel(zq_ref, zk_ref, zv_ref, zqi_ref, zkiw_ref, gq_ref, gk_ref, gki_ref, *rest):
    qT_ref, qiT_ref, wT_ref, knb_ref, vT_ref, kn_ref, v_ref, ki_ref = rest[-8:]
    gq = gq_ref[...]
    gk = gk_ref[...]
    for h in range(N_HEADS):
        hs = slice(h * HEAD_DIM, (h + 1) * HEAD_DIM)
        qn = _head_norm(zq_ref[:, hs], gq)
        qT_ref[0, hs, :] = qn.T.astype(BF16)
        kn = _head_norm(zk_ref[:, hs], gk)
        kn_ref[:, hs] = kn
        knb_ref[:, hs] = kn.astype(BF16)
        v = zv_ref[:, hs]
        v_ref[:, hs] = v
        vT_ref[0, 0, hs, :] = v.T.astype(BF16)
    for p in range(N_HEADS * IDX_DIM // LANES):
        ps = slice(p * LANES, (p + 1) * LANES)
        qiT_ref[0, ps, :] = zqi_ref[:, ps].T
    kiw = zkiw_ref[:, 0:LANES]
    ki_ref[...] = _kidx_norm(kiw, gki_ref[...])
    wT_ref[0] = (kiw * IDX_WEIGHT_SCALE).T[IDX_DIM:IDX_DIM + N_HEADS, :]


def _post(z, gq, gk, gki_pad, layer, stacked):
    nb = SEQ // LANES
    zspec = lambda cb, w: pl.BlockSpec((LANES, w), lambda r: (r, cb * COL // w))
    gspec = pl.BlockSpec((1, LANES), lambda r: (0, 0))
    in_specs = [zspec(CB_Q, WIDTH), zspec(CB_K, WIDTH), zspec(CB_V, WIDTH),
                zspec(CB_QI, COL), zspec(CB_KIW, COL), gspec, gspec, gspec]
    args = [z, z, z, z, z, gq, gk, gki_pad]
    aliases = {}
    for n, buf in enumerate(stacked):
        aliases[len(args)] = 5 + n
        in_specs.append(pl.BlockSpec(memory_space=pl.ANY))
        args.append(buf)
    return pl.pallas_call(
        _post_kernel,
        grid=(N_PROMPT // LANES,),
        in_specs=in_specs,
        out_specs=[
            pl.BlockSpec((1, WIDTH, LANES), lambda r: (r // nb, 0, r % nb)),
            pl.BlockSpec((1, N_HEADS * IDX_DIM, LANES), lambda r: (r // nb, 0, r % nb)),
            pl.BlockSpec((1, N_HEADS, LANES), lambda r: (r // nb, 0, r % nb)),
            pl.BlockSpec((LANES, WIDTH), lambda r: (r, 0)),
            pl.BlockSpec((1, 1, WIDTH, LANES),
                         lambda r: (r // nb, (r % nb) // (TQ // LANES), 0, r % (TQ // LANES))),
            pl.BlockSpec((None, LANES, WIDTH), lambda r: (layer, r, 0)),
            pl.BlockSpec((None, LANES, WIDTH), lambda r: (layer, r, 0)),
            pl.BlockSpec((None, LANES, LANES), lambda r: (layer, r, 0)),
        ],
        out_shape=[
            jax.ShapeDtypeStruct((BATCH, WIDTH, SEQ), BF16),
            jax.ShapeDtypeStruct((BATCH, N_HEADS * IDX_DIM, SEQ), F32),
            jax.ShapeDtypeStruct((BATCH, N_HEADS, SEQ), F32),
            jax.ShapeDtypeStruct((N_PROMPT, WIDTH), BF16),
            jax.ShapeDtypeStruct((BATCH, NQB, WIDTH, TQ), BF16),
            jax.ShapeDtypeStruct((DEPTH, N_PROMPT, WIDTH), F32),
            jax.ShapeDtypeStruct((DEPTH, N_PROMPT, WIDTH), F32),
            jax.ShapeDtypeStruct((DEPTH, N_PROMPT, LANES), F32),
        ],
        input_output_aliases=aliases,
        compiler_params=_cparams(("arbitrary",)),
        name="post_a",
    )(*args)


def _sortable_key(score):
    score = jnp.where(score == 0.0, 0.0, score)
    bits = pltpu.bitcast(score, I32)
    return jnp.where(bits < 0, bits ^ 0x7FFFFFFF, bits)


def _attn_kernel(qT_ref, qiT_ref, wT_ref, k_ref, vT_ref, ki_ref, bias_ref, _rows_ref, o_ref,
                 keys_ref, hi_ref, lo_ref, eq_ref, sel_ref, acc_ref, p_ref, lg_ref, cut_ref):
    qb = pl.program_id(1)
    nkb = qb + 1
    t_pos = qb * TQ + lax.broadcasted_iota(I32, (1, TQ), 1)
    s_loc = lax.broadcasted_iota(I32, (TQ, TQ), 0)

    qi = qiT_ref[0].astype(BF16)
    w = wT_ref[0]

    def score_blk(kb, carry):
        r0 = pl.multiple_of(kb * TQ, TQ)
        ki = ki_ref[pl.ds(r0, TQ), :][:, 0:IDX_DIM].astype(BF16)
        acc = jnp.zeros((TQ, TQ), F32)
        for h in range(N_HEADS):
            s = jnp.dot(ki, qi[h * IDX_DIM:(h + 1) * IDX_DIM, :], preferred_element_type=F32)
            acc = acc + jnp.maximum(s, 0.0) * w[h:h + 1, :]
        key = jnp.where(kb * TQ + s_loc <= t_pos, _sortable_key(acc), INT_MIN)
        keys_ref[kb] = key
        hi_ref[kb] = (key >> 16).astype(I16)
        lo_ref[kb] = ((key & 0xFFFF) - HALF16).astype(I16)
        return carry

    lax.fori_loop(0, nkb, score_blk, 0)

    @pl.when(nkb % 2 == 1)
    def _():
        keys_ref[nkb] = jnp.full((TQ, TQ), INT_MIN, I32)
        hi_ref[nkb] = jnp.full((TQ, TQ), -HALF16, I16)
        lo_ref[nkb] = jnp.full((TQ, TQ), -HALF16, I16)

    n_pairs = (nkb + 1) // 2

    def count(indicator):
        def body(i, acc):
            for d in range(2):
                kb = 2 * i + d
                x = indicator(keys_ref[kb], kb * TQ + s_loc)
                acc = acc + jnp.sum(x.reshape(TQ // SUBLANES, SUBLANES, TQ), axis=0)
            return acc
        acc = lax.fori_loop(0, n_pairs, body, jnp.zeros((SUBLANES, TQ), F32))
        return jnp.sum(acc, axis=0, keepdims=True)

    def count16(indicator):
        rows = 2 * SUBLANES
        def body(i, acc):
            parts = []
            for d in range(2):
                x = indicator(2 * i + d).reshape(TQ // rows, rows, TQ)
                parts += [x[r] for r in range(TQ // rows)]
            while len(parts) > 1:
                parts = [parts[n] + parts[n + 1] for n in range(0, len(parts), 2)]
            return acc + parts[0]
        acc = lax.fori_loop(0, n_pairs, body, jnp.zeros((rows, TQ), I16))
        return jnp.sum(acc.astype(I32), axis=0, keepdims=True)

    one16 = jnp.int16(1)
    zero16 = jnp.int16(0)

    def bisect16(count_ge, target):
        cnt = count_ge(jnp.zeros((1, TQ), I16))
        ok = cnt >= target
        v0 = jnp.where(ok, 0, -HALF16).astype(I32)

        def step(i, carry):
            v, c_at = carry
            cand = v + jnp.left_shift(jnp.int32(1), 14 - i)
            cnt = count_ge(cand.astype(I16))
            ok = cnt >= target
            return jnp.where(ok, cand, v), jnp.where(ok, cnt, c_at)

        return lax.fori_loop(0, 15, step, (v0, jnp.where(ok, cnt, -1)))

    k_eff = jnp.minimum(TOPK, t_pos + 1)
    tau_hi, _ = bisect16(
        lambda c: count16(lambda kb: jnp.where(hi_ref[kb] >= c, one16, zero16)), k_eff)
    tau_hi16 = tau_hi.astype(I16)
    n_gt = count16(lambda kb: jnp.where(hi_ref[kb] > tau_hi16, one16, zero16))

    def eq_blk(i, carry):
        for d in range(2):
            kb = 2 * i + d
            eq_ref[kb] = jnp.where(hi_ref[kb] == tau_hi16, one16, zero16)
        return carry

    lax.fori_loop(0, n_pairs, eq_blk, 0)
    n_eq_hi = count16(lambda kb: eq_ref[kb])
    need_lo = k_eff - n_gt
    tau_lo, n_ge_lo = bisect16(
        lambda c: count16(lambda kb: jnp.where(lo_ref[kb] >= c, eq_ref[kb], zero16)), need_lo)
    n_ge_lo = jnp.where(n_ge_lo < 0, n_eq_hi, n_ge_lo)
    tau = tau_hi * (2 * HALF16) + (tau_lo + HALF16)

    cut_ref[...] = jnp.full((1, TQ), BIG_IDX, I32)

    @pl.when(jnp.max((n_gt + n_ge_lo - k_eff).astype(F32)) > 0.0)
    def _():
        need = k_eff.astype(F32) - count(lambda key, s: jnp.where(key > tau, 1.0, 0.0))

        def idx_step(i, c):
            trial = c + jnp.left_shift(jnp.int32(1), 10 - i)
            cnt = count(lambda key, s: jnp.where(key == tau, jnp.where(s < trial, 1.0, 0.0), 0.0))
            return jnp.where(cnt < need, trial, c)

        cut_ref[...] = lax.fori_loop(0, 11, idx_step, jnp.zeros((1, TQ), I32))

    cut = cut_ref[...]

    def sel_blk(kb, carry):
        key = keys_ref[kb]
        s_pos = kb * TQ + s_loc
        sel_ref[kb] = jnp.where(
            key > tau, 0.0,
            jnp.where(key == tau, jnp.where(s_pos <= cut, 0.0, NEG_LARGE), NEG_LARGE))
        return carry

    lax.fori_loop(0, nkb, sel_blk, 0)

    n_sub = TQ // TK_SUB

    def logits(h, kb, u):
        hs = slice(h * HEAD_DIM, (h + 1) * HEAD_DIM)
        r0 = pl.multiple_of(kb * TQ + u * TK_SUB, TK_SUB)
        us = slice(u * TK_SUB, (u + 1) * TK_SUB)
        dc = jnp.minimum(qb - kb, 2)
        s = jnp.dot(k_ref[pl.ds(r0, TK_SUB), hs], qT_ref[0, hs, :], preferred_element_type=F32)
        return s * ATTN_SCALE + bias_ref[dc, h, us, :] + sel_ref[kb, us, :]

    def fold8(x):
        return x.reshape(TK_SUB // SUBLANES, SUBLANES, TQ)

    def rows(kb, u):
        return pl.ds(pl.multiple_of(kb * TQ + u * TK_SUB, TK_SUB), TK_SUB)

    def max_blk(kb, ms):
        out = []
        for h in range(N_HEADS):
            m8 = ms[h]
            for u in range(n_sub):
                s = logits(h, kb, u)
                lg_ref[h, rows(kb, u), :] = s
                m8 = jnp.maximum(m8, jnp.max(fold8(s), axis=0))
            out.append(m8)
        return tuple(out)

    ms = lax.fori_loop(0, nkb, max_blk,
                       tuple(jnp.full((SUBLANES, TQ), NEG_LARGE, F32) for _ in range(N_HEADS)))
    m = [jnp.max(m8, axis=0, keepdims=True) for m8 in ms]
    acc_ref[...] = jnp.zeros(acc_ref.shape, F32)

    def pv_blk(kb, ls):
        out = []
        for h in range(N_HEADS):
            l8 = ls[h]
            for u in range(n_sub):
                p = jnp.exp(lg_ref[h, rows(kb, u), :] - m[h])
                l8 = l8 + jnp.sum(fold8(p), axis=0)
                p_ref[h, u * TK_SUB:(u + 1) * TK_SUB, :] = p.astype(BF16)
            out.append(l8)
        for h in range(N_HEADS):
            hs = slice(h * HEAD_DIM, (h + 1) * HEAD_DIM)
            acc_ref[h] += jnp.dot(vT_ref[0, kb, hs, :], p_ref[h], preferred_element_type=F32)
        return tuple(out)

    ls = lax.fori_loop(0, nkb, pv_blk,
                       tuple(jnp.zeros((SUBLANES, TQ), F32) for _ in range(N_HEADS)))
    for h in range(N_HEADS):
        o = acc_ref[h] / jnp.sum(ls[h], axis=0, keepdims=True)
        o_ref[:, h * HEAD_DIM:(h + 1) * HEAD_DIM] = o.T.astype(o_ref.dtype)


def _attn_prompt(qT, qiT, wT, knb, vT, ki_all, bias, layer):
    return pl.pallas_call(
        _attn_kernel,
        grid=(BATCH, NQB),
        in_specs=[
            pl.BlockSpec((1, WIDTH, TQ), lambda b, q: (b, 0, q)),
            pl.BlockSpec((1, N_HEADS * IDX_DIM, TQ), lambda b, q: (b, 0, q)),
            pl.BlockSpec((1, N_HEADS, TQ), lambda b, q: (b, 0, q)),
            pl.BlockSpec((SEQ, WIDTH), lambda b, q: (b, 0)),
            pl.BlockSpec((1, NQB, WIDTH, TQ), lambda b, q: (b, 0, 0, 0)),
            pl.BlockSpec((None, SEQ, LANES), lambda b, q: (layer, b, 0)),
            pl.BlockSpec((3, N_HEADS, TQ, TQ), lambda b, q: (0, 0, 0, 0), pipeline_mode=pl.Buffered(1)),
            pl.BlockSpec(memory_space=pl.ANY),
        ],
        out_specs=pl.BlockSpec((TQ, WIDTH), lambda b, q: (b * NQB + q, 0)),
        out_shape=jax.ShapeDtypeStruct((M_ALL, WIDTH), BF16),
        input_output_aliases={7: 0},
        scratch_shapes=[
            pltpu.VMEM((NQB, TQ, TQ), I32),
            pltpu.VMEM((NQB, TQ, TQ), I16),
            pltpu.VMEM((NQB, TQ, TQ), I16),
            pltpu.VMEM((NQB, TQ, TQ), I16),
            pltpu.VMEM((NQB, TQ, TQ), F32),
            pltpu.VMEM((N_HEADS, HEAD_DIM, TQ), F32),
            pltpu.VMEM((N_HEADS, TQ, TQ), BF16),
            pltpu.VMEM((N_HEADS, SEQ, TQ), F32),
            pltpu.VMEM((1, TQ), I32),
        ],
        compiler_params=_cparams(("arbitrary", "arbitrary")),
        name="attn_prompt",
    )(qT, qiT, wT, knb, vT, ki_all, bias, jnp.zeros((M_ALL, WIDTH), BF16))


def _hgrn_kernel(zq_ref, zf_ref, zi_ref, zg_ref, lb_ref, og_ref, _rows_ref, ob_ref, st_ref,
                 ST_ref, b_ref, k_ref, q_ref, i_ref, oi_ref):
    ST_ref[...] = jnp.zeros(ST_ref.shape, F32)
    og = og_ref[...]
    row = lax.broadcasted_iota(I32, (CHUNK, LANES), 0)
    lane = lax.broadcasted_iota(I32, (CHUNK, LANES), 1)
    sub_row = lax.broadcasted_iota(I32, (SUBLANES, LANES), 0)

    def chunk(c, carry):
        r0 = pl.multiple_of(c * CHUNK, CHUNK)
        b_last = []
        for hh in range(HPB):
            hs = slice(hh * LANES, (hh + 1) * LANES)
            lb = lb_ref[hh]
            f = lb + (1.0 - lb) * _sigmoid(zf_ref[pl.ds(r0, CHUNK), hs])
            b = jnp.log(f)
            for d in (1, 2, 4, 8):
                b = b + jnp.where(row % SUB >= d, pltpu.roll(b, d, axis=0), 0.0)
            b_ref[hh] = b
            k_ref[hh] = 1.0 - f
            q_ref[hh] = _silu(zq_ref[pl.ds(r0, CHUNK), hs])
            i_ref[hh] = zi_ref[pl.ds(r0, CHUNK), hs]
            b_last.append(b_ref[hh, pl.ds(SUB - 1, N_SUB, stride=SUB), :])

        def sub_block(j, carry2):
            j0 = pl.multiple_of(j * SUB, SUB)
            for hh in range(HPB):
                hs = slice(hh * LANES, (hh + 1) * LANES)
                b_lo = b_ref[hh, pl.ds(j0, SUBLANES), :]
                b_hi = b_ref[hh, pl.ds(j0 + SUBLANES, SUBLANES), :]
                q_lo = q_ref[hh, pl.ds(j0, SUBLANES), :]
                q_hi = q_ref[hh, pl.ds(j0 + SUBLANES, SUBLANES), :]
                o_lo = jnp.zeros((SUBLANES, LANES), F32)
                o_hi = jnp.zeros((SUBLANES, LANES), F32)
                for s in range(SUB):
                    bs = jnp.broadcast_to(b_ref[hh, pl.ds(j0 + s, 1), :], (SUBLANES, LANES))
                    ks = jnp.broadcast_to(k_ref[hh, pl.ds(j0 + s, 1), :], (SUBLANES, LANES))
                    iv = jnp.broadcast_to(i_ref[hh, pl.ds(j0 + s, 1), :], (SUBLANES, LANES))
                    if s < SUBLANES:
                        ok = sub_row >= s
                        x = jnp.where(ok, q_lo * ks * jnp.exp(jnp.where(ok, b_lo - bs, 0.0)), 0.0)
                        o_lo = o_lo + jnp.sum(x, axis=-1, keepdims=True) * iv
                        x = q_hi * ks * jnp.exp(b_hi - bs)
                        o_hi = o_hi + jnp.sum(x, axis=-1, keepdims=True) * iv
                    else:
                        ok = sub_row >= s - SUBLANES
                        x = jnp.where(ok, q_hi * ks * jnp.exp(jnp.where(ok, b_hi - bs, 0.0)), 0.0)
                        o_hi = o_hi + jnp.sum(x, axis=-1, keepdims=True) * iv
                oi_ref[hh, pl.ds(j0, SUBLANES), :] = o_lo
                oi_ref[hh, pl.ds(j0 + SUBLANES, SUBLANES), :] = o_hi
            return carry2

        lax.fori_loop(0, N_SUB, sub_block, 0)

        for hh in range(HPB):
            hs = slice(hh * LANES, (hh + 1) * LANES)
            b = b_ref[hh]
            bl_b = jnp.concatenate(
                [jnp.broadcast_to(b_last[hh][j:j + 1, :], (SUB, LANES)) for j in range(N_SUB)], axis=0)
            q_dec = (q_ref[hh] * jnp.exp(b)).astype(BF16)
            k_end = (k_ref[hh] * jnp.exp(bl_b - b)).astype(BF16)
            decay = jnp.exp(b_last[hh])
            iT = i_ref[hh].T
            it_stack = jnp.concatenate(
                [jnp.where(lane // SUB == j, iT, 0.0).astype(BF16) for j in range(N_SUB)], axis=0)
            pT = jnp.dot(it_stack, k_end, preferred_element_type=F32)
            r = ST_ref[hh]
            r_list = []
            for j in range(N_SUB):
                r_list.append(r.astype(BF16))
                r = r * decay[j:j + 1, :] + pT[j * LANES:(j + 1) * LANES, :]
            ST_ref[hh] = r
            r_stack = jnp.concatenate(r_list, axis=0)
            oT_all = lax.dot_general(r_stack, q_dec, (((1,), (1,)), ((), ())),
                                     preferred_element_type=F32)
            oT = jnp.zeros((LANES, CHUNK), F32)
            for j in range(N_SUB):
                oT = oT + jnp.where(lane // SUB == j, oT_all[j * LANES:(j + 1) * LANES, :], 0.0)
            o = oT.T + oi_ref[hh]
            zg = zg_ref[pl.ds(r0, CHUNK), hs]
            ob_ref[pl.ds(r0, CHUNK), hs] = (_head_norm(o, og) * _silu(zg)).astype(ob_ref.dtype)
        return carry

    lax.fori_loop(0, SEQ // CHUNK, chunk, 0)
    for hh in range(HPB):
        st_ref[hh] = ST_ref[hh].T


def _hgrn_prompt(z, lb3, og):
    w = HPB * LANES
    zspec = lambda cb: pl.BlockSpec((SEQ, w), lambda b, h: (b, cb * COL // w + h))
    return pl.pallas_call(
        _hgrn_kernel,
        grid=(BATCH, N_HEADS // HPB),
        in_specs=[zspec(CB_HQ), zspec(CB_HF), zspec(CB_HI), zspec(CB_HG),
                  pl.BlockSpec((HPB, 1, LANES), lambda b, h: (h, 0, 0)),
                  pl.BlockSpec((1, LANES), lambda b, h: (0, 0)),
                  pl.BlockSpec(memory_space=pl.ANY)],
        out_specs=[pl.BlockSpec((SEQ, w), lambda b, h: (b, h)),
                   pl.BlockSpec((None, HPB, LANES, LANES), lambda b, h: (b, h, 0, 0))],
        out_shape=[jax.ShapeDtypeStruct((M_ALL, WIDTH), BF16),
                   jax.ShapeDtypeStruct((BATCH, N_HEADS, LANES, LANES), F32)],
        input_output_aliases={6: 0},
        scratch_shapes=[pltpu.VMEM((HPB, LANES, LANES), F32),
                        pltpu.VMEM((HPB, CHUNK, LANES), F32),
                        pltpu.VMEM((HPB, CHUNK, LANES), F32),
                        pltpu.VMEM((HPB, CHUNK, LANES), F32),
                        pltpu.VMEM((HPB, CHUNK, LANES), F32),
                        pltpu.VMEM((HPB, CHUNK, LANES), F32)],
        compiler_params=_cparams(("arbitrary", "arbitrary")),
        name="hgrn_prompt",
    )(z, z, z, z, lb3, og, jnp.zeros((M_ALL, WIDTH), BF16))


def _rows_to_cols(x):
    pad = jnp.zeros((LANES - x.shape[0], LANES), F32)
    return jnp.concatenate([x, pad], axis=0).T


def _hgrn_sample_kernel(zq_ref, zf_ref, zi_ref, zg_ref, lb_ref, og_ref, s0_ref, ob_any_ref,
                        ob_ref, s1_ref):
    del ob_any_ref
    lb = lb_ref[...]
    f = lb + (1.0 - lb) * _sigmoid(zf_ref[...])
    fT = _rows_to_cols(f)
    kT = _rows_to_cols(1.0 - f)
    qT = _rows_to_cols(_silu(zq_ref[...]))
    iv = zi_ref[...]
    rows = []
    for b in range(DEC_BATCH):
        s1 = fT[:, b:b + 1] * s0_ref[b] + kT[:, b:b + 1] * iv[b:b + 1, :]
        s1_ref[b] = s1
        rows.append(jnp.sum(qT[:, b:b + 1] * s1, axis=0, keepdims=True))
    o = jnp.concatenate(rows, axis=0)
    o = _head_norm(o, og_ref[...]) * _silu(zg_ref[...])
    pad = jnp.zeros((LANES - DEC_BATCH, LANES), F32)
    ob_ref[...] = jnp.concatenate([o, pad], axis=0).astype(ob_ref.dtype)


def _hgrn_sample(z, lb3, og, state_hgrn, ob_all, layer):
    cpb = COL // LANES
    zspec = lambda cb: pl.BlockSpec((DEC_BATCH, LANES), lambda h: (SAMPLE_ROW_BLOCK, cb * cpb + h))
    return pl.pallas_call(
        _hgrn_sample_kernel,
        grid=(N_HEADS,),
        in_specs=[zspec(CB_HQ), zspec(CB_HF), zspec(CB_HI), zspec(CB_HG),
                  pl.BlockSpec((None, 1, LANES), lambda h: (h, 0, 0)),
                  pl.BlockSpec((1, LANES), lambda h: (0, 0)),
                  pl.BlockSpec((None, DEC_BATCH, None, LANES, LANES), lambda h: (layer, 0, h, 0, 0)),
                  pl.BlockSpec(memory_space=pl.ANY)],
        out_specs=[pl.BlockSpec((LANES, LANES), lambda h: (SAMPLE_TILE, h)),
                   pl.BlockSpec((DEC_BATCH, None, LANES, LANES), lambda h: (0, h, 0, 0))],
        out_shape=[jax.ShapeDtypeStruct((M_ALL, WIDTH), BF16),
                   jax.ShapeDtypeStruct((DEC_BATCH, N_HEADS, LANES, LANES), F32)],
        input_output_aliases={7: 0},
        compiler_params=_cparams(("arbitrary",)),
        name="hgrn_sample",
    )(z, z, z, z, lb3, og, state_hgrn, ob_all)


def _sample_norm_kernel(zq_ref, zk_ref, zkiw_ref, gq_ref, gk_ref, gki_ref, qn_ref, kn_ref, kiw_ref):
    gq = gq_ref[...]
    gk = gk_ref[...]
    for h in range(N_HEADS):
        hs = slice(h * HEAD_DIM, (h + 1) * HEAD_DIM)
        qn_ref[:, hs] = _head_norm(zq_ref[:, hs], gq)
        kn_ref[:, hs] = _head_norm(zk_ref[:, hs], gk)
    kiw = zkiw_ref[:, 0:LANES]
    lane = lax.broadcasted_iota(I32, kiw.shape, 1)
    kiw_ref[...] = _kidx_norm(kiw, gki_ref[...]) + jnp.where(
        (lane >= IDX_DIM) & (lane < IDX_DIM + N_HEADS), kiw * IDX_WEIGHT_SCALE, 0.0)


def _sample_norms(z, gq, gk, gki_pad):
    zspec = lambda cb, w: pl.BlockSpec((DEC_BATCH, w), lambda i: (SAMPLE_ROW_BLOCK, cb * COL // w))
    gspec = pl.BlockSpec((1, LANES), lambda i: (0, 0))
    return pl.pallas_call(
        _sample_norm_kernel,
        grid=(1,),
        in_specs=[zspec(CB_Q, WIDTH), zspec(CB_K, WIDTH), zspec(CB_KIW, COL), gspec, gspec, gspec],
        out_specs=[pl.BlockSpec((DEC_BATCH, WIDTH), lambda i: (0, 0)),
                   pl.BlockSpec((DEC_BATCH, WIDTH), lambda i: (0, 0)),
                   pl.BlockSpec((DEC_BATCH, LANES), lambda i: (0, 0))],
        out_shape=[jax.ShapeDtypeStruct((DEC_BATCH, WIDTH), F32),
                   jax.ShapeDtypeStruct((DEC_BATCH, WIDTH), F32),
                   jax.ShapeDtypeStruct((DEC_BATCH, LANES), F32)],
        compiler_params=_cparams(("arbitrary",)),
        name="sample_norms",
    )(z, z, z, gq, gk, gki_pad)


KEY_ROWS = N_PAGES + SUBLANES


def _select_row(x, b):
    sub = lax.broadcasted_iota(I32, x.shape, 0)
    return jnp.sum(jnp.where(sub == b, x, 0.0), axis=0, keepdims=True)


def _sample_topk_kernel(pt_ref, zqi_ref, kiw_ref, cache_ref, sel_ref, buf_ref, sem_ref, key_ref,
                        pos_ref, cut_ref, *, layer):
    b = pl.program_id(0)

    def page_copy(p):
        return pltpu.make_async_copy(cache_ref.at[layer, pt_ref[b, p]], buf_ref.at[p], sem_ref.at[0])

    def start(p, c):
        page_copy(p).start()
        return c

    lax.fori_loop(0, N_PAGES, start, 0)

    qrow = _select_row(zqi_ref[...], b)
    head = lax.broadcasted_iota(I32, (N_HEADS, N_HEADS * IDX_DIM), 0)
    col = lax.broadcasted_iota(I32, (N_HEADS, N_HEADS * IDX_DIM), 1)
    qm = jnp.where(col // IDX_DIM == head, jnp.broadcast_to(qrow, head.shape), 0.0)
    q128 = qm[:, 0:LANES]
    for p in range(1, N_HEADS * IDX_DIM // LANES):
        q128 = q128 + qm[:, p * LANES:(p + 1) * LANES]
    lane8 = lax.broadcasted_iota(I32, (N_HEADS, LANES), 1)
    q8 = jnp.where(lane8 < IDX_DIM, q128 + pltpu.roll(q128, IDX_DIM, axis=1), 0.0)
    qT = _rows_to_cols(q8)[0:IDX_DIM, :]

    kiw_row = _select_row(kiw_ref[...], b)
    r8 = lax.broadcasted_iota(I32, (N_HEADS, LANES), 0)
    wcol = jnp.sum(jnp.where(lane8 == IDX_DIM + r8, jnp.broadcast_to(kiw_row, (N_HEADS, LANES)), 0.0),
                   axis=-1, keepdims=True)
    knew = jnp.where(lane8[0:1] < IDX_DIM, kiw_row, 0.0)
    sn = jnp.sum(q8 * knew, axis=-1, keepdims=True)
    s_new = jnp.sum(jnp.maximum(sn, 0.0) * wcol, axis=0, keepdims=True)

    def wait(p, c):
        page_copy(p).wait()
        return c

    lax.fori_loop(0, N_PAGES, wait, 0)

    qcols = [jnp.broadcast_to(qT[:, h:h + 1], (IDX_DIM, PAGE_SIZE))[None] for h in range(N_HEADS)]

    def page_group(g, carry):
        r0 = pl.multiple_of(g * SUBLANES, SUBLANES)
        pages = buf_ref[pl.ds(r0, SUBLANES)]
        s = jnp.zeros((SUBLANES, PAGE_SIZE), F32)
        for h in range(N_HEADS):
            sh = jnp.sum(pages * qcols[h], axis=1)
            s = s + jnp.maximum(sh, 0.0) * wcol[h:h + 1, :]
        key_ref[b, pl.ds(r0, SUBLANES), :] = _sortable_key(s)
        return carry

    lax.fori_loop(0, N_PAGES // SUBLANES, page_group, 0)
    row_t = lax.broadcasted_iota(I32, (SUBLANES, LANES), 0)
    lane_t = lax.broadcasted_iota(I32, (SUBLANES, LANES), 1)
    key_ref[b, pl.ds(N_PAGES, SUBLANES), :] = jnp.where(
        (row_t == 0) & (lane_t == 0),
        _sortable_key(jnp.broadcast_to(s_new, (SUBLANES, LANES))), INT_MIN)

    @pl.when(b == DEC_BATCH - 1)
    def _():
        shape = (DEC_BATCH, KEY_ROWS, LANES)
        pos = lax.broadcasted_iota(I32, shape, 1) * LANES + lax.broadcasted_iota(I32, shape, 2)

        def count(indicator):
            x = indicator(key_ref[...])
            return jnp.sum(jnp.sum(x, axis=1), axis=1, keepdims=True)

        cnt = count(lambda key: jnp.where(key >= 0, 1.0, 0.0))
        tau0 = jnp.where(cnt >= TOPK, 0, INT_MIN).astype(I32)

        def bit_step(i, tau):
            cand = tau + jnp.left_shift(jnp.int32(1), 30 - i)
            cnt = count(lambda key: jnp.where(key >= cand[:, :, None], 1.0, 0.0))
            return jnp.where(cnt >= TOPK, cand, tau)

        tau = lax.fori_loop(0, 31, bit_step, tau0)[:, :, None]

        need = TOPK - count(lambda key: jnp.where(key > tau, 1.0, 0.0))
        n_eq = count(lambda key: jnp.where(key == tau, 1.0, 0.0))
        cut_ref[...] = jnp.full(cut_ref.shape, BIG_IDX, I32)

        @pl.when(jnp.max(n_eq - need) > 0.0)
        def _():
            def idx_step(i, c):
                trial = c + jnp.left_shift(jnp.int32(1), 14 - i)
                cnt = count(lambda key: jnp.where(key == tau, jnp.where(pos < trial[:, :, None], 1.0, 0.0), 0.0))
                return jnp.where(cnt < need, trial, c)
            c = lax.fori_loop(0, 15, idx_step, jnp.zeros((DEC_BATCH, 1), I32))
            cut_ref[...] = jnp.broadcast_to(c, cut_ref.shape)

        cut = cut_ref[:, 0:1][:, :, None]
        key = key_ref[...]
        chosen = jnp.where(key > tau, pos,
                           jnp.where(key == tau, jnp.where(pos <= cut, pos, BIG_IDX), BIG_IDX))
        pos_ref[...] = chosen.astype(F32)

        out_lane = lax.broadcasted_iota(I32, (DEC_BATCH, TOPK), 1)

        def pick(it, carry):
            sel, last = carry
            p = pos_ref[...]
            first = jnp.min(jnp.min(jnp.where(p > last[:, :, None], p, float(BIG_IDX)), axis=1),
                            axis=1, keepdims=True)
            return jnp.where(out_lane == it, first.astype(I32), sel), first

        sel, _ = lax.fori_loop(0, TOPK, pick, (jnp.zeros((DEC_BATCH, TOPK), I32),
                                               jnp.full((DEC_BATCH, 1), -1.0, F32)))
        sel_ref[...] = sel


def _sample_topk(page_table, z, kiw_s, cache_kidx, layer):
    grid_spec = pltpu.PrefetchScalarGridSpec(
        num_scalar_prefetch=1,
        grid=(DEC_BATCH,),
        in_specs=[pl.BlockSpec((DEC_BATCH, COL), lambda b, pt: (SAMPLE_ROW_BLOCK, CB_QI)),
                  pl.BlockSpec((DEC_BATCH, LANES), lambda b, pt: (0, 0)),
                  pl.BlockSpec(memory_space=pl.ANY)],
        out_specs=pl.BlockSpec((DEC_BATCH, TOPK), lambda b, pt: (0, 0)),
        scratch_shapes=[pltpu.VMEM((N_PAGES, IDX_DIM, PAGE_SIZE), F32),
                        pltpu.SemaphoreType.DMA((1,)),
                        pltpu.VMEM((DEC_BATCH, KEY_ROWS, LANES), I32),
                        pltpu.VMEM((DEC_BATCH, KEY_ROWS, LANES), F32),
                        pltpu.VMEM((DEC_BATCH, LANES), I32)],
    )
    cache_t = jnp.swapaxes(cache_kidx, 2, 3)
    return pl.pallas_call(
        functools.partial(_sample_topk_kernel, layer=layer),
        grid_spec=grid_spec,
        out_shape=jax.ShapeDtypeStruct((DEC_BATCH, TOPK), I32),
        compiler_params=_cparams(("arbitrary",)),
        name="sample_topk",
    )(page_table, z, kiw_s, cache_t)


def _sample_attn_kernel(sel_s_ref, pt_ref, sel_ref, qn_ref, kn_ref, zv_ref, rbT_ref,
                        ck_ref, cv_ref, oa_any_ref, o_ref, kbuf_ref, vbuf_ref, sem_ref, rows_ref,
                        *, layer):
    del oa_any_ref
    b = pl.program_id(0)

    def copies(j):
        sp = jnp.minimum(sel_s_ref[b, j], PAST_LEN - 1)
        page = pt_ref[b, sp // PAGE_SIZE]
        off = sp % PAGE_SIZE
        dst = pl.ds(pl.multiple_of(j * N_HEADS, N_HEADS), N_HEADS)
        return (pltpu.make_async_copy(ck_ref.at[layer, page, off], kbuf_ref.at[dst], sem_ref.at[0]),
                pltpu.make_async_copy(cv_ref.at[layer, page, off], vbuf_ref.at[dst], sem_ref.at[1]))

    def start(j, c):
        ck, cv = copies(j)
        ck.start()
        cv.start()
        return c

    lax.fori_loop(0, TOPK, start, 0)

    def heads_on_rows(x):
        row = _select_row(x, b)
        return jnp.concatenate([row[:, h * HEAD_DIM:(h + 1) * HEAD_DIM] for h in range(N_HEADS)],
                               axis=0)

    q = heads_on_rows(qn_ref[...])
    k_new = heads_on_rows(kn_ref[...])
    v_new = heads_on_rows(zv_ref[...])
    sel_b = jnp.sum(jnp.where(lax.broadcasted_iota(I32, (DEC_BATCH, TOPK), 0) == b, sel_ref[...], 0),
                    axis=0, keepdims=True)
    is_new = sel_b >= PAST_LEN
    bucket = _t5_bucket(jnp.maximum(PAST_LEN - sel_b, 0))
    bias = jnp.zeros((N_HEADS, TOPK), F32)
    for beta in range(NUM_BUCKETS):
        bias = jnp.where(bucket == beta, rbT_ref[:, beta:beta + 1], bias)

    def wait(j, c):
        ck, cv = copies(j)
        ck.wait()
        cv.wait()
        return c

    lax.fori_loop(0, TOPK, wait, 0)

    nt = (((1,), (1,)), ((), ()))
    hrow = lax.broadcasted_iota(I32, (N_HEADS, TOPK), 0)
    qb16 = q.astype(BF16)
    logits = jnp.zeros((N_HEADS, TOPK), F32)
    for h in range(N_HEADS):
        kh = kbuf_ref[pl.ds(h, TOPK, stride=N_HEADS), :].astype(BF16)
        res = lax.dot_general(qb16, kh, nt, preferred_element_type=F32)
        logits = jnp.where(hrow == h, res, logits)
    logit_new = jnp.sum(qb16.astype(F32) * k_new.astype(BF16).astype(F32), axis=-1, keepdims=True)
    logits = jnp.where(is_new, logit_new, logits) * ATTN_SCALE + bias
    m = jnp.max(logits, axis=1, keepdims=True)
    e = jnp.exp(logits - m)
    p = e / jnp.sum(e, axis=1, keepdims=True)
    p_new = jnp.sum(jnp.where(is_new, p, 0.0), axis=1, keepdims=True)
    p_mm = jnp.where(is_new, 0.0, p).astype(BF16)
    hrow_o = lax.broadcasted_iota(I32, (N_HEADS, HEAD_DIM), 0)
    o = jnp.zeros((N_HEADS, HEAD_DIM), F32)
    for h in range(N_HEADS):
        vh = vbuf_ref[pl.ds(h, TOPK, stride=N_HEADS), :].astype(BF16)
        res = jnp.dot(p_mm, vh, preferred_element_type=F32)
        o = jnp.where(hrow_o == h, res, o)
    o = o + p_new.astype(BF16).astype(F32) * v_new.astype(BF16).astype(F32)
    orow = jnp.concatenate([o[h:h + 1, :] for h in range(N_HEADS)], axis=1)

    @pl.when(b == 0)
    def _():
        rows_ref[...] = jnp.zeros(rows_ref.shape, F32)

    sub = lax.broadcasted_iota(I32, (DEC_BATCH, WIDTH), 0)
    rows_ref[...] = jnp.where(sub == b, jnp.broadcast_to(orow, (DEC_BATCH, WIDTH)), rows_ref[...])

    @pl.when(b == DEC_BATCH - 1)
    def _():
        pad = jnp.zeros((LANES - DEC_BATCH, WIDTH), F32)
        o_ref[...] = jnp.concatenate([rows_ref[...], pad], axis=0).astype(o_ref.dtype)


def _sample_attn(sel, page_table, qn_s, kn_s, z, rbT, cache_k, cache_v, oa_all, layer):
    grid_spec = pltpu.PrefetchScalarGridSpec(
        num_scalar_prefetch=2,
        grid=(DEC_BATCH,),
        in_specs=[pl.BlockSpec((DEC_BATCH, TOPK), lambda b, s, pt: (0, 0)),
                  pl.BlockSpec((DEC_BATCH, WIDTH), lambda b, s, pt: (0, 0)),
                  pl.BlockSpec((DEC_BATCH, WIDTH), lambda b, s, pt: (0, 0)),
                  pl.BlockSpec((DEC_BATCH, WIDTH), lambda b, s, pt: (SAMPLE_ROW_BLOCK, CB_V * COL // WIDTH)),
                  pl.BlockSpec((N_HEADS, NUM_BUCKETS), lambda b, s, pt: (0, 0)),
                  pl.BlockSpec(memory_space=pl.ANY),
                  pl.BlockSpec(memory_space=pl.ANY),
                  pl.BlockSpec(memory_space=pl.ANY)],
        out_specs=pl.BlockSpec((LANES, WIDTH), lambda b, s, pt: (SAMPLE_TILE, 0)),
        scratch_shapes=[pltpu.VMEM((TOPK * N_HEADS, HEAD_DIM), F32),
                        pltpu.VMEM((TOPK * N_HEADS, HEAD_DIM), F32),
                        pltpu.SemaphoreType.DMA((2,)),
                        pltpu.VMEM((DEC_BATCH, WIDTH), F32)],
    )
    return pl.pallas_call(
        functools.partial(_sample_attn_kernel, layer=layer),
        grid_spec=grid_spec,
        out_shape=jax.ShapeDtypeStruct((M_ALL, WIDTH), BF16),
        input_output_aliases={9: 0},
        compiler_params=_cparams(("arbitrary",)),
        name="sample_attn",
    )(sel, page_table, sel, qn_s, kn_s, z, rbT, cache_k, cache_v, oa_all)


KIW_PAD = COL - IDX_DIM - N_HEADS


def _w_in_prep_kernel(w_ref, o_ref):
    j = pl.program_id(1)
    x = w_ref[0]
    row = lax.broadcasted_iota(I32, x.shape, 0)
    o_ref[...] = jnp.where((j != CB_KIW) | (row < IDX_DIM + N_HEADS), x, 0.0).astype(BF16)


def _permute_w_in(w_in):
    depth = w_in.shape[0]
    wt = jnp.swapaxes(w_in, 1, 2)

    def src_rows(l, j):
        return l, pl.multiple_of(jnp.where(j <= CB_KIW, j * COL, j * COL - KIW_PAD), SUBLANES), 0

    return pl.pallas_call(
        _w_in_prep_kernel,
        grid=(depth, NP_IN // COL),
        in_specs=[pl.BlockSpec((pl.Element(1), pl.Element(COL), pl.Element(D_MODEL)), src_rows)],
        out_specs=pl.BlockSpec((None, COL, D_MODEL), lambda l, j: (l, j, 0)),
        out_shape=jax.ShapeDtypeStruct((depth, NP_IN, D_MODEL), BF16),
        compiler_params=_cparams(("arbitrary", "arbitrary")),
        name="w_in_prep",
    )(wt)


def _stack_rows(prompt_rows, sample_rows):
    pad = jnp.zeros((M_ALL - N_PROMPT - DEC_BATCH, prompt_rows.shape[1]), prompt_rows.dtype)
    return jnp.concatenate([prompt_rows, sample_rows.astype(prompt_rows.dtype), pad], axis=0)


def kernel(x_prompt, x_sample, cache_k, cache_v, cache_kidx, state_hgrn, page_table, norm_mix_g, w_in,
           q_norm_g, k_norm_g, kidx_norm_g, rel_bias, hgrn_lb, hgrn_out_g, w_up_a, w_up_b, w_out,
           norm_ffn_g, w_ffn_gate, w_ffn_up, w_ffn_down):
    w_in_p = _permute_w_in(w_in)
    w_up_a, w_up_b, w_out = (w.astype(BF16) for w in (w_up_a, w_up_b, w_out))
    w_ffn_gate, w_ffn_up, w_ffn_down = (w.astype(BF16) for w in (w_ffn_gate, w_ffn_up, w_ffn_down))
    lower = _lower_bounds(hgrn_lb)
    bias = _bias_tiles(rel_bias)
    rbT = rel_bias.T
    gki_pad = jnp.pad(kidx_norm_g, ((0, 0), (0, LANES - IDX_DIM)))

    x = _stack_rows(x_prompt.reshape(N_PROMPT, D_MODEL), x_sample.reshape(DEC_BATCH, D_MODEL))
    stacked = (jnp.zeros((DEPTH, N_PROMPT, WIDTH), F32), jnp.zeros((DEPTH, N_PROMPT, WIDTH), F32),
               jnp.zeros((DEPTH, N_PROMPT, LANES), F32))
    outs = [[] for _ in range(5)]
    for l in range(DEPTH):
        gq = q_norm_g[l].reshape(1, HEAD_DIM)
        gk = k_norm_g[l].reshape(1, HEAD_DIM)
        gki = gki_pad[l].reshape(1, LANES)
        og = hgrn_out_g[l].reshape(1, LANES)
        lb3 = lower[l].reshape(N_HEADS, 1, LANES)

        xn = _rmsnorm(x, norm_mix_g[l])
        z = _in_proj(xn, w_in_p, l)

        qT, qiT, wT, knb, vT, *stacked = _post(z, gq, gk, gki, l, stacked)
        oa = _attn_prompt(qT, qiT, wT, knb, vT, stacked[2], bias, l)
        qn_s, kn_s, kiw_s = _sample_norms(z, gq, gk, gki)
        sel = _sample_topk(page_table, z, kiw_s, cache_kidx, l)
        oa = _sample_attn(sel, page_table, qn_s, kn_s, z, rbT, cache_k, cache_v, oa, l)

        ob, st_p = _hgrn_prompt(z, lb3, og)
        ob, st_s = _hgrn_sample(z, lb3, og, state_hgrn, ob, l)

        mixed = _merge(oa, ob, w_up_a, w_up_b, z, l)
        x = _residual_proj(mixed, w_out, x, l, TM, "out_proj")
        hn = _rmsnorm(x, norm_ffn_g[l])
        x = _residual_proj(_ffn_up(hn, w_ffn_gate, w_ffn_up, l), w_ffn_down, x, l, TM_DOWN, "ffn_down")

        v_cols = slice(CB_V * COL, CB_V * COL + WIDTH)
        outs[0].append(st_p)
        outs[1].append(kn_s.reshape(DEC_BATCH, 1, N_HEADS, HEAD_DIM))
        outs[2].append(z[N_PROMPT:N_PROMPT + DEC_BATCH, v_cols].reshape(DEC_BATCH, 1, N_HEADS, HEAD_DIM))
        outs[3].append(kiw_s[:, :IDX_DIM].reshape(DEC_BATCH, 1, IDX_DIM))
        outs[4].append(st_s)

    k_all, v_all, ki_all = stacked
    y_prompt = x[:N_PROMPT].reshape(BATCH, SEQ, D_MODEL)
    y_sample = x[N_PROMPT:N_PROMPT + DEC_BATCH].reshape(DEC_BATCH, 1, D_MODEL)
    st_p, k_s, v_s, ki_s, st_s = (jnp.stack(o) for o in outs)
    return (y_prompt, y_sample,
            k_all.reshape(DEPTH, BATCH, SEQ, N_HEADS, HEAD_DIM),
            v_all.reshape(DEPTH, BATCH, SEQ, N_HEADS, HEAD_DIM),
            ki_all[:, :, :IDX_DIM].reshape(DEPTH, BATCH, SEQ, IDX_DIM),
            st_p, k_s, v_s, ki_s, st_s)
```

```python
import functools
import math

import jax
import jax.numpy as jnp
from jax import lax
from jax.experimental import pallas as pl
from jax.experimental.pallas import tpu as pltpu

F32 = jnp.float32
BF16 = jnp.bfloat16
I32 = jnp.int32
I16 = jnp.int16

D_MODEL = 2048
BATCH = 4
SEQ = 2048
DEPTH = 4
DEC_BATCH = 8
PAST_LEN = 16384
PAGE_SIZE = 128
N_PAGES = PAST_LEN // PAGE_SIZE
N_HEADS = 8
HEAD_DIM = 128
WIDTH = N_HEADS * HEAD_DIM
IDX_DIM = 64
TOPK = 256
ATTN_SCALE = HEAD_DIM ** -0.5
IDX_WEIGHT_SCALE = (N_HEADS ** -0.5) * (IDX_DIM ** -0.5)
NEG_LARGE = -1e30
NUM_BUCKETS = 32
MAX_DISTANCE = 128
D_FF = ((8 * D_MODEL // 3 + 255) // 256) * 256
EPS = 1e-6
IN_WIDTHS = (WIDTH, WIDTH, WIDTH, N_HEADS * IDX_DIM, IDX_DIM, N_HEADS,
             WIDTH, WIDTH, WIDTH, WIDTH, D_MODEL, D_MODEL)

SUBLANES = 8
LANES = 128
VMEM_LIMIT = 56 * 1024 * 1024

N_PROMPT = BATCH * SEQ
M_ALL = N_PROMPT + LANES
SAMPLE_ROW_BLOCK = N_PROMPT // SUBLANES
SAMPLE_TILE = N_PROMPT // LANES

COL = 512
NP_IN = 24 * COL
CB_Q, CB_K, CB_V, CB_QI, CB_KIW = 0, 2, 4, 6, 7
CB_HQ, CB_HF, CB_HI, CB_HG, CB_GA, CB_GB = 8, 10, 12, 14, 16, 20

TM = M_ALL // 5
TM_DOWN = M_ALL // 10
TM_NORM = M_ALL // 20
TN = 512
TN_IN = 1024

TQ = 256
NQB = SEQ // TQ
TK_SUB = 128
INT_MIN = -2 ** 31
HALF16 = 2 ** 15
BIG_IDX = 2 ** 30

SUB = 16
CHUNK = 128
N_SUB = CHUNK // SUB
HPB = 4


def _cparams(sem):
    return pltpu.CompilerParams(dimension_semantics=sem, vmem_limit_bytes=VMEM_LIMIT)


def _sigmoid(x):
    return 1.0 / (1.0 + jnp.exp(-x))


def _silu(x):
    return x * _sigmoid(x)


def _lb_kernel(lb_ref, o_ref):
    x = lb_ref[...]
    m = jnp.max(x, axis=0, keepdims=True)
    e = jnp.exp(x - m)
    p = e / jnp.sum(e, axis=0, keepdims=True)
    acc = jnp.zeros_like(p[0:1])
    rows = []
    for l in range(DEPTH):
        acc = acc + p[l:l + 1]
        rows.append(acc - p[0:1])
    o_ref[...] = jnp.concatenate(rows, axis=0)


def _lower_bounds(hgrn_lb):
    return pl.pallas_call(
        _lb_kernel,
        out_shape=jax.ShapeDtypeStruct((DEPTH, WIDTH), F32),
        name="hgrn_lower_bounds",
    )(hgrn_lb)


def _t5_bucket(n):
    max_exact = NUM_BUCKETS // 2
    nf = jnp.maximum(n, 1).astype(F32)
    large = max_exact + (jnp.log(nf / max_exact) / math.log(MAX_DISTANCE / max_exact)
                         * (NUM_BUCKETS - max_exact)).astype(I32)
    large = jnp.minimum(large, NUM_BUCKETS - 1)
    return jnp.where(n < max_exact, n, large)


def _bias_kernel(rb_ref, o_ref):
    dc = pl.program_id(0)
    s = lax.broadcasted_iota(I32, (TQ, TQ), 0)
    t = lax.broadcasted_iota(I32, (TQ, TQ), 1)
    bucket = _t5_bucket(jnp.maximum(dc * TQ + t - s, 0))
    for h in range(N_HEADS):
        acc = jnp.zeros((TQ, TQ), F32)
        for beta in range(NUM_BUCKETS):
            acc = jnp.where(bucket == beta, rb_ref[beta, h], acc)
        o_ref[0, h] = acc


def _bias_tiles(rel_bias):
    return pl.pallas_call(
        _bias_kernel,
        grid=(3,),
        in_specs=[pl.BlockSpec(memory_space=pltpu.SMEM)],
        out_specs=pl.BlockSpec((1, N_HEADS, TQ, TQ), lambda d: (d, 0, 0, 0)),
        out_shape=jax.ShapeDtypeStruct((3, N_HEADS, TQ, TQ), F32),
        compiler_params=_cparams(("arbitrary",)),
        name="bias_tiles",
    )(rel_bias)


def _rmsnorm_kernel(x_ref, g_ref, o_ref):
    x = x_ref[...]
    ms = jnp.mean(x * x, axis=-1, keepdims=True)
    o_ref[...] = (x * lax.rsqrt(ms + EPS) * g_ref[...]).astype(o_ref.dtype)


def _rmsnorm(x, g):
    m, d = x.shape
    return pl.pallas_call(
        _rmsnorm_kernel,
        grid=(m // TM_NORM,),
        in_specs=[pl.BlockSpec((TM_NORM, d), lambda i: (i, 0)),
                  pl.BlockSpec((1, d), lambda i: (0, 0))],
        out_specs=pl.BlockSpec((TM_NORM, d), lambda i: (i, 0)),
        out_shape=jax.ShapeDtypeStruct((m, d), BF16),
        compiler_params=_cparams(("arbitrary",)),
        name="rmsnorm",
    )(x, g.reshape(1, d))


def _wspec(k, layer):
    return pl.BlockSpec((None, k, TN), lambda i, j: (layer, 0, j))


def _in_proj_kernel(a_ref, wt_ref, o_ref):
    o_ref[...] = lax.dot_general(a_ref[...], wt_ref[...], (((1,), (1,)), ((), ())),
                                 preferred_element_type=F32)


def _in_proj(xn, w_in_p, layer):
    return pl.pallas_call(
        _in_proj_kernel,
        grid=(M_ALL // TM, NP_IN // TN_IN),
        in_specs=[pl.BlockSpec((TM, D_MODEL), lambda i, j: (i, 0)),
                  pl.BlockSpec((None, TN_IN, D_MODEL), lambda i, j: (layer, j, 0))],
        out_specs=pl.BlockSpec((TM, TN_IN), lambda i, j: (i, j)),
        out_shape=jax.ShapeDtypeStruct((M_ALL, NP_IN), F32),
        compiler_params=_cparams(("arbitrary", "arbitrary")),
        name="in_proj",
    )(xn, w_in_p)


def _merge_kernel(oa_ref, ob_ref, wa_ref, wb_ref, ga_ref, gb_ref, o_ref):
    a = jnp.dot(oa_ref[...], wa_ref[...], preferred_element_type=F32)
    b = jnp.dot(ob_ref[...], wb_ref[...], preferred_element_type=F32)
    o_ref[...] = (_sigmoid(ga_ref[...]) * a + _sigmoid(gb_ref[...]) * b).astype(o_ref.dtype)


def _merge(oa, ob, wa, wb, z, layer):
    return pl.pallas_call(
        _merge_kernel,
        grid=(M_ALL // TM, D_MODEL // TN),
        in_specs=[pl.BlockSpec((TM, WIDTH), lambda i, j: (i, 0)),
                  pl.BlockSpec((TM, WIDTH), lambda i, j: (i, 0)),
                  _wspec(WIDTH, layer), _wspec(WIDTH, layer),
                  pl.BlockSpec((TM, TN), lambda i, j: (i, CB_GA + j)),
                  pl.BlockSpec((TM, TN), lambda i, j: (i, CB_GB + j))],
        out_specs=pl.BlockSpec((TM, TN), lambda i, j: (i, j)),
        out_shape=jax.ShapeDtypeStruct((M_ALL, D_MODEL), BF16),
        compiler_params=_cparams(("arbitrary", "arbitrary")),
        name="merge",
    )(oa, ob, wa, wb, z, z)


def _residual_proj_kernel(a_ref, w_ref, x_ref, o_ref):
    o_ref[...] = x_ref[...] + jnp.dot(a_ref[...], w_ref[...], preferred_element_type=F32)


def _residual_proj(a, w, x, layer, tm, name):
    k = a.shape[1]
    return pl.pallas_call(
        _residual_proj_kernel,
        grid=(M_ALL // tm, D_MODEL // TN),
        in_specs=[pl.BlockSpec((tm, k), lambda i, j: (i, 0)),
                  _wspec(k, layer),
                  pl.BlockSpec((tm, TN), lambda i, j: (i, j))],
        out_specs=pl.BlockSpec((tm, TN), lambda i, j: (i, j)),
        out_shape=jax.ShapeDtypeStruct((M_ALL, D_MODEL), F32),
        compiler_params=_cparams(("arbitrary", "arbitrary")),
        name=name,
    )(a, w, x)


def _ffn_up_kernel(a_ref, wg_ref, wu_ref, o_ref):
    a = a_ref[...]
    g = jnp.dot(a, wg_ref[...], preferred_element_type=F32)
    u = jnp.dot(a, wu_ref[...], preferred_element_type=F32)
    o_ref[...] = (_silu(g) * u).astype(o_ref.dtype)


def _ffn_up(hn, wg, wu, layer):
    return pl.pallas_call(
        _ffn_up_kernel,
        grid=(M_ALL // TM, D_FF // TN),
        in_specs=[pl.BlockSpec((TM, D_MODEL), lambda i, j: (i, 0)),
                  _wspec(D_MODEL, layer), _wspec(D_MODEL, layer)],
        out_specs=pl.BlockSpec((TM, TN), lambda i, j: (i, j)),
        out_shape=jax.ShapeDtypeStruct((M_ALL, D_FF), BF16),
        compiler_params=_cparams(("arbitrary", "arbitrary")),
        name="ffn_up",
    )(hn, wg, wu)


def _head_norm(x, g):
    ms = jnp.mean(x * x, axis=-1, keepdims=True)
    return x * lax.rsqrt(ms + EPS) * g


def _kidx_norm(kiw, g_pad):
    lane = lax.broadcasted_iota(I32, kiw.shape, 1)
    ki = jnp.where(lane < IDX_DIM, kiw, 0.0)
    ms = jnp.sum(ki * ki, axis=-1, keepdims=True) * (1.0 / IDX_DIM)
    return ki * lax.rsqrt(ms + EPS) * g_pad


def _post_kernel(zq_ref, zk_ref, zv_ref, zqi_ref, zkiw_ref, gq_ref, gk_ref, gki_ref, *rest):
    qT_ref, qiT_ref, wT_ref, knb_ref, vT_ref, kn_ref, v_ref, ki_ref = rest[-8:]
    gq = gq_ref[...]
    gk = gk_ref[...]
    for h in range(N_HEADS):
        hs = slice(h * HEAD_DIM, (h + 1) * HEAD_DIM)
        qn = _head_norm(zq_ref[:, hs], gq)
        qT_ref[0, hs, :] = qn.T.astype(BF16)
        kn = _head_norm(zk_ref[:, hs], gk)
        kn_ref[:, hs] = kn
        knb_ref[:, hs] = kn.astype(BF16)
        v = zv_ref[:, hs]
        v_ref[:, hs] = v
        vT_ref[0, 0, hs, :] = v.T.astype(BF16)
    for p in range(N_HEADS * IDX_DIM // LANES):
        ps = slice(p * LANES, (p + 1) * LANES)
        qiT_ref[0, ps, :] = zqi_ref[:, ps].T
    kiw = zkiw_ref[:, 0:LANES]
    ki_ref[...] = _kidx_norm(kiw, gki_ref[...])
    wT_ref[0] = (kiw * IDX_WEIGHT_SCALE).T[IDX_DIM:IDX_DIM + N_HEADS, :]


def _post(z, gq, gk, gki_pad, layer, stacked):
    nb = SEQ // LANES
    zspec = lambda cb, w: pl.BlockSpec((LANES, w), lambda r: (r, cb * COL // w))
    gspec = pl.BlockSpec((1, LANES), lambda r: (0, 0))
    in_specs = [zspec(CB_Q, WIDTH), zspec(CB_K, WIDTH), zspec(CB_V, WIDTH),
                zspec(CB_QI, COL), zspec(CB_KIW, COL), gspec, gspec, gspec]
    args = [z, z, z, z, z, gq, gk, gki_pad]
    aliases = {}
    for n, buf in enumerate(stacked):
        aliases[len(args)] = 5 + n
        in_specs.append(pl.BlockSpec(memory_space=pl.ANY))
        args.append(buf)
    return pl.pallas_call(
        _post_kernel,
        grid=(N_PROMPT // LANES,),
        in_specs=in_specs,
        out_specs=[
            pl.BlockSpec((1, WIDTH, LANES), lambda r: (r // nb, 0, r % nb)),
            pl.BlockSpec((1, N_HEADS * IDX_DIM, LANES), lambda r: (r // nb, 0, r % nb)),
            pl.BlockSpec((1, N_HEADS, LANES), lambda r: (r // nb, 0, r % nb)),
            pl.BlockSpec((LANES, WIDTH), lambda r: (r, 0)),
            pl.BlockSpec((1, 1, WIDTH, LANES),
                         lambda r: (r // nb, (r % nb) // (TQ // LANES), 0, r % (TQ // LANES))),
            pl.BlockSpec((None, LANES, WIDTH), lambda r: (layer, r, 0)),
            pl.BlockSpec((None, LANES, WIDTH), lambda r: (layer, r, 0)),
            pl.BlockSpec((None, LANES, LANES), lambda r: (layer, r, 0)),
        ],
        out_shape=[
            jax.ShapeDtypeStruct((BATCH, WIDTH, SEQ), BF16),
            jax.ShapeDtypeStruct((BATCH, N_HEADS * IDX_DIM, SEQ), F32),
            jax.ShapeDtypeStruct((BATCH, N_HEADS, SEQ), F32),
            jax.ShapeDtypeStruct((N_PROMPT, WIDTH), BF16),
            jax.ShapeDtypeStruct((BATCH, NQB, WIDTH, TQ), BF16),
            jax.ShapeDtypeStruct((DEPTH, N_PROMPT, WIDTH), F32),
            jax.ShapeDtypeStruct((DEPTH, N_PROMPT, WIDTH), F32),
            jax.ShapeDtypeStruct((DEPTH, N_PROMPT, LANES), F32),
        ],
        input_output_aliases=aliases,
        compiler_params=_cparams(("arbitrary",)),
        name="post_a",
    )(*args)


def _sortable_key(score):
    score = jnp.where(score == 0.0, 0.0, score)
    bits = pltpu.bitcast(score, I32)
    return jnp.where(bits < 0, bits ^ 0x7FFFFFFF, bits)


def _attn_kernel(qT_ref, qiT_ref, wT_ref, k_ref, vT_ref, ki_ref, bias_ref, _rows_ref, o_ref,
                 keys_ref, hi_ref, lo_ref, eq_ref, sel_ref, acc_ref, p_ref, lg_ref, cut_ref):
    qb = pl.program_id(1)
    nkb = qb + 1
    t_pos = qb * TQ + lax.broadcasted_iota(I32, (1, TQ), 1)
    s_loc = lax.broadcasted_iota(I32, (TQ, TQ), 0)

    qi = qiT_ref[0].astype(BF16)
    w = wT_ref[0]

    def score_blk(kb, carry):
        r0 = pl.multiple_of(kb * TQ, TQ)
        ki = ki_ref[pl.ds(r0, TQ), :][:, 0:IDX_DIM].astype(BF16)
        acc = jnp.zeros((TQ, TQ), F32)
        for h in range(N_HEADS):
            s = jnp.dot(ki, qi[h * IDX_DIM:(h + 1) * IDX_DIM, :], preferred_element_type=F32)
            acc = acc + jnp.maximum(s, 0.0) * w[h:h + 1, :]
        key = jnp.where(kb * TQ + s_loc <= t_pos, _sortable_key(acc), INT_MIN)
        keys_ref[kb] = key
        hi_ref[kb] = (key >> 16).astype(I16)
        lo_ref[kb] = ((key & 0xFFFF) - HALF16).astype(I16)
        return carry

    lax.fori_loop(0, nkb, score_blk, 0)

    @pl.when(nkb % 2 == 1)
    def _():
        keys_ref[nkb] = jnp.full((TQ, TQ), INT_MIN, I32)
        hi_ref[nkb] = jnp.full((TQ, TQ), -HALF16, I16)
        lo_ref[nkb] = jnp.full((TQ, TQ), -HALF16, I16)

    n_pairs = (nkb + 1) // 2

    def count(indicator):
        def body(i, acc):
            for d in range(2):
                kb = 2 * i + d
                x = indicator(keys_ref[kb], kb * TQ + s_loc)
                acc = acc + jnp.sum(x.reshape(TQ // SUBLANES, SUBLANES, TQ), axis=0)
            return acc
        acc = lax.fori_loop(0, n_pairs, body, jnp.zeros((SUBLANES, TQ), F32))
        return jnp.sum(acc, axis=0, keepdims=True)

    def count16(indicator):
        rows = 2 * SUBLANES
        def body(i, acc):
            parts = []
            for d in range(2):
                x = indicator(2 * i + d).reshape(TQ // rows, rows, TQ)
                parts += [x[r] for r in range(TQ // rows)]
            while len(parts) > 1:
                parts = [parts[n] + parts[n + 1] for n in range(0, len(parts), 2)]
            return acc + parts[0]
        acc = lax.fori_loop(0, n_pairs, body, jnp.zeros((rows, TQ), I16))
        return jnp.sum(acc.astype(I32), axis=0, keepdims=True)

    one16 = jnp.int16(1)
    zero16 = jnp.int16(0)

    def bisect16(count_ge, target):
        cnt = count_ge(jnp.zeros((1, TQ), I16))
        ok = cnt >= target
        v0 = jnp.where(ok, 0, -HALF16).astype(I32)

        def step(i, carry):
            v, c_at = carry
            cand = v + jnp.left_shift(jnp.int32(1), 14 - i)
            cnt = count_ge(cand.astype(I16))
            ok = cnt >= target
            return jnp.where(ok, cand, v), jnp.where(ok, cnt, c_at)

        return lax.fori_loop(0, 15, step, (v0, jnp.where(ok, cnt, -1)))

    k_eff = jnp.minimum(TOPK, t_pos + 1)
    tau_hi, _ = bisect16(
        lambda c: count16(lambda kb: jnp.where(hi_ref[kb] >= c, one16, zero16)), k_eff)
    tau_hi16 = tau_hi.astype(I16)
    n_gt = count16(lambda kb: jnp.where(hi_ref[kb] > tau_hi16, one16, zero16))

    def eq_blk(i, carry):
        for d in range(2):
            kb = 2 * i + d
            eq_ref[kb] = jnp.where(hi_ref[kb] == tau_hi16, one16, zero16)
        return carry

    lax.fori_loop(0, n_pairs, eq_blk, 0)
    n_eq_hi = count16(lambda kb: eq_ref[kb])
    need_lo = k_eff - n_gt
    tau_lo, n_ge_lo = bisect16(
        lambda c: count16(lambda kb: jnp.where(lo_ref[kb] >= c, eq_ref[kb], zero16)), need_lo)
    n_ge_lo = jnp.where(n_ge_lo < 0, n_eq_hi, n_ge_lo)
    tau = tau_hi * (2 * HALF16) + (tau_lo + HALF16)

    cut_ref[...] = jnp.full((1, TQ), BIG_IDX, I32)

    @pl.when(jnp.max((n_gt + n_ge_lo - k_eff).astype(F32)) > 0.0)
    def _():
        need = k_eff.astype(F32) - count(lambda key, s: jnp.where(key > tau, 1.0, 0.0))

        def idx_step(i, c):
            trial = c + jnp.left_shift(jnp.int32(1), 10 - i)
            cnt = count(lambda key, s: jnp.where(key == tau, jnp.where(s < trial, 1.0, 0.0), 0.0))
            return jnp.where(cnt < need, trial, c)

        cut_ref[...] = lax.fori_loop(0, 11, idx_step, jnp.zeros((1, TQ), I32))

    cut = cut_ref[...]

    def sel_blk(kb, carry):
        key = keys_ref[kb]
        s_pos = kb * TQ + s_loc
        sel_ref[kb] = jnp.where(
            key > tau, 0.0,
            jnp.where(key == tau, jnp.where(s_pos <= cut, 0.0, NEG_LARGE), NEG_LARGE))
        return carry

    lax.fori_loop(0, nkb, sel_blk, 0)

    n_sub = TQ // TK_SUB

    def logits(h, kb, u):
        hs = slice(h * HEAD_DIM, (h + 1) * HEAD_DIM)
        r0 = pl.multiple_of(kb * TQ + u * TK_SUB, TK_SUB)
        us = slice(u * TK_SUB, (u + 1) * TK_SUB)
        dc = jnp.minimum(qb - kb, 2)
        s = jnp.dot(k_ref[pl.ds(r0, TK_SUB), hs], qT_ref[0, hs, :], preferred_element_type=F32)
        return s * ATTN_SCALE + bias_ref[dc, h, us, :] + sel_ref[kb, us, :]

    def fold8(x):
        return x.reshape(TK_SUB // SUBLANES, SUBLANES, TQ)

    def rows(kb, u):
        return pl.ds(pl.multiple_of(kb * TQ + u * TK_SUB, TK_SUB), TK_SUB)

    def max_blk(kb, ms):
        out = []
        for h in range(N_HEADS):
            m8 = ms[h]
            for u in range(n_sub):
                s = logits(h, kb, u)
                lg_ref[h, rows(kb, u), :] = s
                m8 = jnp.maximum(m8, jnp.max(fold8(s), axis=0))
            out.append(m8)
        return tuple(out)

    ms = lax.fori_loop(0, nkb, max_blk,
                       tuple(jnp.full((SUBLANES, TQ), NEG_LARGE, F32) for _ in range(N_HEADS)))
    m = [jnp.max(m8, axis=0, keepdims=True) for m8 in ms]
    acc_ref[...] = jnp.zeros(acc_ref.shape, F32)

    def pv_blk(kb, ls):
        out = []
        for h in range(N_HEADS):
            l8 = ls[h]
            for u in range(n_sub):
                p = jnp.exp(lg_ref[h, rows(kb, u), :] - m[h])
                l8 = l8 + jnp.sum(fold8(p), axis=0)
                p_ref[h, u * TK_SUB:(u + 1) * TK_SUB, :] = p.astype(BF16)
            out.append(l8)
        for h in range(N_HEADS):
            hs = slice(h * HEAD_DIM, (h + 1) * HEAD_DIM)
            acc_ref[h] += jnp.dot(vT_ref[0, kb, hs, :], p_ref[h], preferred_element_type=F32)
        return tuple(out)

    ls = lax.fori_loop(0, nkb, pv_blk,
                       tuple(jnp.zeros((SUBLANES, TQ), F32) for _ in range(N_HEADS)))
    for h in range(N_HEADS):
        o = acc_ref[h] / jnp.sum(ls[h], axis=0, keepdims=True)
        o_ref[:, h * HEAD_DIM:(h + 1) * HEAD_DIM] = o.T.astype(o_ref.dtype)


def _attn_prompt(qT, qiT, wT, knb, vT, ki_all, bias, layer):
    return pl.pallas_call(
        _attn_kernel,
        grid=(BATCH, NQB),
        in_specs=[
            pl.BlockSpec((1, WIDTH, TQ), lambda b, q: (b, 0, q)),
            pl.BlockSpec((1, N_HEADS * IDX_DIM, TQ), lambda b, q: (b, 0, q)),
            pl.BlockSpec((1, N_HEADS, TQ), lambda b, q: (b, 0, q)),
            pl.BlockSpec((SEQ, WIDTH), lambda b, q: (b, 0)),
            pl.BlockSpec((1, NQB, WIDTH, TQ), lambda b, q: (b, 0, 0, 0)),
            pl.BlockSpec((None, SEQ, LANES), lambda b, q: (layer, b, 0)),
            pl.BlockSpec((3, N_HEADS, TQ, TQ), lambda b, q: (0, 0, 0, 0), pipeline_mode=pl.Buffered(1)),
            pl.BlockSpec(memory_space=pl.ANY),
        ],
        out_specs=pl.BlockSpec((TQ, WIDTH), lambda b, q: (b * NQB + q, 0)),
        out_shape=jax.ShapeDtypeStruct((M_ALL, WIDTH), BF16),
        input_output_aliases={7: 0},
        scratch_shapes=[
            pltpu.VMEM((NQB, TQ, TQ), I32),
            pltpu.VMEM((NQB, TQ, TQ), I16),
            pltpu.VMEM((NQB, TQ, TQ), I16),
            pltpu.VMEM((NQB, TQ, TQ), I16),
            pltpu.VMEM((NQB, TQ, TQ), F32),
            pltpu.VMEM((N_HEADS, HEAD_DIM, TQ), F32),
            pltpu.VMEM((N_HEADS, TQ, TQ), BF16),
            pltpu.VMEM((N_HEADS, SEQ, TQ), F32),
            pltpu.VMEM((1, TQ), I32),
        ],
        compiler_params=_cparams(("arbitrary", "arbitrary")),
        name="attn_prompt",
    )(qT, qiT, wT, knb, vT, ki_all, bias, jnp.zeros((M_ALL, WIDTH), BF16))


def _hgrn_kernel(zq_ref, zf_ref, zi_ref, zg_ref, lb_ref, og_ref, _rows_ref, ob_ref, st_ref,
                 ST_ref, b_ref, k_ref, q_ref, i_ref, oi_ref):
    ST_ref[...] = jnp.zeros(ST_ref.shape, F32)
    og = og_ref[...]
    row = lax.broadcasted_iota(I32, (CHUNK, LANES), 0)
    lane = lax.broadcasted_iota(I32, (CHUNK, LANES), 1)
    sub_row = lax.broadcasted_iota(I32, (SUBLANES, LANES), 0)

    def chunk(c, carry):
        r0 = pl.multiple_of(c * CHUNK, CHUNK)
        b_last = []
        for hh in range(HPB):
            hs = slice(hh * LANES, (hh + 1) * LANES)
            lb = lb_ref[hh]
            f = lb + (1.0 - lb) * _sigmoid(zf_ref[pl.ds(r0, CHUNK), hs])
            b = jnp.log(f)
            for d in (1, 2, 4, 8):
                b = b + jnp.where(row % SUB >= d, pltpu.roll(b, d, axis=0), 0.0)
            b_ref[hh] = b
            k_ref[hh] = 1.0 - f
            q_ref[hh] = _silu(zq_ref[pl.ds(r0, CHUNK), hs])
            i_ref[hh] = zi_ref[pl.ds(r0, CHUNK), hs]
            b_last.append(b_ref[hh, pl.ds(SUB - 1, N_SUB, stride=SUB), :])

        def sub_block(j, carry2):
            j0 = pl.multiple_of(j * SUB, SUB)
            for hh in range(HPB):
                hs = slice(hh * LANES, (hh + 1) * LANES)
                b_lo = b_ref[hh, pl.ds(j0, SUBLANES), :]
                b_hi = b_ref[hh, pl.ds(j0 + SUBLANES, SUBLANES), :]
                q_lo = q_ref[hh, pl.ds(j0, SUBLANES), :]
                q_hi = q_ref[hh, pl.ds(j0 + SUBLANES, SUBLANES), :]
                o_lo = jnp.zeros((SUBLANES, LANES), F32)
                o_hi = jnp.zeros((SUBLANES, LANES), F32)
                for s in range(SUB):
                    bs = jnp.broadcast_to(b_ref[hh, pl.ds(j0 + s, 1), :], (SUBLANES, LANES))
                    ks = jnp.broadcast_to(k_ref[hh, pl.ds(j0 + s, 1), :], (SUBLANES, LANES))
                    iv = jnp.broadcast_to(i_ref[hh, pl.ds(j0 + s, 1), :], (SUBLANES, LANES))
                    if s < SUBLANES:
                        ok = sub_row >= s
                        x = jnp.where(ok, q_lo * ks * jnp.exp(jnp.where(ok, b_lo - bs, 0.0)), 0.0)
                        o_lo = o_lo + jnp.sum(x, axis=-1, keepdims=True) * iv
                        x = q_hi * ks * jnp.exp(b_hi - bs)
                        o_hi = o_hi + jnp.sum(x, axis=-1, keepdims=True) * iv
                    else:
                        ok = sub_row >= s - SUBLANES
                        x = jnp.where(ok, q_hi * ks * jnp.exp(jnp.where(ok, b_hi - bs, 0.0)), 0.0)
                        o_hi = o_hi + jnp.sum(x, axis=-1, keepdims=True) * iv
                oi_ref[hh, pl.ds(j0, SUBLANES), :] = o_lo
                oi_ref[hh, pl.ds(j0 + SUBLANES, SUBLANES), :] = o_hi
            return carry2

        lax.fori_loop(0, N_SUB, sub_block, 0)

        for hh in range(HPB):
            hs = slice(hh * LANES, (hh + 1) * LANES)
            b = b_ref[hh]
            bl_b = jnp.concatenate(
                [jnp.broadcast_to(b_last[hh][j:j + 1, :], (SUB, LANES)) for j in range(N_SUB)], axis=0)
            q_dec = (q_ref[hh] * jnp.exp(b)).astype(BF16)
            k_end = (k_ref[hh] * jnp.exp(bl_b - b)).astype(BF16)
            decay = jnp.exp(b_last[hh])
            iT = i_ref[hh].T
            it_stack = jnp.concatenate(
                [jnp.where(lane // SUB == j, iT, 0.0).astype(BF16) for j in range(N_SUB)], axis=0)
            pT = jnp.dot(it_stack, k_end, preferred_element_type=F32)
            r = ST_ref[hh]
            r_list = []
            for j in range(N_SUB):
                r_list.append(r.astype(BF16))
                r = r * decay[j:j + 1, :] + pT[j * LANES:(j + 1) * LANES, :]
            ST_ref[hh] = r
            r_stack = jnp.concatenate(r_list, axis=0)
            oT_all = lax.dot_general(r_stack, q_dec, (((1,), (1,)), ((), ())),
                                     preferred_element_type=F32)
            oT = jnp.zeros((LANES, CHUNK), F32)
            for j in range(N_SUB):
                oT = oT + jnp.where(lane // SUB == j, oT_all[j * LANES:(j + 1) * LANES, :], 0.0)
            o = oT.T + oi_ref[hh]
            zg = zg_ref[pl.ds(r0, CHUNK), hs]
            ob_ref[pl.ds(r0, CHUNK), hs] = (_head_norm(o, og) * _silu(zg)).astype(ob_ref.dtype)
        return carry

    lax.fori_loop(0, SEQ // CHUNK, chunk, 0)
    for hh in range(HPB):
        st_ref[hh] = ST_ref[hh].T


def _hgrn_prompt(z, lb3, og):
    w = HPB * LANES
    zspec = lambda cb: pl.BlockSpec((SEQ, w), lambda b, h: (b, cb * COL // w + h))
    return pl.pallas_call(
        _hgrn_kernel,
        grid=(BATCH, N_HEADS // HPB),
        in_specs=[zspec(CB_HQ), zspec(CB_HF), zspec(CB_HI), zspec(CB_HG),
                  pl.BlockSpec((HPB, 1, LANES), lambda b, h: (h, 0, 0)),
                  pl.BlockSpec((1, LANES), lambda b, h: (0, 0)),
                  pl.BlockSpec(memory_space=pl.ANY)],
        out_specs=[pl.BlockSpec((SEQ, w), lambda b, h: (b, h)),
                   pl.BlockSpec((None, HPB, LANES, LANES), lambda b, h: (b, h, 0, 0))],
        out_shape=[jax.ShapeDtypeStruct((M_ALL, WIDTH), BF16),
                   jax.ShapeDtypeStruct((BATCH, N_HEADS, LANES, LANES), F32)],
        input_output_aliases={6: 0},
        scratch_shapes=[pltpu.VMEM((HPB, LANES, LANES), F32),
                        pltpu.VMEM((HPB, CHUNK, LANES), F32),
                        pltpu.VMEM((HPB, CHUNK, LANES), F32),
                        pltpu.VMEM((HPB, CHUNK, LANES), F32),
                        pltpu.VMEM((HPB, CHUNK, LANES), F32),
                        pltpu.VMEM((HPB, CHUNK, LANES), F32)],
        compiler_params=_cparams(("arbitrary", "arbitrary")),
        name="hgrn_prompt",
    )(z, z, z, z, lb3, og, jnp.zeros((M_ALL, WIDTH), BF16))


def _rows_to_cols(x):
    pad = jnp.zeros((LANES - x.shape[0], LANES), F32)
    return jnp.concatenate([x, pad], axis=0).T


def _hgrn_sample_kernel(zq_ref, zf_ref, zi_ref, zg_ref, lb_ref, og_ref, s0_ref, ob_any_ref,
                        ob_ref, s1_ref):
    del ob_any_ref
    lb = lb_ref[...]
    f = lb + (1.0 - lb) * _sigmoid(zf_ref[...])
    fT = _rows_to_cols(f)
    kT = _rows_to_cols(1.0 - f)
    qT = _rows_to_cols(_silu(zq_ref[...]))
    iv = zi_ref[...]
    rows = []
    for b in range(DEC_BATCH):
        s1 = fT[:, b:b + 1] * s0_ref[b] + kT[:, b:b + 1] * iv[b:b + 1, :]
        s1_ref[b] = s1
        rows.append(jnp.sum(qT[:, b:b + 1] * s1, axis=0, keepdims=True))
    o = jnp.concatenate(rows, axis=0)
    o = _head_norm(o, og_ref[...]) * _silu(zg_ref[...])
    pad = jnp.zeros((LANES - DEC_BATCH, LANES), F32)
    ob_ref[...] = jnp.concatenate([o, pad], axis=0).astype(ob_ref.dtype)


def _hgrn_sample(z, lb3, og, state_hgrn, ob_all, layer):
    cpb = COL // LANES
    zspec = lambda cb: pl.BlockSpec((DEC_BATCH, LANES), lambda h: (SAMPLE_ROW_BLOCK, cb * cpb + h))
    return pl.pallas_call(
        _hgrn_sample_kernel,
        grid=(N_HEADS,),
        in_specs=[zspec(CB_HQ), zspec(CB_HF), zspec(CB_HI), zspec(CB_HG),
                  pl.BlockSpec((None, 1, LANES), lambda h: (h, 0, 0)),
                  pl.BlockSpec((1, LANES), lambda h: (0, 0)),
                  pl.BlockSpec((None, DEC_BATCH, None, LANES, LANES), lambda h: (layer, 0, h, 0, 0)),
                  pl.BlockSpec(memory_space=pl.ANY)],
        out_specs=[pl.BlockSpec((LANES, LANES), lambda h: (SAMPLE_TILE, h)),
                   pl.BlockSpec((DEC_BATCH, None, LANES, LANES), lambda h: (0, h, 0, 0))],
        out_shape=[jax.ShapeDtypeStruct((M_ALL, WIDTH), BF16),
                   jax.ShapeDtypeStruct((DEC_BATCH, N_HEADS, LANES, LANES), F32)],
        input_output_aliases={7: 0},
        compiler_params=_cparams(("arbitrary",)),
        name="hgrn_sample",
    )(z, z, z, z, lb3, og, state_hgrn, ob_all)


def _sample_norm_kernel(zq_ref, zk_ref, zkiw_ref, gq_ref, gk_ref, gki_ref, qn_ref, kn_ref, kiw_ref):
    gq = gq_ref[...]
    gk = gk_ref[...]
    for h in range(N_HEADS):
        hs = slice(h * HEAD_DIM, (h + 1) * HEAD_DIM)
        qn_ref[:, hs] = _head_norm(zq_ref[:, hs], gq)
        kn_ref[:, hs] = _head_norm(zk_ref[:, hs], gk)
    kiw = zkiw_ref[:, 0:LANES]
    lane = lax.broadcasted_iota(I32, kiw.shape, 1)
    kiw_ref[...] = _kidx_norm(kiw, gki_ref[...]) + jnp.where(
        (lane >= IDX_DIM) & (lane < IDX_DIM + N_HEADS), kiw * IDX_WEIGHT_SCALE, 0.0)


def _sample_norms(z, gq, gk, gki_pad):
    zspec = lambda cb, w: pl.BlockSpec((DEC_BATCH, w), lambda i: (SAMPLE_ROW_BLOCK, cb * COL // w))
    gspec = pl.BlockSpec((1, LANES), lambda i: (0, 0))
    return pl.pallas_call(
        _sample_norm_kernel,
        grid=(1,),
        in_specs=[zspec(CB_Q, WIDTH), zspec(CB_K, WIDTH), zspec(CB_KIW, COL), gspec, gspec, gspec],
        out_specs=[pl.BlockSpec((DEC_BATCH, WIDTH), lambda i: (0, 0)),
                   pl.BlockSpec((DEC_BATCH, WIDTH), lambda i: (0, 0)),
                   pl.BlockSpec((DEC_BATCH, LANES), lambda i: (0, 0))],
        out_shape=[jax.ShapeDtypeStruct((DEC_BATCH, WIDTH), F32),
                   jax.ShapeDtypeStruct((DEC_BATCH, WIDTH), F32),
                   jax.ShapeDtypeStruct((DEC_BATCH, LANES), F32)],
        compiler_params=_cparams(("arbitrary",)),
        name="sample_norms",
    )(z, z, z, gq, gk, gki_pad)


KEY_ROWS = N_PAGES + SUBLANES


def _select_row(x, b):
    sub = lax.broadcasted_iota(I32, x.shape, 0)
    return jnp.sum(jnp.where(sub == b, x, 0.0), axis=0, keepdims=True)


def _sample_topk_kernel(pt_ref, zqi_ref, kiw_ref, cache_ref, sel_ref, buf_ref, sem_ref, key_ref,
                        pos_ref, cut_ref, *, layer):
    b = pl.program_id(0)

    def page_copy(p):
        return pltpu.make_async_copy(cache_ref.at[layer, pt_ref[b, p]], buf_ref.at[p], sem_ref.at[0])

    def start(p, c):
        page_copy(p).start()
        return c

    lax.fori_loop(0, N_PAGES, start, 0)

    qrow = _select_row(zqi_ref[...], b)
    head = lax.broadcasted_iota(I32, (N_HEADS, N_HEADS * IDX_DIM), 0)
    col = lax.broadcasted_iota(I32, (N_HEADS, N_HEADS * IDX_DIM), 1)
    qm = jnp.where(col // IDX_DIM == head, jnp.broadcast_to(qrow, head.shape), 0.0)
    q128 = qm[:, 0:LANES]
    for p in range(1, N_HEADS * IDX_DIM // LANES):
        q128 = q128 + qm[:, p * LANES:(p + 1) * LANES]
    lane8 = lax.broadcasted_iota(I32, (N_HEADS, LANES), 1)
    q8 = jnp.where(lane8 < IDX_DIM, q128 + pltpu.roll(q128, IDX_DIM, axis=1), 0.0)
    qT = _rows_to_cols(q8)[0:IDX_DIM, :]

    kiw_row = _select_row(kiw_ref[...], b)
    r8 = lax.broadcasted_iota(I32, (N_HEADS, LANES), 0)
    wcol = jnp.sum(jnp.where(lane8 == IDX_DIM + r8, jnp.broadcast_to(kiw_row, (N_HEADS, LANES)), 0.0),
                   axis=-1, keepdims=True)
    knew = jnp.where(lane8[0:1] < IDX_DIM, kiw_row, 0.0)
    sn = jnp.sum(q8 * knew, axis=-1, keepdims=True)
    s_new = jnp.sum(jnp.maximum(sn, 0.0) * wcol, axis=0, keepdims=True)

    def wait(p, c):
        page_copy(p).wait()
        return c

    lax.fori_loop(0, N_PAGES, wait, 0)

    qcols = [jnp.broadcast_to(qT[:, h:h + 1], (IDX_DIM, PAGE_SIZE))[None] for h in range(N_HEADS)]

    def page_group(g, carry):
        r0 = pl.multiple_of(g * SUBLANES, SUBLANES)
        pages = buf_ref[pl.ds(r0, SUBLANES)]
        s = jnp.zeros((SUBLANES, PAGE_SIZE), F32)
        for h in range(N_HEADS):
            sh = jnp.sum(pages * qcols[h], axis=1)
            s = s + jnp.maximum(sh, 0.0) * wcol[h:h + 1, :]
        key_ref[b, pl.ds(r0, SUBLANES), :] = _sortable_key(s)
        return carry

    lax.fori_loop(0, N_PAGES // SUBLANES, page_group, 0)
    row_t = lax.broadcasted_iota(I32, (SUBLANES, LANES), 0)
    lane_t = lax.broadcasted_iota(I32, (SUBLANES, LANES), 1)
    key_ref[b, pl.ds(N_PAGES, SUBLANES), :] = jnp.where(
        (row_t == 0) & (lane_t == 0),
        _sortable_key(jnp.broadcast_to(s_new, (SUBLANES, LANES))), INT_MIN)

    @pl.when(b == DEC_BATCH - 1)
    def _():
        shape = (DEC_BATCH, KEY_ROWS, LANES)
        pos = lax.broadcasted_iota(I32, shape, 1) * LANES + lax.broadcasted_iota(I32, shape, 2)

        def count(indicator):
            x = indicator(key_ref[...])
            return jnp.sum(jnp.sum(x, axis=1), axis=1, keepdims=True)

        cnt = count(lambda key: jnp.where(key >= 0, 1.0, 0.0))
        tau0 = jnp.where(cnt >= TOPK, 0, INT_MIN).astype(I32)

        def bit_step(i, tau):
            cand = tau + jnp.left_shift(jnp.int32(1), 30 - i)
            cnt = count(lambda key: jnp.where(key >= cand[:, :, None], 1.0, 0.0))
            return jnp.where(cnt >= TOPK, cand, tau)

        tau = lax.fori_loop(0, 31, bit_step, tau0)[:, :, None]

        need = TOPK - count(lambda key: jnp.where(key > tau, 1.0, 0.0))
        n_eq = count(lambda key: jnp.where(key == tau, 1.0, 0.0))
        cut_ref[...] = jnp.full(cut_ref.shape, BIG_IDX, I32)

        @pl.when(jnp.max(n_eq - need) > 0.0)
        def _():
            def idx_step(i, c):
                trial = c + jnp.left_shift(jnp.int32(1), 14 - i)
                cnt = count(lambda key: jnp.where(key == tau, jnp.where(pos < trial[:, :, None], 1.0, 0.0), 0.0))
                return jnp.where(cnt < need, trial, c)
            c = lax.fori_loop(0, 15, idx_step, jnp.zeros((DEC_BATCH, 1), I32))
            cut_ref[...] = jnp.broadcast_to(c, cut_ref.shape)

        cut = cut_ref[:, 0:1][:, :, None]
        key = key_ref[...]
        chosen = jnp.where(key > tau, pos,
                           jnp.where(key == tau, jnp.where(pos <= cut, pos, BIG_IDX), BIG_IDX))
        pos_ref[...] = chosen.astype(F32)

        out_lane = lax.broadcasted_iota(I32, (DEC_BATCH, TOPK), 1)

        def pick(it, carry):
            sel, last = carry
            p = pos_ref[...]
            first = jnp.min(jnp.min(jnp.where(p > last[:, :, None], p, float(BIG_IDX)), axis=1),
                            axis=1, keepdims=True)
            return jnp.where(out_lane == it, first.astype(I32), sel), first

        sel, _ = lax.fori_loop(0, TOPK, pick, (jnp.zeros((DEC_BATCH, TOPK), I32),
                                               jnp.full((DEC_BATCH, 1), -1.0, F32)))
        sel_ref[...] = sel


def _sample_topk(page_table, z, kiw_s, cache_kidx, layer):
    grid_spec = pltpu.PrefetchScalarGridSpec(
        num_scalar_prefetch=1,
        grid=(DEC_BATCH,),
        in_specs=[pl.BlockSpec((DEC_BATCH, COL), lambda b, pt: (SAMPLE_ROW_BLOCK, CB_QI)),
                  pl.BlockSpec((DEC_BATCH, LANES), lambda b, pt: (0, 0)),
                  pl.BlockSpec(memory_space=pl.ANY)],
        out_specs=pl.BlockSpec((DEC_BATCH, TOPK), lambda b, pt: (0, 0)),
        scratch_shapes=[pltpu.VMEM((N_PAGES, IDX_DIM, PAGE_SIZE), F32),
                        pltpu.SemaphoreType.DMA((1,)),
                        pltpu.VMEM((DEC_BATCH, KEY_ROWS, LANES), I32),
                        pltpu.VMEM((DEC_BATCH, KEY_ROWS, LANES), F32),
                        pltpu.VMEM((DEC_BATCH, LANES), I32)],
    )
    cache_t = jnp.swapaxes(cache_kidx, 2, 3)
    return pl.pallas_call(
        functools.partial(_sample_topk_kernel, layer=layer),
        grid_spec=grid_spec,
        out_shape=jax.ShapeDtypeStruct((DEC_BATCH, TOPK), I32),
        compiler_params=_cparams(("arbitrary",)),
        name="sample_topk",
    )(page_table, z, kiw_s, cache_t)


def _sample_attn_kernel(sel_s_ref, pt_ref, sel_ref, qn_ref, kn_ref, zv_ref, rbT_ref,
                        ck_ref, cv_ref, oa_any_ref, o_ref, kbuf_ref, vbuf_ref, sem_ref, rows_ref,
                        *, layer):
    del oa_any_ref
    b = pl.program_id(0)

    def copies(j):
        sp = jnp.minimum(sel_s_ref[b, j], PAST_LEN - 1)
        page = pt_ref[b, sp // PAGE_SIZE]
        off = sp % PAGE_SIZE
        dst = pl.ds(pl.multiple_of(j * N_HEADS, N_HEADS), N_HEADS)
        return (pltpu.make_async_copy(ck_ref.at[layer, page, off], kbuf_ref.at[dst], sem_ref.at[0]),
                pltpu.make_async_copy(cv_ref.at[layer, page, off], vbuf_ref.at[dst], sem_ref.at[1]))

    def start(j, c):
        ck, cv = copies(j)
        ck.start()
        cv.start()
        return c

    lax.fori_loop(0, TOPK, start, 0)

    def heads_on_rows(x):
        row = _select_row(x, b)
        return jnp.concatenate([row[:, h * HEAD_DIM:(h + 1) * HEAD_DIM] for h in range(N_HEADS)],
                               axis=0)

    q = heads_on_rows(qn_ref[...])
    k_new = heads_on_rows(kn_ref[...])
    v_new = heads_on_rows(zv_ref[...])
    sel_b = jnp.sum(jnp.where(lax.broadcasted_iota(I32, (DEC_BATCH, TOPK), 0) == b, sel_ref[...], 0),
                    axis=0, keepdims=True)
    is_new = sel_b >= PAST_LEN
    bucket = _t5_bucket(jnp.maximum(PAST_LEN - sel_b, 0))
    bias = jnp.zeros((N_HEADS, TOPK), F32)
    for beta in range(NUM_BUCKETS):
        bias = jnp.where(bucket == beta, rbT_ref[:, beta:beta + 1], bias)

    def wait(j, c):
        ck, cv = copies(j)
        ck.wait()
        cv.wait()
        return c

    lax.fori_loop(0, TOPK, wait, 0)

    nt = (((1,), (1,)), ((), ()))
    hrow = lax.broadcasted_iota(I32, (N_HEADS, TOPK), 0)
    qb16 = q.astype(BF16)
    logits = jnp.zeros((N_HEADS, TOPK), F32)
    for h in range(N_HEADS):
        kh = kbuf_ref[pl.ds(h, TOPK, stride=N_HEADS), :].astype(BF16)
        res = lax.dot_general(qb16, kh, nt, preferred_element_type=F32)
        logits = jnp.where(hrow == h, res, logits)
    logit_new = jnp.sum(qb16.astype(F32) * k_new.astype(BF16).astype(F32), axis=-1, keepdims=True)
    logits = jnp.where(is_new, logit_new, logits) * ATTN_SCALE + bias
    m = jnp.max(logits, axis=1, keepdims=True)
    e = jnp.exp(logits - m)
    p = e / jnp.sum(e, axis=1, keepdims=True)
    p_new = jnp.sum(jnp.where(is_new, p, 0.0), axis=1, keepdims=True)
    p_mm = jnp.where(is_new, 0.0, p).astype(BF16)
    hrow_o = lax.broadcasted_iota(I32, (N_HEADS, HEAD_DIM), 0)
    o = jnp.zeros((N_HEADS, HEAD_DIM), F32)
    for h in range(N_HEADS):
        vh = vbuf_ref[pl.ds(h, TOPK, stride=N_HEADS), :].astype(BF16)
        res = jnp.dot(p_mm, vh, preferred_element_type=F32)
        o = jnp.where(hrow_o == h, res, o)
    o = o + p_new.astype(BF16).astype(F32) * v_new.astype(BF16).astype(F32)
    orow = jnp.concatenate([o[h:h + 1, :] for h in range(N_HEADS)], axis=1)

    @pl.when(b == 0)
    def _():
        rows_ref[...] = jnp.zeros(rows_ref.shape, F32)

    sub = lax.broadcasted_iota(I32, (DEC_BATCH, WIDTH), 0)
    rows_ref[...] = jnp.where(sub == b, jnp.broadcast_to(orow, (DEC_BATCH, WIDTH)), rows_ref[...])

    @pl.when(b == DEC_BATCH - 1)
    def _():
        pad = jnp.zeros((LANES - DEC_BATCH, WIDTH), F32)
        o_ref[...] = jnp.concatenate([rows_ref[...], pad], axis=0).astype(o_ref.dtype)


def _sample_attn(sel, page_table, qn_s, kn_s, z, rbT, cache_k, cache_v, oa_all, layer):
    grid_spec = pltpu.PrefetchScalarGridSpec(
        num_scalar_prefetch=2,
        grid=(DEC_BATCH,),
        in_specs=[pl.BlockSpec((DEC_BATCH, TOPK), lambda b, s, pt: (0, 0)),
                  pl.BlockSpec((DEC_BATCH, WIDTH), lambda b, s, pt: (0, 0)),
                  pl.BlockSpec((DEC_BATCH, WIDTH), lambda b, s, pt: (0, 0)),
                  pl.BlockSpec((DEC_BATCH, WIDTH), lambda b, s, pt: (SAMPLE_ROW_BLOCK, CB_V * COL // WIDTH)),
                  pl.BlockSpec((N_HEADS, NUM_BUCKETS), lambda b, s, pt: (0, 0)),
                  pl.BlockSpec(memory_space=pl.ANY),
                  pl.BlockSpec(memory_space=pl.ANY),
                  pl.BlockSpec(memory_space=pl.ANY)],
        out_specs=pl.BlockSpec((LANES, WIDTH), lambda b, s, pt: (SAMPLE_TILE, 0)),
        scratch_shapes=[pltpu.VMEM((TOPK * N_HEADS, HEAD_DIM), F32),
                        pltpu.VMEM((TOPK * N_HEADS, HEAD_DIM), F32),
                        pltpu.SemaphoreType.DMA((2,)),
                        pltpu.VMEM((DEC_BATCH, WIDTH), F32)],
    )
    return pl.pallas_call(
        functools.partial(_sample_attn_kernel, layer=layer),
        grid_spec=grid_spec,
        out_shape=jax.ShapeDtypeStruct((M_ALL, WIDTH), BF16),
        input_output_aliases={9: 0},
        compiler_params=_cparams(("arbitrary",)),
        name="sample_attn",
    )(sel, page_table, sel, qn_s, kn_s, z, rbT, cache_k, cache_v, oa_all)


KIW_PAD = COL - IDX_DIM - N_HEADS


def _w_in_prep_kernel(w_ref, o_ref):
    j = pl.program_id(1)
    x = w_ref[0]
    row = lax.broadcasted_iota(I32, x.shape, 0)
    o_ref[...] = jnp.where((j != CB_KIW) | (row < IDX_DIM + N_HEADS), x, 0.0).astype(BF16)


def _permute_w_in(w_in):
    depth = w_in.shape[0]
    wt = jnp.swapaxes(w_in, 1, 2)

    def src_rows(l, j):
        return l, pl.multiple_of(jnp.where(j <= CB_KIW, j * COL, j * COL - KIW_PAD), SUBLANES), 0

    return pl.pallas_call(
        _w_in_prep_kernel,
        grid=(depth, NP_IN // COL),
        in_specs=[pl.BlockSpec((pl.Element(1), pl.Element(COL), pl.Element(D_MODEL)), src_rows)],
        out_specs=pl.BlockSpec((None, COL, D_MODEL), lambda l, j: (l, j, 0)),
        out_shape=jax.ShapeDtypeStruct((depth, NP_IN, D_MODEL), BF16),
        compiler_params=_cparams(("arbitrary", "arbitrary")),
        name="w_in_prep",
    )(wt)


def _stack_rows(prompt_rows, sample_rows):
    pad = jnp.zeros((M_ALL - N_PROMPT - DEC_BATCH, prompt_rows.shape[1]), prompt_rows.dtype)
    return jnp.concatenate([prompt_rows, sample_rows.astype(prompt_rows.dtype), pad], axis=0)


def kernel(x_prompt, x_sample, cache_k, cache_v, cache_kidx, state_hgrn, page_table, norm_mix_g, w_in,
           q_norm_g, k_norm_g, kidx_norm_g, rel_bias, hgrn_lb, hgrn_out_g, w_up_a, w_up_b, w_out,
           norm_ffn_g, w_ffn_gate, w_ffn_up, w_ffn_down):
    w_in_p = _permute_w_in(w_in)
    w_up_a, w_up_b, w_out = (w.astype(BF16) for w in (w_up_a, w_up_b, w_out))
    w_ffn_gate, w_ffn_up, w_ffn_down = (w.astype(BF16) for w in (w_ffn_gate, w_ffn_up, w_ffn_down))
    lower = _lower_bounds(hgrn_lb)
    bias = _bias_tiles(rel_bias)
    rbT = rel_bias.T
    gki_pad = jnp.pad(kidx_norm_g, ((0, 0), (0, LANES - IDX_DIM)))

    x = _stack_rows(x_prompt.reshape(N_PROMPT, D_MODEL), x_sample.reshape(DEC_BATCH, D_MODEL))
    stacked = (jnp.zeros((DEPTH, N_PROMPT, WIDTH), F32), jnp.zeros((DEPTH, N_PROMPT, WIDTH), F32),
               jnp.zeros((DEPTH, N_PROMPT, LANES), F32))
    outs = [[] for _ in range(5)]
    for l in range(DEPTH):
        gq = q_norm_g[l].reshape(1, HEAD_DIM)
        gk = k_norm_g[l].reshape(1, HEAD_DIM)
        gki = gki_pad[l].reshape(1, LANES)
        og = hgrn_out_g[l].reshape(1, LANES)
        lb3 = lower[l].reshape(N_HEADS, 1, LANES)

        xn = _rmsnorm(x, norm_mix_g[l])
        z = _in_proj(xn, w_in_p, l)

        qT, qiT, wT, knb, vT, *stacked = _post(z, gq, gk, gki, l, stacked)
        oa = _attn_prompt(qT, qiT, wT, knb, vT, stacked[2], bias, l)
        qn_s, kn_s, kiw_s = _sample_norms(z, gq, gk, gki)
        sel = _sample_topk(page_table, z, kiw_s, cache_kidx, l)
        oa = _sample_attn(sel, page_table, qn_s, kn_s, z, rbT, cache_k, cache_v, oa, l)

        ob, st_p = _hgrn_prompt(z, lb3, og)
        ob, st_s = _hgrn_sample(z, lb3, og, state_hgrn, ob, l)

        mixed = _merge(oa, ob, w_up_a, w_up_b, z, l)
        x = _residual_proj(mixed, w_out, x, l, TM, "out_proj")
        hn = _rmsnorm(x, norm_ffn_g[l])
        x = _residual_proj(_ffn_up(hn, w_ffn_gate, w_ffn_up, l), w_ffn_down, x, l, TM_DOWN, "ffn_down")

        v_cols = slice(CB_V * COL, CB_V * COL + WIDTH)
        outs[0].append(st_p)
        outs[1].append(kn_s.reshape(DEC_BATCH, 1, N_HEADS, HEAD_DIM))
        outs[2].append(z[N_PROMPT:N_PROMPT + DEC_BATCH, v_cols].reshape(DEC_BATCH, 1, N_HEADS, HEAD_DIM))
        outs[3].append(kiw_s[:, :IDX_DIM].reshape(DEC_BATCH, 1, IDX_DIM))
        outs[4].append(st_s)

    k_all, v_all, ki_all = stacked
    y_prompt = x[:N_PROMPT].reshape(BATCH, SEQ, D_MODEL)
    y_sample = x[N_PROMPT:N_PROMPT + DEC_BATCH].reshape(DEC_BATCH, 1, D_MODEL)
    st_p, k_s, v_s, ki_s, st_s = (jnp.stack(o) for o in outs)
    return (y_prompt, y_sample,
            k_all.reshape(DEPTH, BATCH, SEQ, N_HEADS, HEAD_DIM),
            v_all.reshape(DEPTH, BATCH, SEQ, N_HEADS, HEAD_DIM),
            ki_all[:, :, :IDX_DIM].reshape(DEPTH, BATCH, SEQ, IDX_DIM),
            st_p, k_s, v_s, ki_s, st_s)
```
